```python
import math
import jax
import jax.numpy as jnp
from jax import lax
import numpy as np

D_MODEL = 4096
BATCH = 2
SEQ = 4096
DEPTH = 4
DEC_BATCH = 32
DEC_SEQ = 64
PAST_LEN = 1024

CHUNK = 64
GDN_HEADS = 16
GDN_DK = 128
GDN_DV = 128
GDN_CONV = 4
GDN_QK_W = GDN_HEADS * GDN_DK
GDN_V_W = GDN_HEADS * GDN_DV
GDN_CONV_CH = 2 * GDN_QK_W + GDN_V_W
ML_HEADS = 4
ML_DK = 256
ML_DV = 512
RET_HEADS = 8
RET_DK = 128
RET_DV = 256
ROPE_BASE = 10000.0
N_BRANCH = 3
BRANCH_W = 2048
D_FF = 11008
IN_SPLITS = (GDN_CONV_CH, GDN_V_W, GDN_HEADS, GDN_HEADS,
             ML_HEADS * ML_DK, ML_HEADS * ML_DK, ML_HEADS * ML_DV, ML_HEADS * ML_DV, ML_HEADS, ML_HEADS,
             RET_HEADS * RET_DK, RET_HEADS * RET_DK, RET_HEADS * RET_DV, RET_HEADS * RET_DV,
             N_BRANCH * D_MODEL)
IN_W = sum(IN_SPLITS)
DEEPNORM_ALPHA = (2 * DEPTH) ** 0.25
DEEPNORM_BETA = (8 * DEPTH) ** -0.25
LN_EPS = 1e-5

kernel_name = 'hybrid_stream_gdn_mlstm_retention'


def _split(h, sizes):
    return jnp.split(h, np.cumsum(sizes)[:-1].tolist(), axis=-1)


def _layernorm(x, g, b):
    x32 = x.astype(jnp.float32)
    mu = jnp.mean(x32, -1, keepdims=True)
    var = jnp.mean(jnp.square(x32 - mu), -1, keepdims=True)
    return ((x32 - mu) * lax.rsqrt(var + LN_EPS) * g + b).astype(x.dtype)


def _l2norm(x):
    return x * lax.rsqrt(jnp.sum(jnp.square(x), -1, keepdims=True) + 1e-6)


def _rms_head(o, g):
    return o * lax.rsqrt(jnp.mean(jnp.square(o), -1, keepdims=True) + 1e-6) * g


def _ln_head(o, g):
    mu = jnp.mean(o, -1, keepdims=True)
    var = jnp.mean(jnp.square(o - mu), -1, keepdims=True)
    return (o - mu) * lax.rsqrt(var + LN_EPS) * g


def _rotary(x, pos):
    half = x.shape[-1] // 2
    freq = ROPE_BASE ** (-jnp.arange(half, dtype=jnp.float32) / half)
    ang = pos[:, None] * freq[None, :]
    cos = jnp.cos(ang)[None, :, None, :]
    sin = jnp.sin(ang)[None, :, None, :]
    x1, x2 = x[..., :half], x[..., half:]
    return jnp.concatenate([x1 * cos - x2 * sin, x1 * sin + x2 * cos], axis=-1)


def _causal_conv(u, buf, w):
    T = u.shape[1]
    full = jnp.concatenate([buf.astype(u.dtype), u], axis=1)
    out = full[:, 0:T] * w[0]
    for j in range(1, GDN_CONV):
        out = out + full[:, j:j + T] * w[j]
    return out, full[:, -(GDN_CONV - 1):]


def _chunk_len(T):
    return CHUNK if T % CHUNK == 0 else T


def _to_chunks(a, L):
    B, T = a.shape[:2]
    a = a.reshape((B, T // L, L) + a.shape[2:])
    return jnp.moveaxis(a, 2, 3)


def _from_chunks(o):
    NC, B, H, L, d = o.shape
    return o.transpose(1, 0, 3, 2, 4).reshape(B, NC * L, H, d)


def _masks(L):
    incl = jnp.tril(jnp.ones((L, L), bool))
    strict = jnp.tril(jnp.ones((L, L), bool), -1)
    return incl, strict


def _gated_delta(q, k, v, g, beta, S0):
    T, dv = q.shape[1], v.shape[-1]
    L = _chunk_len(T)
    incl, strict = _masks(L)
    qc, kc, vc = _to_chunks(q, L), _to_chunks(k, L), _to_chunks(v, L)
    gc, bc = _to_chunks(g, L), _to_chunks(beta, L)
    G = jnp.cumsum(gc, axis=-1)
    gam = jnp.exp(jnp.where(incl, G[..., :, None] - G[..., None, :], -jnp.inf))
    kk = jnp.einsum('bnhtk,bnhsk->bnhts', kc, kc)
    A = jnp.where(strict, bc[..., :, None] * kk * gam, 0.0)
    rhs = jnp.concatenate([bc[..., None] * vc, (bc * jnp.exp(G))[..., None] * kc], axis=-1)
    X = lax.linalg.triangular_solve(A, rhs, left_side=True, lower=True, unit_diagonal=True)
    U, W = X[..., :dv], X[..., dv:]
    qk = jnp.einsum('bnhtk,bnhsk->bnhts', qc, kc) * gam
    G_last = G[..., -1]
    q_dec = qc * jnp.exp(G)[..., None]
    k_dec = kc * jnp.exp(G_last[..., None] - G)[..., None]

    def step(S, inp):
        U_, W_, qk_, qd_, kd_, gl_ = inp
        delta = U_ - jnp.einsum('bhlk,bhkv->bhlv', W_, S)
        o = jnp.einsum('bhlk,bhkv->bhlv', qd_, S) + jnp.einsum('bhts,bhsv->bhtv', qk_, delta)
        S = jnp.exp(gl_)[..., None, None] * S + jnp.einsum('bhlk,bhlv->bhkv', kd_, delta)
        return S, o

    xs = (jnp.moveaxis(U, 1, 0), jnp.moveaxis(W, 1, 0), jnp.moveaxis(qk, 1, 0),
          jnp.moveaxis(q_dec, 1, 0), jnp.moveaxis(k_dec, 1, 0), jnp.moveaxis(G_last, 1, 0))
    S, o = lax.scan(step, S0, xs)
    return _from_chunks(o), S


def _mlstm(q, k, v, ig, lf, C0, n0, m0):
    T = q.shape[1]
    L = _chunk_len(T)
    incl, _ = _masks(L)
    qc, kc, vc = _to_chunks(q, L), _to_chunks(k, L), _to_chunks(v, L)
    igc, lfc = _to_chunks(ig, L), _to_chunks(lf, L)
    F = jnp.cumsum(lfc, axis=-1)
    logD = jnp.where(incl, F[..., :, None] - F[..., None, :] + igc[..., None, :], -jnp.inf)
    m_intra = jnp.max(logD, axis=-1)
    qk = jnp.einsum('bnhtk,bnhsk->bnhts', qc, kc)

    def step(carry, inp):
        C, n, m = carry
        logD_, mi_, F_, qk_, q_, k_, v_ = inp
        m_t = jnp.maximum(F_ + m[..., None], mi_)
        D = jnp.exp(logD_ - m_t[..., None])
        Sm = D * qk_
        inter = jnp.exp(F_ + m[..., None] - m_t)
        num = inter[..., None] * jnp.einsum('bhlk,bhkv->bhlv', q_, C) + jnp.einsum('bhts,bhsv->bhtv', Sm, v_)
        den = inter * jnp.einsum('bhlk,bhk->bhl', q_, n) + jnp.sum(Sm, axis=-1)
        h = num / jnp.maximum(jnp.abs(den), jnp.exp(-m_t))[..., None]
        w_end = D[..., -1, :]
        dec = inter[..., -1]
        C = dec[..., None, None] * C + jnp.einsum('bhl,bhlk,bhlv->bhkv', w_end, k_, v_)
        n = dec[..., None] * n + jnp.einsum('bhl,bhlk->bhk', w_end, k_)
        return (C, n, m_t[..., -1]), h

    xs = (jnp.moveaxis(logD, 1, 0), jnp.moveaxis(m_intra, 1, 0), jnp.moveaxis(F, 1, 0),
          jnp.moveaxis(qk, 1, 0), jnp.moveaxis(qc, 1, 0), jnp.moveaxis(kc, 1, 0), jnp.moveaxis(vc, 1, 0))
    (C, n, m), h = lax.scan(step, (C0, n0, m0), xs)
    return _from_chunks(h), C, n, m


def _retention(q, k, v, S0):
    T, H = q.shape[1], q.shape[2]
    L = _chunk_len(T)
    incl, _ = _masks(L)
    lg = jnp.log(1.0 - 2.0 ** (-5.0 - jnp.arange(H, dtype=jnp.float32)))
    j = jnp.arange(L, dtype=jnp.float32)
    expo = lg[:, None, None] * (j[:, None] - j[None, :])
    D = jnp.where(incl, jnp.exp(jnp.where(incl, expo, 0.0)), 0.0)
    qc, kc, vc = _to_chunks(q, L), _to_chunks(k, L), _to_chunks(v, L)
    intra = jnp.einsum('bnhts,bnhsv->bnhtv', jnp.einsum('bnhtk,bnhsk->bnhts', qc, kc) * D, vc)
    q_in = qc * jnp.exp(lg[:, None] * (j + 1.0))[..., None]
    k_st = kc * jnp.exp(lg[:, None] * (L - 1.0 - j))[..., None]
    gL = jnp.exp(lg * L)

    def step(S, inp):
        intra_, q_, k_, v_ = inp
        o = intra_ + jnp.einsum('bhlk,bhkv->bhlv', q_, S)
        S = gL[:, None, None] * S + jnp.einsum('bhlk,bhlv->bhkv', k_, v_)
        return S, o

    xs = (jnp.moveaxis(intra, 1, 0), jnp.moveaxis(q_in, 1, 0), jnp.moveaxis(k_st, 1, 0), jnp.moveaxis(vc, 1, 0))
    S, o = lax.scan(step, S0, xs)
    return _from_chunks(o), S


def _layer(x, st, pos0, w_in, b_in, conv_w, A_log, dt_bias, gdn_g, f_bias, ml_g, ret_g,
           w_branch, w_out, ln1_g, ln1_b, ln2_g, ln2_b, w_up, w_down):
    f32 = jnp.float32
    gS, gconv, mC, mn, mm, rS = st
    B, T, _ = x.shape
    h = jnp.einsum('btd,de->bte', x, w_in) + b_in
    (qkv_raw, z, b_raw, a_raw, mq, mk, mv, mo, mi, mf, rq, rk, rv, rg, gates) = _split(h, IN_SPLITS)
    qkv, gconv_new = _causal_conv(qkv_raw, gconv, conv_w)
    qkv = jax.nn.silu(qkv.astype(f32))
    gq, gk, gv = jnp.split(qkv, [GDN_QK_W, 2 * GDN_QK_W], axis=-1)
    gq = _l2norm(gq.reshape(B, T, GDN_HEADS, GDN_DK)) * GDN_DK ** -0.5
    gk = _l2norm(gk.reshape(B, T, GDN_HEADS, GDN_DK))
    gv = gv.reshape(B, T, GDN_HEADS, GDN_DV)
    beta = jax.nn.sigmoid(b_raw.astype(f32))
    g = -jnp.exp(A_log.astype(f32)) * jax.nn.softplus(a_raw.astype(f32) + dt_bias.astype(f32))
    oA, gS_new = _gated_delta(gq, gk, gv, g, beta, gS.astype(f32))
    oA = _rms_head(oA, gdn_g.astype(f32)) * jax.nn.silu(z.astype(f32)).reshape(B, T, GDN_HEADS, GDN_DV)
    oA = oA.reshape(B, T, BRANCH_W)
    mq = mq.astype(f32).reshape(B, T, ML_HEADS, ML_DK) * ML_DK ** -0.5
    mk = mk.astype(f32).reshape(B, T, ML_HEADS, ML_DK)
    mv = mv.astype(f32).reshape(B, T, ML_HEADS, ML_DV)
    lf = jax.nn.log_sigmoid(mf.astype(f32) + f_bias.astype(f32))
    oB, mC_new, mn_new, mm_new = _mlstm(mq, mk, mv, mi.astype(f32), lf,
                                        mC.astype(f32), mn.astype(f32), mm.astype(f32))
    oB = _ln_head(oB, ml_g.astype(f32).reshape(ML_HEADS, ML_DV)) * jax.nn.sigmoid(mo.astype(f32)).reshape(B, T, ML_HEADS, ML_DV)
    oB = oB.reshape(B, T, BRANCH_W)
    pos = pos0 + jnp.arange(T, dtype=f32)
    rq = _rotary(rq.astype(f32).reshape(B, T, RET_HEADS, RET_DK), pos)
    rk = _rotary(rk.astype(f32).reshape(B, T, RET_HEADS, RET_DK), pos) * RET_DK ** -0.5
    rv = rv.astype(f32).reshape(B, T, RET_HEADS, RET_DV)
    oC, rS_new = _retention(rq, rk, rv, rS.astype(f32))
    oC = _ln_head(oC, ret_g.astype(f32).reshape(RET_HEADS, RET_DV)) * jax.nn.silu(rg.astype(f32)).reshape(B, T, RET_HEADS, RET_DV)
    oC = oC.reshape(B, T, BRANCH_W)
    branches = jnp.stack([oA, oB, oC], axis=2).astype(x.dtype)
    proj = jnp.einsum('btnc,ncd->btnd', branches, w_branch)
    gate = jax.nn.sigmoid(gates.astype(f32).reshape(B, T, N_BRANCH, D_MODEL)).astype(x.dtype)
    mix = jnp.einsum('btd,de->bte', jnp.sum(gate * proj, axis=2), w_out)
    x = _layernorm(DEEPNORM_ALPHA * x + mix, ln1_g, ln1_b)
    u_gate, u_val = jnp.split(jnp.einsum('btd,df->btf', x, w_up), [D_FF], axis=-1)
    ffn = jnp.einsum('btf,fd->btd', jax.nn.silu(u_gate) * u_val, w_down)
    x = _layernorm(DEEPNORM_ALPHA * x + ffn, ln2_g, ln2_b)
    dt = x.dtype
    new = (gS_new.astype(dt), gconv_new.astype(dt), mC_new.astype(dt), mn_new.astype(dt),
           mm_new.astype(dt), rS_new.astype(dt))
    return x, new


def setup_inputs(seed: int = 0) -> dict:
    key = jax.random.key(seed)
    ks = jax.random.split(key, 32)
    f32 = jnp.float32

    def nrm(k, s):
        return jax.random.normal(k, s, f32)

    dt = jnp.exp(jax.random.uniform(ks[5], (DEPTH, GDN_HEADS), f32, math.log(1e-3), math.log(1e-1)))
    return {
        'x_prompt': nrm(ks[0], (BATCH, SEQ, D_MODEL)),
        'x_sample': nrm(ks[1], (DEC_BATCH, DEC_SEQ, D_MODEL)),
        'state_gdn_S': 0.1 * nrm(ks[2], (DEPTH, DEC_BATCH, GDN_HEADS, GDN_DK, GDN_DV)),
        'state_gdn_conv': nrm(ks[3], (DEPTH, DEC_BATCH, GDN_CONV - 1, GDN_CONV_CH)),
        'state_mlstm_C': 0.1 * nrm(ks[4], (DEPTH, DEC_BATCH, ML_HEADS, ML_DK, ML_DV)),
        'state_mlstm_n': 0.1 * nrm(ks[6], (DEPTH, DEC_BATCH, ML_HEADS, ML_DK)),
        'state_mlstm_m': nrm(ks[7], (DEPTH, DEC_BATCH, ML_HEADS)),
        'state_ret_S': 0.1 * nrm(ks[8], (DEPTH, DEC_BATCH, RET_HEADS, RET_DK, RET_DV)),
        'w_in': nrm(ks[9], (DEPTH, D_MODEL, IN_W)) * D_MODEL ** -0.5,
        'b_in': 0.01 * nrm(ks[10], (DEPTH, IN_W)),
        'gdn_conv_w': 0.5 * nrm(ks[11], (DEPTH, GDN_CONV, GDN_CONV_CH)),
        'gdn_A_log': jnp.log(jax.random.uniform(ks[12], (DEPTH, GDN_HEADS), f32, 1.0, 16.0)),
        'gdn_dt_bias': dt + jnp.log(-jnp.expm1(-dt)),
        'gdn_norm_g': 1.0 + 0.01 * nrm(ks[13], (DEPTH, GDN_DV)),
        'mlstm_f_bias': jnp.linspace(3.0, 6.0, ML_HEADS, dtype=f32)[None, :] + 0.01 * nrm(ks[14], (DEPTH, ML_HEADS)),
        'mlstm_norm_g': 1.0 + 0.01 * nrm(ks[15], (DEPTH, ML_HEADS * ML_DV)),
        'ret_norm_g': 1.0 + 0.01 * nrm(ks[16], (DEPTH, RET_HEADS * RET_DV)),
        'w_branch': nrm(ks[17], (DEPTH, N_BRANCH, BRANCH_W, D_MODEL)) * (BRANCH_W ** -0.5 * DEEPNORM_BETA),
        'w_out': nrm(ks[18], (DEPTH, D_MODEL, D_MODEL)) * (D_MODEL ** -0.5 * DEEPNORM_BETA),
        'ln1_g': 1.0 + 0.01 * nrm(ks[19], (DEPTH, D_MODEL)),
        'ln1_b': 0.01 * nrm(ks[20], (DEPTH, D_MODEL)),
        'ln2_g': 1.0 + 0.01 * nrm(ks[21], (DEPTH, D_MODEL)),
        'ln2_b': 0.01 * nrm(ks[22], (DEPTH, D_MODEL)),
        'w_ffn_up': nrm(ks[23], (DEPTH, D_MODEL, 2 * D_FF)) * (D_MODEL ** -0.5 * DEEPNORM_BETA),
        'w_ffn_down': nrm(ks[24], (DEPTH, D_FF, D_MODEL)) * (D_FF ** -0.5 * DEEPNORM_BETA),
    }


def reference(x_prompt, x_sample, state_gdn_S, state_gdn_conv, state_mlstm_C, state_mlstm_n,
              state_mlstm_m, state_ret_S, w_in, b_in, gdn_conv_w, gdn_A_log, gdn_dt_bias, gdn_norm_g,
              mlstm_f_bias, mlstm_norm_g, ret_norm_g, w_branch, w_out, ln1_g, ln1_b, ln2_g, ln2_b,
              w_ffn_up, w_ffn_down):
    dt = x_prompt.dtype
    B = x_prompt.shape[0]
    zero_state = (jnp.zeros((B, GDN_HEADS, GDN_DK, GDN_DV), dt),
                  jnp.zeros((B, GDN_CONV - 1, GDN_CONV_CH), dt),
                  jnp.zeros((B, ML_HEADS, ML_DK, ML_DV), dt),
                  jnp.zeros((B, ML_HEADS, ML_DK), dt),
                  jnp.zeros((B, ML_HEADS), dt),
                  jnp.zeros((B, RET_HEADS, RET_DK, RET_DV), dt))
    yp, ys = x_prompt, x_sample
    new_p, new_s = [], []
    for l in range(DEPTH):
        wl = (w_in[l], b_in[l], gdn_conv_w[l], gdn_A_log[l], gdn_dt_bias[l], gdn_norm_g[l],
              mlstm_f_bias[l], mlstm_norm_g[l], ret_norm_g[l], w_branch[l], w_out[l],
              ln1_g[l], ln1_b[l], ln2_g[l], ln2_b[l], w_ffn_up[l], w_ffn_down[l])
        yp, st_p = _layer(yp, zero_state, 0.0, *wl)
        st_in = (state_gdn_S[l], state_gdn_conv[l], state_mlstm_C[l], state_mlstm_n[l],
                 state_mlstm_m[l], state_ret_S[l])
        ys, st_s = _layer(ys, st_in, float(PAST_LEN), *wl)
        new_p.append(st_p)
        new_s.append(st_s)

    def stk(lst, i):
        return jnp.stack([s[i] for s in lst], axis=0)

    return (yp, ys,
            stk(new_p, 0), stk(new_p, 1), stk(new_p, 2), stk(new_p, 3), stk(new_p, 4), stk(new_p, 5),
            stk(new_s, 0), stk(new_s, 1), stk(new_s, 2), stk(new_s, 3), stk(new_s, 4), stk(new_s, 5))
```

```python
import functools

import jax
import jax.numpy as jnp
from jax import lax
from jax.experimental import pallas as pl
from jax.experimental.pallas import tpu as pltpu

_F32 = jnp.float32
_BF16 = jnp.bfloat16
_HI = lax.Precision.HIGHEST

_CHUNK = 64
_PAST_LEN = 1024
_ROPE_BASE = 10000.0
_LN_EPS = 1e-5
_LANE = 128
_SUBLANE = 8
_VMEM_LIMIT = 56 * 1024 * 1024


def _tile(dim, pref, align=_LANE):
    best = None
    t = align
    while t <= min(dim, pref):
        if dim % t == 0:
            best = t
        t += align
    return dim if best is None else best


def _params(*sem):
    return pltpu.CompilerParams(dimension_semantics=sem, vmem_limit_bytes=_VMEM_LIMIT)


def _mm_kernel(*refs, nk, has_bias, has_res, alpha):
    it = iter(refs)
    x_ref, w_ref = next(it), next(it)
    b_ref = next(it) if has_bias else None
    r_ref = next(it) if has_res else None
    o_ref = next(it)
    acc_ref = next(it) if nk > 1 else None

    def finish(v):
        if has_bias:
            v = v + b_ref[...]
        if has_res:
            v = alpha * r_ref[...] + v
        o_ref[...] = v.astype(o_ref.dtype)

    part = jnp.dot(x_ref[...], w_ref[...], preferred_element_type=_F32)
    if nk == 1:
        finish(part)
    else:
        k = pl.program_id(2)

        @pl.when(k == 0)
        def _():
            acc_ref[...] = part

        @pl.when(k > 0)
        def _():
            acc_ref[...] += part

        @pl.when(k == nk - 1)
        def _():
            finish(acc_ref[...])


def _matmul(x, w, *, bias=None, res=None, alpha=1.0, out_dtype=_F32, bm=1024, bn=1024, bk=None, name):
    M, K = x.shape
    N = w.shape[1]
    bm, bn = _tile(M, bm), _tile(N, bn)
    bk = K if bk is None else _tile(K, bk)
    nk = K // bk
    in_specs = [pl.BlockSpec((bm, bk), lambda i, j, k: (i, k)),
                pl.BlockSpec((bk, bn), lambda i, j, k: (k, j))]
    args = [x, w]
    if bias is not None:
        in_specs.append(pl.BlockSpec((1, bn), lambda i, j, k: (0, j)))
        args.append(bias.reshape(1, N).astype(_F32))
    if res is not None:
        in_specs.append(pl.BlockSpec((bm, bn), lambda i, j, k: (i, j)))
        args.append(res)
    return pl.pallas_call(
        functools.partial(_mm_kernel, nk=nk, has_bias=bias is not None, has_res=res is not None, alpha=alpha),
        grid=(M // bm, N // bn, nk),
        in_specs=in_specs,
        out_specs=pl.BlockSpec((bm, bn), lambda i, j, k: (i, j)),
        out_shape=jax.ShapeDtypeStruct((M, N), out_dtype),
        scratch_shapes=[pltpu.VMEM((bm, bn), _F32)] if nk > 1 else [],
        compiler_params=_params("parallel", "parallel", "arbitrary"),
        name=name,
    )(*args)


def _small_proj_kernel(x_ref, w_ref, b_ref, o_ref):
    o_ref[...] = jnp.dot(x_ref[...], w_ref[...], precision=_HI, preferred_element_type=_F32) + b_ref[...]


def _small_proj(x, w, b):
    M, K = x.shape
    N = w.shape[1]
    bm = _tile(M, 256)
    return pl.pallas_call(
        _small_proj_kernel,
        grid=(M // bm,),
        in_specs=[pl.BlockSpec((bm, K), lambda i: (i, 0)),
                  pl.BlockSpec((K, N), lambda i: (0, 0)),
                  pl.BlockSpec((1, N), lambda i: (0, 0))],
        out_specs=pl.BlockSpec((bm, N), lambda i: (i, 0)),
        out_shape=jax.ShapeDtypeStruct((M, N), _F32),
        compiler_params=_params("parallel"),
        name="small_proj",
    )(x, w, b.reshape(1, N))


def _ln_kernel(z_ref, g_ref, b_ref, o_ref, ob_ref):
    z = z_ref[...]
    mu = jnp.mean(z, axis=-1, keepdims=True)
    d = z - mu
    var = jnp.mean(d * d, axis=-1, keepdims=True)
    y = d * lax.rsqrt(var + _LN_EPS) * g_ref[...] + b_ref[...]
    o_ref[...] = y
    ob_ref[...] = y.astype(_BF16)


def _layernorm(z, g, b):
    M, D = z.shape
    bm = _tile(M, 256)
    return pl.pallas_call(
        _ln_kernel,
        grid=(M // bm,),
        in_specs=[pl.BlockSpec((bm, D), lambda i: (i, 0)),
                  pl.BlockSpec((1, D), lambda i: (0, 0)),
                  pl.BlockSpec((1, D), lambda i: (0, 0))],
        out_specs=[pl.BlockSpec((bm, D), lambda i: (i, 0)),
                   pl.BlockSpec((bm, D), lambda i: (i, 0))],
        out_shape=[jax.ShapeDtypeStruct((M, D), _F32), jax.ShapeDtypeStruct((M, D), _BF16)],
        compiler_params=_params("parallel"),
        name="layernorm",
    )(z, g.reshape(1, D), b.reshape(1, D))


def _merge_kernel(a_ref, b_ref, c_ref, w_ref, g_ref, o_ref, acc_ref):
    br = pl.program_id(2)
    gate = jax.nn.sigmoid(g_ref[...])

    def contrib(x_ref):
        return gate * jnp.dot(x_ref[...], w_ref[0], preferred_element_type=_F32)

    @pl.when(br == 0)
    def _():
        acc_ref[...] = contrib(a_ref)

    @pl.when(br == 1)
    def _():
        acc_ref[...] += contrib(b_ref)

    @pl.when(br == 2)
    def _():
        o_ref[...] = (acc_ref[...] + contrib(c_ref)).astype(o_ref.dtype)


def _merge(oa, ob, oc, w_branch, gates):
    M, BW = oa.shape
    D = w_branch.shape[2]
    bm, bn = _tile(M, 1024), _tile(D, 1024)
    nj = D // bn
    xspec = pl.BlockSpec((bm, BW), lambda i, j, n: (i, 0))
    return pl.pallas_call(
        _merge_kernel,
        grid=(M // bm, nj, 3),
        in_specs=[xspec, xspec, xspec,
                  pl.BlockSpec((1, BW, bn), lambda i, j, n: (n, 0, j)),
                  pl.BlockSpec((bm, bn), lambda i, j, n: (i, n * nj + j))],
        out_specs=pl.BlockSpec((bm, bn), lambda i, j, n: (i, j)),
        out_shape=jax.ShapeDtypeStruct((M, D), _BF16),
        scratch_shapes=[pltpu.VMEM((bm, bn), _F32)],
        compiler_params=_params("parallel", "parallel", "arbitrary"),
        name="branch_merge",
    )(oa, ob, oc, w_branch, gates)


def _ffn_up_kernel(x_ref, wg_ref, wv_ref, o_ref):
    x = x_ref[...]
    g = jnp.dot(x, wg_ref[...], preferred_element_type=_F32)
    v = jnp.dot(x, wv_ref[...], preferred_element_type=_F32)
    o_ref[...] = (g * jax.nn.sigmoid(g) * v).astype(o_ref.dtype)


def _ffn_up(x, wg, wv):
    M, K = x.shape
    F = wg.shape[1]
    bm, bn = _tile(M, 1024), _tile(F, 512)
    wspec = pl.BlockSpec((K, bn), lambda i, j: (0, j))
    return pl.pallas_call(
        _ffn_up_kernel,
        grid=(M // bm, F // bn),
        in_specs=[pl.BlockSpec((bm, K), lambda i, j: (i, 0)), wspec, wspec],
        out_specs=pl.BlockSpec((bm, bn), lambda i, j: (i, j)),
        out_shape=jax.ShapeDtypeStruct((M, F), _BF16),
        compiler_params=_params("parallel", "parallel"),
        name="ffn_up",
    )(x, wg, wv)


def _bf(x):
    return x.astype(_BF16)


def _dot(a, b):
    return jnp.dot(a, b, preferred_element_type=_F32)


def _dot_nt(a, b):
    return lax.dot_general(a, b, (((1,), (1,)), ((), ())), preferred_element_type=_F32)


def _dot_tn(a, b):
    return lax.dot_general(a, b, (((0,), (0,)), ((), ())), preferred_element_type=_F32)


def _dot_hi(a, b):
    return jnp.dot(a, b, precision=_HI, preferred_element_type=_F32)


def _transpose_hi(x):
    n = x.shape[1]
    eye = (lax.broadcasted_iota(jnp.int32, (n, n), 0) == lax.broadcasted_iota(jnp.int32, (n, n), 1)).astype(_F32)
    return lax.dot_general(eye, x, (((1,), (1,)), ((), ())), precision=_HI, preferred_element_type=_F32)


def _silu(x):
    return x * jax.nn.sigmoid(x)


def _softplus(x):
    return jnp.maximum(x, 0.0) + jnp.log1p(jnp.exp(-jnp.abs(x)))


def _chunk_masks():
    L = _CHUNK
    row = lax.broadcasted_iota(jnp.int32, (L, L), 0)
    col = lax.broadcasted_iota(jnp.int32, (L, L), 1)
    return row, col


def _ln_head(o, g):
    mu = jnp.mean(o, axis=-1, keepdims=True)
    d = o - mu
    var = jnp.mean(d * d, axis=-1, keepdims=True)
    return d * lax.rsqrt(var + _LN_EPS) * g


def _unit_lower_inverse(A, row, col):
    same16 = (row // 16) == (col // 16)
    same32 = (row // 32) == (col // 32)
    eye = (row == col).astype(_F32)
    D = jnp.where(same16, A, 0.0)
    D2 = _dot_hi(D, D)
    P = eye - D
    P = P + _dot_hi(P, D2)
    D4 = _dot_hi(D2, D2)
    P = P + _dot_hi(P, D4)
    D8 = _dot_hi(D4, D4)
    P = P + _dot_hi(P, D8)
    E1 = jnp.where(jnp.logical_and(same32, jnp.logical_not(same16)), A, 0.0)
    P = P - _dot_hi(P, _dot_hi(E1, P))
    E2 = jnp.where(same32, 0.0, A)
    P = P - _dot_hi(P, _dot_hi(E2, P))
    return P


class _Steps:
    def __init__(self, bp, tp, bs, ts):
        assert tp % _CHUNK == 0 and ts % _CHUNK == 0
        self.bp, self.bs = bp, bs
        self.ncp, self.ncs = tp // _CHUNK, ts // _CHUNK
        self.np_steps = bp * self.ncp
        self.n = self.np_steps + bs * self.ncs

    def _split(self, i):
        in_p = i < self.np_steps
        j = jnp.maximum(i - self.np_steps, 0)
        return in_p, j

    def seq(self, i):
        in_p, j = self._split(i)
        return jnp.where(in_p, i // self.ncp, self.bp + j // self.ncs)

    def chunk(self, i):
        in_p, j = self._split(i)
        return jnp.where(in_p, i % self.ncp, j % self.ncs)

    def pos_block(self, i):
        in_p, _ = self._split(i)
        return jnp.where(in_p, self.chunk(i), self.ncp + self.chunk(i))

    def first_last(self, i):
        in_p, _ = self._split(i)
        c = self.chunk(i)
        return c == 0, jnp.where(in_p, c == self.ncp - 1, c == self.ncs - 1)


def _gdn_kernel(q_ref, k_ref, v_ref, z_ref, gs_ref, cwq_ref, cwk_ref, cwv_ref, gp_ref, ng_ref,
                s0_ref, c0q_ref, c0k_ref, c0v_ref, o_ref, sout_ref, s_scr, bq, bk, bv,
                *, hpg, dk, dv, steps):
    L, P = _CHUNK, _SUBLANE
    first, last = steps.first_last(pl.program_id(1))

    @pl.when(first)
    def _():
        s_scr[...] = s0_ref[0]
        bq[0:P, :] = c0q_ref[0]
        bk[0:P, :] = c0k_ref[0]
        bv[0:P, :] = c0v_ref[0]

    bq[P:P + L, :] = q_ref[...]
    bk[P:P + L, :] = k_ref[...]
    bv[P:P + L, :] = v_ref[...]

    row, col = _chunk_masks()
    incl, strict = row >= col, row > col
    tril = incl.astype(_F32)
    gs = gs_ref[...]
    gp = gp_ref[0]
    beta_blk = jax.nn.sigmoid(gs)
    g_blk = -jnp.exp(gp[0:1, :]) * _softplus(gs + gp[1:2, :])
    G_blk = _dot_hi(tril, g_blk)
    GT = _transpose_hi(G_blk)
    ng = ng_ref[...]

    def conv(buf, cw_ref, cs):
        out = buf[P - 3:P - 3 + L, cs] * cw_ref[0:1, cs]
        for j in range(1, 4):
            out = out + buf[P - 3 + j:P - 3 + j + L, cs] * cw_ref[j:j + 1, cs]
        return out

    def l2norm(x):
        return x * lax.rsqrt(jnp.sum(x * x, axis=-1, keepdims=True) + 1e-6)

    for h in range(hpg):
        ks, vs = slice(h * dk, (h + 1) * dk), slice(h * dv, (h + 1) * dv)
        q = l2norm(_silu(conv(bq, cwq_ref, ks))) * dk ** -0.5
        k = l2norm(_silu(conv(bk, cwk_ref, ks)))
        v = _silu(conv(bv, cwv_ref, vs))
        beta = beta_blk[:, h:h + 1]
        Gc = G_blk[:, hpg + h:hpg + h + 1]
        Gr = GT[hpg + h:hpg + h + 1, :]
        gam = jnp.where(incl, jnp.exp(jnp.where(incl, Gc - Gr, 0.0)), 0.0)
        qb, kb = _bf(q), _bf(k)
        A = jnp.where(strict, beta * _dot_nt(kb, kb) * gam, 0.0)
        eG = jnp.exp(Gc)
        T = _unit_lower_inverse(A, row, col)
        U = _dot_hi(T, beta * v)
        W = _dot_hi(T, (beta * eG) * k)
        qk = _dot_nt(qb, kb) * gam
        G_last = Gc[L - 1:L, :]
        q_dec = q * eG
        k_dec = k * jnp.exp(G_last - Gc)
        S = s_scr[h]
        Sb = _bf(S)
        delta = U - _dot(_bf(W), Sb)
        db = _bf(delta)
        o = _dot(_bf(q_dec), Sb) + _dot(_bf(qk), db)
        s_scr[h] = jnp.exp(G_last) * S + _dot_tn(_bf(k_dec), db)
        o = o * lax.rsqrt(jnp.mean(o * o, axis=-1, keepdims=True) + 1e-6) * ng
        o_ref[:, vs] = (o * _silu(z_ref[:, vs])).astype(o_ref.dtype)

    bq[0:P, :] = bq[L:L + P, :]
    bk[0:P, :] = bk[L:L + P, :]
    bv[0:P, :] = bv[L:L + P, :]

    @pl.when(last)
    def _():
        sout_ref[0] = s_scr[...]


def _gdn(hg, hs, conv_w, gp, norm_g, s0, conv0, steps, hpg):
    ntok = hg.shape[0]
    nseq, H, dk, dv = s0.shape
    assert dk == dv and H % hpg == 0
    ng_ = H // hpg
    L, P = _CHUNK, _SUBLANE
    wq = hpg * dk
    seq = steps.seq
    tok = lambda off: pl.BlockSpec((L, wq), lambda g, i: (i, off + g))
    cw = lambda off: pl.BlockSpec((conv_w.shape[0], wq), lambda g, i: (0, off + g))
    c0 = lambda off: pl.BlockSpec((1, P, wq), lambda g, i: (seq(i), 0, off + g))
    sspec = pl.BlockSpec((1, hpg, dk, dv), lambda g, i: (seq(i), g, 0, 0))
    return pl.pallas_call(
        functools.partial(_gdn_kernel, hpg=hpg, dk=dk, dv=dv, steps=steps),
        grid=(ng_, steps.n),
        in_specs=[tok(0), tok(ng_), tok(2 * ng_), tok(3 * ng_),
                  pl.BlockSpec((L, _LANE), lambda g, i: (i, g)),
                  cw(0), cw(ng_), cw(2 * ng_),
                  pl.BlockSpec((1, 2, _LANE), lambda g, i: (g, 0, 0)),
                  pl.BlockSpec((1, dv), lambda g, i: (0, 0)),
                  sspec, c0(0), c0(ng_), c0(2 * ng_)],
        out_specs=[pl.BlockSpec((L, hpg * dv), lambda g, i: (i, g)), sspec],
        out_shape=[jax.ShapeDtypeStruct((ntok, H * dv), _BF16),
                   jax.ShapeDtypeStruct((nseq, H, dk, dv), _F32)],
        scratch_shapes=[pltpu.VMEM((hpg, dk, dv), _F32)] + [pltpu.VMEM((P + L, wq), _F32)] * 3,
        compiler_params=_params("parallel", "arbitrary"),
        name="gdn_mixer",
    )(hg, hg, hg, hg, hs, conv_w, conv_w, conv_w, gp, norm_g.reshape(1, dv), s0, conv0, conv0, conv0)


def _mlstm_kernel(q_ref, k_ref, v_ref, og_ref, gs_ref, fb_ref, ng_ref, c0_ref, n0_ref, m0_ref,
                  o_ref, cout_ref, nout_ref, mout_ref, c_scr, n_scr, m_scr, *, H, dk, dv, steps):
    L, P = _CHUNK, _SUBLANE
    first, last = steps.first_last(pl.program_id(0))

    @pl.when(first)
    def _():
        c_scr[...] = c0_ref[0]
        n_scr[...] = n0_ref[0]
        m0 = m0_ref[0]
        for h in range(H):
            m_scr[h] = jnp.broadcast_to(m0[:, h:h + 1], (P, _LANE))

    row, col = _chunk_masks()
    incl = row >= col
    tril = incl.astype(_F32)
    gs = gs_ref[...]
    lf_blk = -_softplus(-(gs + fb_ref[...]))
    F_blk = _dot_hi(tril, lf_blk)
    FT = _transpose_hi(F_blk)
    gsT = _transpose_hi(gs)

    for h in range(H):
        ks, vs = slice(h * dk, (h + 1) * dk), slice(h * dv, (h + 1) * dv)
        q = q_ref[:, ks] * dk ** -0.5
        k = k_ref[:, ks]
        qb, kb, vb = _bf(q), _bf(k), _bf(v_ref[:, vs])
        igc = gs[:, h:h + 1]
        Fc = F_blk[:, H + h:H + h + 1]
        logD = jnp.where(incl, Fc - FT[H + h:H + h + 1, :] + gsT[h:h + 1, :], -jnp.inf)
        m_intra = jnp.max(logD, axis=-1, keepdims=True)
        m_prev = m_scr[h][0:1, 0:1]
        m_t = jnp.maximum(Fc + m_prev, m_intra)
        Dm = jnp.exp(logD - m_t)
        Sm = Dm * _dot_nt(qb, kb)
        inter = jnp.exp(Fc + m_prev - m_t)
        C = c_scr[h]
        n = n_scr[h:h + 1, :]
        num = inter * _dot(qb, _bf(C)) + _dot(_bf(Sm), vb)
        den = inter * jnp.sum(q * n, axis=-1, keepdims=True) + jnp.sum(Sm, axis=-1, keepdims=True)
        hh = num / jnp.maximum(jnp.abs(den), jnp.exp(-m_t))
        m_last = m_t[L - 1:L, :]
        w_end = jnp.exp(Fc[L - 1:L, :] - Fc + igc - m_last)
        dec = inter[L - 1:L, :]
        wk = w_end * k
        c_scr[h] = dec * C + _dot_tn(_bf(wk), vb)
        n_scr[h:h + 1, :] = dec * n + jnp.sum(wk, axis=0, keepdims=True)
        m_scr[h] = jnp.broadcast_to(m_last, (P, _LANE))
        o_ref[:, vs] = (_ln_head(hh, ng_ref[:, vs]) * jax.nn.sigmoid(og_ref[:, vs])).astype(o_ref.dtype)

    @pl.when(last)
    def _():
        cout_ref[0] = c_scr[...]
        nout_ref[0] = n_scr[...]
        lane = lax.broadcasted_iota(jnp.int32, (1, _LANE), 1)
        m_row = jnp.zeros((1, _LANE), _F32)
        for h in range(H):
            m_row = jnp.where(lane == h, m_scr[h][0:1, :], m_row)
        mout_ref[0] = m_row


def _mlstm(hm, hs, gate_block, f_bias_row, norm_g, c0, n0, m0, steps):
    ntok = hm.shape[0]
    nseq, H, dk, dv = c0.shape
    assert dv % dk == 0 and 2 * H <= _LANE
    L = _CHUNK
    r = dv // dk
    tq = lambda j: pl.BlockSpec((L, H * dk), lambda i: (i, j))
    tv = lambda j: pl.BlockSpec((L, H * dv), lambda i: (i, j))
    seq = steps.seq
    cspec = pl.BlockSpec((1, H, dk, dv), lambda i: (seq(i), 0, 0, 0))
    nspec = pl.BlockSpec((1, H, dk), lambda i: (seq(i), 0, 0))
    mspec = pl.BlockSpec((1, 1, _LANE), lambda i: (seq(i), 0, 0))
    assert 2 % r == 0
    return pl.pallas_call(
        functools.partial(_mlstm_kernel, H=H, dk=dk, dv=dv, steps=steps),
        grid=(steps.n,),
        in_specs=[tq(0), tq(1), tv(2 // r), tv(2 // r + 1),
                  pl.BlockSpec((L, _LANE), lambda i: (i, gate_block)),
                  pl.BlockSpec((1, _LANE), lambda i: (0, 0)),
                  pl.BlockSpec((1, H * dv), lambda i: (0, 0)),
                  cspec, nspec, mspec],
        out_specs=[pl.BlockSpec((L, H * dv), lambda i: (i, 0)), cspec, nspec, mspec],
        out_shape=[jax.ShapeDtypeStruct((ntok, H * dv), _BF16),
                   jax.ShapeDtypeStruct((nseq, H, dk, dv), _F32),
                   jax.ShapeDtypeStruct((nseq, H, dk), _F32),
                   jax.ShapeDtypeStruct((nseq, 1, _LANE), _F32)],
        scratch_shapes=[pltpu.VMEM((H, dk, dv), _F32), pltpu.VMEM((H, dk), _F32),
                        pltpu.VMEM((H, _SUBLANE, _LANE), _F32)],
        compiler_params=_params("arbitrary"),
        name="mlstm_mixer",
    )(hm, hm, hm, hm, hs, f_bias_row, norm_g.reshape(1, H * dv), c0, n0, m0)


def _ret_kernel(lg_ref, q_ref, k_ref, v_ref, og_ref, cos_ref, sin_ref, ng_ref, s0_ref,
                o_ref, sout_ref, s_scr, *, H, dk, dv, steps):
    L = _CHUNK
    first, last = steps.first_last(pl.program_id(0))

    @pl.when(first)
    def _():
        s_scr[...] = s0_ref[0]

    row, col = _chunk_masks()
    incl = row >= col
    tdiff = (row - col).astype(_F32)
    jc = lax.broadcasted_iota(jnp.int32, (L, 1), 0).astype(_F32)
    cos, sin = cos_ref[...], sin_ref[...]

    def rotary(x):
        return x * cos + pltpu.roll(x, dk // 2, 1) * sin

    for h in range(H):
        ks, vs = slice(h * dk, (h + 1) * dk), slice(h * dv, (h + 1) * dv)
        lg = lg_ref[h]
        q = rotary(q_ref[:, ks])
        k = rotary(k_ref[:, ks]) * dk ** -0.5
        vb = _bf(v_ref[:, vs])
        D = jnp.where(incl, jnp.exp(jnp.where(incl, lg * tdiff, 0.0)), 0.0)
        qk = _dot_nt(_bf(q), _bf(k)) * D
        q_in = q * jnp.exp(lg * (jc + 1.0))
        k_st = k * jnp.exp(lg * (L - 1.0 - jc))
        gL = jnp.exp(lg * jnp.full((1, 1), float(L), _F32))
        S = s_scr[h]
        o = _dot(_bf(qk), vb) + _dot(_bf(q_in), _bf(S))
        s_scr[h] = gL * S + _dot_tn(_bf(k_st), vb)
        o_ref[:, vs] = (_ln_head(o, ng_ref[:, vs]) * _silu(og_ref[:, vs])).astype(o_ref.dtype)

    @pl.when(last)
    def _():
        sout_ref[0] = s_scr[...]


def _retention(hr, lg, cos_tab, sin_tab, norm_g, s0, steps):
    ntok = hr.shape[0]
    nseq, H, dk, dv = s0.shape
    assert dv % dk == 0 and 2 % (dv // dk) == 0
    L = _CHUNK
    r = dv // dk
    tq = lambda j: pl.BlockSpec((L, H * dk), lambda i: (i, j))
    tv = lambda j: pl.BlockSpec((L, H * dv), lambda i: (i, j))
    tab = pl.BlockSpec((L, dk), lambda i: (steps.pos_block(i), 0))
    sspec = pl.BlockSpec((1, H, dk, dv), lambda i: (steps.seq(i), 0, 0, 0))
    return pl.pallas_call(
        functools.partial(_ret_kernel, H=H, dk=dk, dv=dv, steps=steps),
        grid=(steps.n,),
        in_specs=[pl.BlockSpec(memory_space=pltpu.SMEM),
                  tq(0), tq(1), tv(2 // r), tv(2 // r + 1), tab, tab,
                  pl.BlockSpec((1, H * dv), lambda i: (0, 0)), sspec],
        out_specs=[pl.BlockSpec((L, H * dv), lambda i: (i, 0)), sspec],
        out_shape=[jax.ShapeDtypeStruct((ntok, H * dv), _BF16),
                   jax.ShapeDtypeStruct((nseq, H, dk, dv), _F32)],
        scratch_shapes=[pltpu.VMEM((H, dk, dv), _F32)],
        compiler_params=_params("arbitrary"),
        name="retention_mixer",
    )(lg, hr, hr, hr, hr, cos_tab, sin_tab, norm_g.reshape(1, H * dv), s0)


def _rotary_tables(steps, tp, ts, dk):
    half = dk // 2
    freq = _ROPE_BASE ** (-jnp.arange(half, dtype=_F32) / half)
    pos = jnp.concatenate([0.0 + jnp.arange(tp, dtype=_F32), float(_PAST_LEN) + jnp.arange(ts, dtype=_F32)])
    ang = pos[:, None] * freq[None, :]
    cos, sin = jnp.cos(ang), jnp.sin(ang)
    return jnp.concatenate([cos, cos], axis=1), jnp.concatenate([-sin, sin], axis=1)


def kernel(x_prompt, x_sample, state_gdn_S, state_gdn_conv, state_mlstm_C, state_mlstm_n, state_mlstm_m, state_ret_S, w_in, b_in, gdn_conv_w, gdn_A_log, gdn_dt_bias, gdn_norm_g, mlstm_f_bias, mlstm_norm_g, ret_norm_g, w_branch, w_out, ln1_g, ln1_b, ln2_g, ln2_b, w_ffn_up, w_ffn_down):
    bp, tp, D = x_prompt.shape
    bs, ts, _ = x_sample.shape
    depth = w_in.shape[0]
    alpha = (2 * depth) ** 0.25
    Hg, gdk, gdv = state_gdn_S.shape[2:]
    Hm, mdk, mdv = state_mlstm_C.shape[2:]
    Hr, rdk, rdv = state_ret_S.shape[2:]
    F = w_ffn_down.shape[1]
    steps = _Steps(bp, tp, bs, ts)
    np_tok = bp * tp

    widths = (2 * Hg * gdk + Hg * gdv, Hg * gdv, Hg, Hg,
              Hm * mdk, Hm * mdk, Hm * mdv, Hm * mdv, Hm, Hm,
              Hr * rdk, Hr * rdk, Hr * rdv, Hr * rdv, 3 * D)
    offs = [0]
    for w_ in widths:
        offs.append(offs[-1] + w_)
    o_qkv, o_z, o_b, o_a, o_mq, _, _, _, o_mi, o_mf, o_rq, _, _, _, o_gate, o_end = offs
    assert o_end == w_in.shape[2]
    qkv_w = widths[0]

    hpg = min(Hg, 4)
    ngrp = Hg // hpg
    assert 2 * hpg <= _LANE
    ns = (ngrp + 1) * _LANE

    def small_cols(vec):
        out = jnp.zeros(vec.shape[:-1] + (ns,), _F32)
        for g in range(ngrp):
            out = out.at[..., g * _LANE:g * _LANE + hpg].set(vec[..., o_b + g * hpg:o_b + (g + 1) * hpg])
            out = out.at[..., g * _LANE + hpg:g * _LANE + 2 * hpg].set(vec[..., o_a + g * hpg:o_a + (g + 1) * hpg])
        out = out.at[..., ngrp * _LANE:ngrp * _LANE + Hm].set(vec[..., o_mi:o_mi + Hm])
        out = out.at[..., ngrp * _LANE + Hm:ngrp * _LANE + 2 * Hm].set(vec[..., o_mf:o_mf + Hm])
        return out

    cos_tab, sin_tab = _rotary_tables(steps, tp, ts, rdk)
    lg = jnp.log(1.0 - 2.0 ** (-5.0 - jnp.arange(Hr, dtype=_F32)))
    Fp = -(-F // 512) * 512 if F > 512 else F

    x = jnp.concatenate([x_prompt.reshape(np_tok, D), x_sample.reshape(bs * ts, D)], axis=0)
    xb = x.astype(_BF16)
    new_p, new_s = [], []
    for l in range(depth):
        wl, bl = w_in[l], b_in[l]
        hg = _matmul(xb, wl[:, o_qkv:o_b].astype(_BF16), bias=bl[o_qkv:o_b], name="in_proj_gdn")
        hm = _matmul(xb, wl[:, o_mq:o_mi].astype(_BF16), bias=bl[o_mq:o_mi], name="in_proj_mlstm")
        hr = _matmul(xb, wl[:, o_rq:o_gate].astype(_BF16), bias=bl[o_rq:o_gate], name="in_proj_ret")
        hgate = _matmul(xb, wl[:, o_gate:].astype(_BF16), bias=bl[o_gate:], name="in_proj_gates")
        hs = _small_proj(x, small_cols(wl), small_cols(bl))

        gp = jnp.zeros((ngrp, 2, _LANE), _F32)
        gp = gp.at[:, 0, hpg:2 * hpg].set(gdn_A_log[l].reshape(ngrp, hpg))
        gp = gp.at[:, 1, hpg:2 * hpg].set(gdn_dt_bias[l].reshape(ngrp, hpg))
        s0 = jnp.concatenate([jnp.zeros((bp,) + state_gdn_S.shape[2:], _F32), state_gdn_S[l]], axis=0)
        conv0 = jnp.concatenate([jnp.zeros((bp,) + state_gdn_conv.shape[2:], _F32), state_gdn_conv[l]], axis=0)
        conv0 = jnp.pad(conv0, ((0, 0), (_SUBLANE - conv0.shape[1], 0), (0, 0)))
        oa, gS = _gdn(hg, hs, gdn_conv_w[l], gp, gdn_norm_g[l], s0, conv0, steps, hpg)
        nconv = state_gdn_conv.shape[2]
        conv_p = hg[:np_tok].reshape(bp, tp, -1)[:, tp - nconv:, :qkv_w]
        conv_s = hg[np_tok:].reshape(bs, ts, -1)[:, ts - nconv:, :qkv_w]

        fb = jnp.zeros((1, _LANE), _F32).at[0, Hm:2 * Hm].set(mlstm_f_bias[l])
        c0 = jnp.concatenate([jnp.zeros((bp,) + state_mlstm_C.shape[2:], _F32), state_mlstm_C[l]], axis=0)
        n0 = jnp.concatenate([jnp.zeros((bp,) + state_mlstm_n.shape[2:], _F32), state_mlstm_n[l]], axis=0)
        m0 = jnp.concatenate([jnp.zeros((bp, Hm), _F32), state_mlstm_m[l]], axis=0)
        m0 = jnp.pad(m0, ((0, 0), (0, _LANE - Hm))).reshape(bp + bs, 1, _LANE)
        ob, mC, mn, mm = _mlstm(hm, hs, ngrp, fb, mlstm_norm_g[l], c0, n0, m0, steps)
        mm = mm[:, 0, :Hm]

        r0 = jnp.concatenate([jnp.zeros((bp,) + state_ret_S.shape[2:], _F32), state_ret_S[l]], axis=0)
        oc, rS = _retention(hr, lg, cos_tab, sin_tab, ret_norm_g[l], r0, steps)

        merged = _merge(oa, ob, oc, w_branch[l].astype(_BF16), hgate)
        z1 = _matmul(merged, w_out[l].astype(_BF16), res=x, alpha=alpha, name="out_proj")
        x, xb = _layernorm(z1, ln1_g[l], ln1_b[l])

        wu = w_ffn_up[l]
        pad_c = ((0, 0), (0, Fp - F))
        act = _ffn_up(xb, jnp.pad(wu[:, :F].astype(_BF16), pad_c), jnp.pad(wu[:, F:].astype(_BF16), pad_c))
        wd = jnp.pad(w_ffn_down[l].astype(_BF16), ((0, Fp - F), (0, 0)))
        z2 = _matmul(act, wd, res=x, alpha=alpha, bk=2816, name="ffn_down")
        x, xb = _layernorm(z2, ln2_g[l], ln2_b[l])

        new_p.append((gS[:bp], conv_p, mC[:bp], mn[:bp], mm[:bp], rS[:bp]))
        new_s.append((gS[bp:], conv_s, mC[bp:], mn[bp:], mm[bp:], rS[bp:]))

    def stk(lst, i):
        return jnp.stack([s[i] for s in lst], axis=0)

    yp = x[:np_tok].reshape(bp, tp, D)
    ys = x[np_tok:].reshape(bs, ts, D)
    return (yp, ys,
            stk(new_p, 0), stk(new_p, 1), stk(new_p, 2), stk(new_p, 3), stk(new_p, 4), stk(new_p, 5),
            stk(new_s, 0), stk(new_s, 1), stk(new_s, 2), stk(new_s, 3), stk(new_s, 4), stk(new_s, 5))
```

```python
import functools

import jax
import jax.numpy as jnp
from jax import lax
from jax.experimental import pallas as pl
from jax.experimental.pallas import tpu as pltpu

_F32 = jnp.float32
_BF16 = jnp.bfloat16
_HI = lax.Precision.HIGHEST

_CHUNK = 64
_PAST_LEN = 1024
_ROPE_BASE = 10000.0
_LN_EPS = 1e-5
_LANE = 128
_SUBLANE = 8
_VMEM_LIMIT = 56 * 1024 * 1024


def _tile(dim, pref, align=_LANE):
    best = None
    t = align
    while t <= min(dim, pref):
        if dim % t == 0:
            best = t
        t += align
    return dim if best is None else best


def _params(*sem):
    return pltpu.CompilerParams(dimension_semantics=sem, vmem_limit_bytes=_VMEM_LIMIT)


def _bf(x):
    return x.astype(_BF16)


def _dot(a, b):
    return jnp.dot(a, b, preferred_element_type=_F32)


def _dot_nt(a, b):
    return lax.dot_general(a, b, (((1,), (1,)), ((), ())), preferred_element_type=_F32)


def _dot_tn(a, b):
    return lax.dot_general(a, b, (((0,), (0,)), ((), ())), preferred_element_type=_F32)


def _dot_hi(a, b):
    return jnp.dot(a, b, precision=_HI, preferred_element_type=_F32)


def _split(x):
    hi = x.astype(_BF16)
    return hi, (x - hi.astype(_F32)).astype(_BF16)


def _mm3(a, b):
    return _dot(a[0], b[0]) + (_dot(a[0], b[1]) + _dot(a[1], b[0]))


def _mm_kernel(*refs, nk, has_bias, has_res, alpha):
    it = iter(refs)
    x_ref, w_ref = next(it), next(it)
    b_ref = next(it) if has_bias else None
    r_ref = next(it) if has_res else None
    o_ref = next(it)
    acc_ref = next(it) if nk > 1 else None

    def finish(v):
        if has_bias:
            v = v + b_ref[...]
        if has_res:
            v = alpha * r_ref[...] + v
        o_ref[...] = v.astype(o_ref.dtype)

    part = jnp.dot(x_ref[...], w_ref[...], preferred_element_type=_F32)
    if nk == 1:
        finish(part)
    else:
        k = pl.program_id(2)

        @pl.when(k == 0)
        def _():
            acc_ref[...] = part

        @pl.when(k > 0)
        def _():
            acc_ref[...] += part

        @pl.when(k == nk - 1)
        def _():
            finish(acc_ref[...])


def _matmul(x, w, *, bias=None, res=None, alpha=1.0, out_dtype=_F32, bm=1024, bn=1024, bk=None, name):
    M, K = x.shape
    N = w.shape[1]
    bm, bn = _tile(M, bm), _tile(N, bn)
    bk = K if bk is None else _tile(K, bk)
    nk = K // bk
    in_specs = [pl.BlockSpec((bm, bk), lambda i, j, k: (i, k)),
                pl.BlockSpec((bk, bn), lambda i, j, k: (k, j))]
    args = [x, w]
    if bias is not None:
        in_specs.append(pl.BlockSpec((1, bn), lambda i, j, k: (0, j)))
        args.append(bias.reshape(1, N).astype(_F32))
    if res is not None:
        in_specs.append(pl.BlockSpec((bm, bn), lambda i, j, k: (i, j)))
        args.append(res)
    return pl.pallas_call(
        functools.partial(_mm_kernel, nk=nk, has_bias=bias is not None, has_res=res is not None, alpha=alpha),
        grid=(M // bm, N // bn, nk),
        in_specs=in_specs,
        out_specs=pl.BlockSpec((bm, bn), lambda i, j, k: (i, j)),
        out_shape=jax.ShapeDtypeStruct((M, N), out_dtype),
        scratch_shapes=[pltpu.VMEM((bm, bn), _F32)] if nk > 1 else [],
        compiler_params=_params("parallel", "parallel", "arbitrary"),
        name=name,
    )(*args)


def _small_proj_kernel(x_ref, w_ref, b_ref, o_ref):
    o_ref[...] = _mm3(_split(x_ref[...]), _split(w_ref[...])) + b_ref[...]


def _small_proj(x, w, b):
    M, K = x.shape
    N = w.shape[1]
    bm = _tile(M, 256)
    return pl.pallas_call(
        _small_proj_kernel,
        grid=(M // bm,),
        in_specs=[pl.BlockSpec((bm, K), lambda i: (i, 0)),
                  pl.BlockSpec((K, N), lambda i: (0, 0)),
                  pl.BlockSpec((1, N), lambda i: (0, 0))],
        out_specs=pl.BlockSpec((bm, N), lambda i: (i, 0)),
        out_shape=jax.ShapeDtypeStruct((M, N), _F32),
        compiler_params=_params("parallel"),
        name="small_proj",
    )(x, w, b.reshape(1, N))


def _ln_kernel(z_ref, g_ref, b_ref, o_ref, ob_ref):
    z = z_ref[...]
    mu = jnp.mean(z, axis=-1, keepdims=True)
    d = z - mu
    var = jnp.mean(d * d, axis=-1, keepdims=True)
    y = d * lax.rsqrt(var + _LN_EPS) * g_ref[...] + b_ref[...]
    o_ref[...] = y
    ob_ref[...] = y.astype(_BF16)


def _layernorm(z, g, b):
    M, D = z.shape
    bm = _tile(M, 256)
    return pl.pallas_call(
        _ln_kernel,
        grid=(M // bm,),
        in_specs=[pl.BlockSpec((bm, D), lambda i: (i, 0)),
                  pl.BlockSpec((1, D), lambda i: (0, 0)),
                  pl.BlockSpec((1, D), lambda i: (0, 0))],
        out_specs=[pl.BlockSpec((bm, D), lambda i: (i, 0)),
                   pl.BlockSpec((bm, D), lambda i: (i, 0))],
        out_shape=[jax.ShapeDtypeStruct((M, D), _F32), jax.ShapeDtypeStruct((M, D), _BF16)],
        compiler_params=_params("parallel"),
        name="layernorm",
    )(z, g.reshape(1, D), b.reshape(1, D))


def _merge_kernel(a_ref, b_ref, c_ref, w_ref, g_ref, o_ref, acc_ref):
    br = pl.program_id(2)
    gate = jax.nn.sigmoid(g_ref[...])

    def contrib(x_ref):
        return gate * jnp.dot(x_ref[...], w_ref[0], preferred_element_type=_F32)

    @pl.when(br == 0)
    def _():
        acc_ref[...] = contrib(a_ref)

    @pl.when(br == 1)
    def _():
        acc_ref[...] += contrib(b_ref)

    @pl.when(br == 2)
    def _():
        o_ref[...] = (acc_ref[...] + contrib(c_ref)).astype(o_ref.dtype)


def _merge(oa, ob, oc, w_branch, h, gate_col0):
    M, BW = oa.shape
    D = w_branch.shape[2]
    bm, bn = _tile(M, 1024), _tile(D, 1024)
    nj = D // bn
    assert gate_col0 % bn == 0
    g0 = gate_col0 // bn
    xspec = pl.BlockSpec((bm, BW), lambda i, j, n: (i, 0))
    return pl.pallas_call(
        _merge_kernel,
        grid=(M // bm, nj, 3),
        in_specs=[xspec, xspec, xspec,
                  pl.BlockSpec((1, BW, bn), lambda i, j, n: (n, 0, j)),
                  pl.BlockSpec((bm, bn), lambda i, j, n: (i, g0 + n * nj + j))],
        out_specs=pl.BlockSpec((bm, bn), lambda i, j, n: (i, j)),
        out_shape=jax.ShapeDtypeStruct((M, D), _BF16),
        scratch_shapes=[pltpu.VMEM((bm, bn), _F32)],
        compiler_params=_params("parallel", "parallel", "arbitrary"),
        name="branch_merge",
    )(oa, ob, oc, w_branch, h)


def _ffn_up_kernel(x_ref, wg_ref, wv_ref, o_ref):
    x = x_ref[...]
    g = jnp.dot(x, _bf(wg_ref[0]), preferred_element_type=_F32)
    v = jnp.dot(x, _bf(wv_ref[0]), preferred_element_type=_F32)
    o_ref[...] = (g * jax.nn.sigmoid(g) * v).astype(o_ref.dtype)


def _ffn_up(x, w_up, layer):
    M, K = x.shape
    F = w_up.shape[2] // 2
    bm, bn = _tile(M, 1024), _tile(F, 256)
    nj = F // bn
    return pl.pallas_call(
        _ffn_up_kernel,
        grid=(M // bm, nj),
        in_specs=[pl.BlockSpec((bm, K), lambda i, j: (i, 0)),
                  pl.BlockSpec((1, K, bn), lambda i, j: (layer, 0, j)),
                  pl.BlockSpec((1, K, bn), lambda i, j: (layer, 0, nj + j))],
        out_specs=pl.BlockSpec((bm, bn), lambda i, j: (i, j)),
        out_shape=jax.ShapeDtypeStruct((M, F), _BF16),
        compiler_params=_params("parallel", "parallel"),
        name="ffn_up",
    )(x, w_up, w_up)


def _transpose_hi(x):
    n = x.shape[1]
    eye = (lax.broadcasted_iota(jnp.int32, (n, n), 0) == lax.broadcasted_iota(jnp.int32, (n, n), 1)).astype(_F32)
    return lax.dot_general(eye, x, (((1,), (1,)), ((), ())), precision=_HI, preferred_element_type=_F32)


def _silu(x):
    return x * jax.nn.sigmoid(x)


def _softplus(x):
    return jnp.maximum(x, 0.0) + jnp.log1p(jnp.exp(-jnp.abs(x)))


def _chunk_masks():
    L = _CHUNK
    row = lax.broadcasted_iota(jnp.int32, (L, L), 0)
    col = lax.broadcasted_iota(jnp.int32, (L, L), 1)
    return row, col


def _ln_head(o, g):
    mu = jnp.mean(o, axis=-1, keepdims=True)
    d = o - mu
    var = jnp.mean(d * d, axis=-1, keepdims=True)
    return d * lax.rsqrt(var + _LN_EPS) * g


def _unit_lower_inverses(As, row, col):
    same16 = (row // 16) == (col // 16)
    same32 = (row // 32) == (col // 32)
    eye = (row == col).astype(_F32)
    D = [jnp.where(same16, A, 0.0) for A in As]
    Ds = [_split(d) for d in D]
    D2s = [_split(_mm3(d, d)) for d in Ds]
    P = [eye - d for d in D]
    P = [p + _mm3(_split(p), d2) for p, d2 in zip(P, D2s)]
    D4s = [_split(_mm3(d2, d2)) for d2 in D2s]
    P = [p + _mm3(_split(p), d4) for p, d4 in zip(P, D4s)]
    D8s = [_split(_mm3(d4, d4)) for d4 in D4s]
    P = [p + _mm3(_split(p), d8) for p, d8 in zip(P, D8s)]
    for off_mask in (jnp.logical_and(same32, jnp.logical_not(same16)), jnp.logical_not(same32)):
        Es = [_split(jnp.where(off_mask, A, 0.0)) for A in As]
        Ps = [_split(p) for p in P]
        Qs = [_split(_mm3(e, ps)) for e, ps in zip(Es, Ps)]
        P = [p - _mm3(ps, q) for p, ps, q in zip(P, Ps, Qs)]
    return P


class _Steps:
    def __init__(self, bp, tp, bs, ts):
        assert tp % _CHUNK == 0 and ts % _CHUNK == 0
        self.bp, self.bs = bp, bs
        self.ncp, self.ncs = tp // _CHUNK, ts // _CHUNK
        self.np_steps = bp * self.ncp
        self.n = self.np_steps + bs * self.ncs

    def in_prompt(self, i):
        return i < self.np_steps

    def _sample_step(self, i):
        return jnp.maximum(i - self.np_steps, 0)

    def seq(self, i):
        return jnp.where(self.in_prompt(i), i // self.ncp, self.bp + self._sample_step(i) // self.ncs)

    def sample_seq(self, i):
        return self._sample_step(i) // self.ncs

    def chunk(self, i):
        return jnp.where(self.in_prompt(i), i % self.ncp, self._sample_step(i) % self.ncs)

    def pos_block(self, i):
        return jnp.where(self.in_prompt(i), self.chunk(i), self.ncp + self.chunk(i))

    def first_last(self, i):
        c = self.chunk(i)
        return c == 0, jnp.where(self.in_prompt(i), c == self.ncp - 1, c == self.ncs - 1)


def _gdn_kernel(q_ref, k_ref, v_ref, z_ref, gs_ref, cwq_ref, cwk_ref, cwv_ref, gp_ref, ng_ref,
                s0_ref, c0q_ref, c0k_ref, c0v_ref, o_ref, sout_ref, s_scr, bq, bk, bv,
                *, hpg, dk, dv, steps):
    L, P = _CHUNK, _SUBLANE
    i = pl.program_id(1)
    first, last = steps.first_last(i)
    in_p = steps.in_prompt(i)
    nc = c0q_ref.shape[2]

    @pl.when(jnp.logical_and(first, in_p))
    def _():
        s_scr[...] = jnp.zeros(s_scr.shape, _F32)
        for buf in (bq, bk, bv):
            buf[0:P, :] = jnp.zeros((P, buf.shape[1]), _F32)

    @pl.when(jnp.logical_and(first, jnp.logical_not(in_p)))
    def _():
        s_scr[...] = s0_ref[0, 0]
        bq[P - nc:P, :] = c0q_ref[0, 0]
        bk[P - nc:P, :] = c0k_ref[0, 0]
        bv[P - nc:P, :] = c0v_ref[0, 0]

    bq[P:P + L, :] = q_ref[...]
    bk[P:P + L, :] = k_ref[...]
    bv[P:P + L, :] = v_ref[...]

    row, col = _chunk_masks()
    incl, strict = row >= col, row > col
    tril = incl.astype(_F32)
    gs = gs_ref[...]
    gp = gp_ref[0]
    beta_blk = jax.nn.sigmoid(gs)
    g_blk = -jnp.exp(gp[0:1, :]) * _softplus(gs + gp[1:2, :])
    G_blk = _dot_hi(tril, g_blk)
    GT = _transpose_hi(G_blk)
    ng = ng_ref[...]
    heads = range(hpg)

    def conv(buf, cw_ref, cs):
        out = buf[P - nc:P - nc + L, cs] * cw_ref[0:1, cs]
        for j in range(1, nc + 1):
            out = out + buf[P - nc + j:P - nc + j + L, cs] * cw_ref[j:j + 1, cs]
        return out

    def l2norm(x):
        return x * lax.rsqrt(jnp.sum(x * x, axis=-1, keepdims=True) + 1e-6)

    ks = [slice(h * dk, (h + 1) * dk) for h in heads]
    vs = [slice(h * dv, (h + 1) * dv) for h in heads]
    q = [l2norm(_silu(conv(bq, cwq_ref, ks[h]))) * dk ** -0.5 for h in heads]
    k = [l2norm(_silu(conv(bk, cwk_ref, ks[h]))) for h in heads]
    v = [_silu(conv(bv, cwv_ref, vs[h])) for h in heads]
    beta = [beta_blk[:, h:h + 1] for h in heads]
    Gc = [G_blk[:, hpg + h:hpg + h + 1] for h in heads]
    gam = [jnp.where(incl, jnp.exp(jnp.where(incl, Gc[h] - GT[hpg + h:hpg + h + 1, :], 0.0)), 0.0) for h in heads]
    kb = [_bf(k[h]) for h in heads]
    kq = [_dot_nt(jnp.concatenate([kb[h], _bf(q[h])], axis=0), kb[h]) for h in heads]
    A = [jnp.where(strict, beta[h] * kq[h][0:L] * gam[h], 0.0) for h in heads]
    qk = [_bf(kq[h][L:2 * L] * gam[h]) for h in heads]
    eG = [jnp.exp(Gc[h]) for h in heads]
    T = _unit_lower_inverses(A, row, col)
    rhs = [jnp.concatenate([beta[h] * v[h], (beta[h] * eG[h]) * k[h]], axis=1) for h in heads]
    UW = [_mm3(_split(T[h]), _split(rhs[h])) for h in heads]
    G_last = [Gc[h][L - 1:L, :] for h in heads]
    S = [s_scr[h] for h in heads]
    Sb = [_bf(S[h]) for h in heads]
    WQ = [_dot(jnp.concatenate([_bf(UW[h][:, dv:]), _bf(q[h] * eG[h])], axis=0), Sb[h]) for h in heads]
    db = [_bf(UW[h][:, 0:dv] - WQ[h][0:L]) for h in heads]
    o = [WQ[h][L:2 * L] + _dot(qk[h], db[h]) for h in heads]
    for h in heads:
        k_dec = k[h] * jnp.exp(G_last[h] - Gc[h])
        s_scr[h] = jnp.exp(G_last[h]) * S[h] + _dot_tn(_bf(k_dec), db[h])
    for h in heads:
        on = o[h] * lax.rsqrt(jnp.mean(o[h] * o[h], axis=-1, keepdims=True) + 1e-6) * ng
        o_ref[:, vs[h]] = (on * _silu(z_ref[:, vs[h]])).astype(o_ref.dtype)

    bq[0:P, :] = bq[L:L + P, :]
    bk[0:P, :] = bk[L:L + P, :]
    bv[0:P, :] = bv[L:L + P, :]

    @pl.when(last)
    def _():
        sout_ref[0] = s_scr[...]


def _gdn(h, hs, conv_w, gp, norm_g, s_in, conv_in, layer, steps, hpg):
    ntok = h.shape[0]
    _, _, H, dk, dv = s_in.shape
    nc = conv_in.shape[2]
    assert dk == dv and H % hpg == 0 and nc < _SUBLANE and conv_w.shape[0] == nc + 1
    ng_ = H // hpg
    nseq = steps.bp + steps.bs
    L, P = _CHUNK, _SUBLANE
    wq = hpg * dk
    sseq = steps.sample_seq
    tok = lambda off: pl.BlockSpec((L, wq), lambda g, i: (i, off + g))
    cw = lambda off: pl.BlockSpec((nc + 1, wq), lambda g, i: (0, off + g))
    c0 = lambda off: pl.BlockSpec((1, 1, nc, wq), lambda g, i: (layer, sseq(i), 0, off + g))
    return pl.pallas_call(
        functools.partial(_gdn_kernel, hpg=hpg, dk=dk, dv=dv, steps=steps),
        grid=(ng_, steps.n),
        in_specs=[tok(0), tok(ng_), tok(2 * ng_), tok(3 * ng_),
                  pl.BlockSpec((L, _LANE), lambda g, i: (i, g)),
                  cw(0), cw(ng_), cw(2 * ng_),
                  pl.BlockSpec((1, 2, _LANE), lambda g, i: (g, 0, 0)),
                  pl.BlockSpec((1, dv), lambda g, i: (0, 0)),
                  pl.BlockSpec((1, 1, hpg, dk, dv), lambda g, i: (layer, sseq(i), g, 0, 0)),
                  c0(0), c0(ng_), c0(2 * ng_)],
        out_specs=[pl.BlockSpec((L, hpg * dv), lambda g, i: (i, g)),
                   pl.BlockSpec((1, hpg, dk, dv), lambda g, i: (steps.seq(i), g, 0, 0))],
        out_shape=[jax.ShapeDtypeStruct((ntok, H * dv), _BF16),
                   jax.ShapeDtypeStruct((nseq, H, dk, dv), _F32)],
        scratch_shapes=[pltpu.VMEM((hpg, dk, dv), _F32)] + [pltpu.VMEM((P + L, wq), _F32)] * 3,
        compiler_params=_params("parallel", "arbitrary"),
        name="gdn_mixer",
    )(h, h, h, h, hs, conv_w, conv_w, conv_w, gp, norm_g.reshape(1, dv), s_in, conv_in, conv_in, conv_in)


def _mlstm_kernel(q_ref, k_ref, v_ref, og_ref, gs_ref, fb_ref, ng_ref, c0_ref, n0_ref, m0_ref,
                  o_ref, cout_ref, nout_ref, mout_ref, c_scr, n_scr, m_scr, *, H, dk, dv, steps):
    L, P = _CHUNK, _SUBLANE
    i = pl.program_id(0)
    first, last = steps.first_last(i)
    in_p = steps.in_prompt(i)

    @pl.when(jnp.logical_and(first, in_p))
    def _():
        c_scr[...] = jnp.zeros(c_scr.shape, _F32)
        n_scr[...] = jnp.zeros(n_scr.shape, _F32)
        m_scr[...] = jnp.zeros(m_scr.shape, _F32)

    @pl.when(jnp.logical_and(first, jnp.logical_not(in_p)))
    def _():
        c_scr[...] = c0_ref[0, 0]
        n_scr[...] = n0_ref[0, 0]
        m0 = m0_ref[0, 0]
        for h in range(H):
            m_scr[h] = jnp.broadcast_to(m0[:, h:h + 1], (P, _LANE))

    row, col = _chunk_masks()
    incl = row >= col
    tril = incl.astype(_F32)
    gs = gs_ref[...]
    lf_blk = -_softplus(-(gs + fb_ref[...]))
    F_blk = _dot_hi(tril, lf_blk)
    FT = _transpose_hi(F_blk)
    gsT = _transpose_hi(gs)

    for h in range(H):
        ks, vs = slice(h * dk, (h + 1) * dk), slice(h * dv, (h + 1) * dv)
        q = q_ref[:, ks] * dk ** -0.5
        k = k_ref[:, ks]
        qb, kb, vb = _bf(q), _bf(k), _bf(v_ref[:, vs])
        igc = gs[:, h:h + 1]
        Fc = F_blk[:, H + h:H + h + 1]
        logD = jnp.where(incl, Fc - FT[H + h:H + h + 1, :] + gsT[h:h + 1, :], -jnp.inf)
        m_intra = jnp.max(logD, axis=-1, keepdims=True)
        m_prev = m_scr[h][0:1, 0:1]
        m_t = jnp.maximum(Fc + m_prev, m_intra)
        Dm = jnp.exp(logD - m_t)
        Sm = Dm * _dot_nt(qb, kb)
        inter = jnp.exp(Fc + m_prev - m_t)
        C = c_scr[h]
        n = n_scr[h:h + 1, :]
        num = inter * _dot(qb, _bf(C)) + _dot(_bf(Sm), vb)
        den = inter * jnp.sum(q * n, axis=-1, keepdims=True) + jnp.sum(Sm, axis=-1, keepdims=True)
        hh = num / jnp.maximum(jnp.abs(den), jnp.exp(-m_t))
        m_last = m_t[L - 1:L, :]
        w_end = jnp.exp(Fc[L - 1:L, :] - Fc + igc - m_last)
        dec = inter[L - 1:L, :]
        wk = w_end * k
        c_scr[h] = dec * C + _dot_tn(_bf(wk), vb)
        n_scr[h:h + 1, :] = dec * n + jnp.sum(wk, axis=0, keepdims=True)
        m_scr[h] = jnp.broadcast_to(m_last, (P, _LANE))
        o_ref[:, vs] = (_ln_head(hh, ng_ref[:, vs]) * jax.nn.sigmoid(og_ref[:, vs])).astype(o_ref.dtype)

    @pl.when(last)
    def _():
        cout_ref[0] = c_scr[...]
        nout_ref[0] = n_scr[...]
        lane = lax.broadcasted_iota(jnp.int32, (1, _LANE), 1)
        m_row = jnp.zeros((1, _LANE), _F32)
        for h in range(H):
            m_row = jnp.where(lane == h, m_scr[h][0:1, :], m_row)
        mout_ref[0] = m_row


def _mlstm(h, col0, hs, gate_block, f_bias_row, norm_g, c_in, n_in, m_in, layer, steps):
    ntok = h.shape[0]
    _, _, H, dk, dv = c_in.shape
    assert dv % dk == 0 and 2 % (dv // dk) == 0 and 2 * H <= _LANE and col0 % (H * dv) == 0
    L = _CHUNK
    nseq = steps.bp + steps.bs
    jq, jv = col0 // (H * dk), col0 // (H * dv) + 2 * dk // dv
    tq = lambda j: pl.BlockSpec((L, H * dk), lambda i: (i, jq + j))
    tv = lambda j: pl.BlockSpec((L, H * dv), lambda i: (i, jv + j))
    seq, sseq = steps.seq, steps.sample_seq
    return pl.pallas_call(
        functools.partial(_mlstm_kernel, H=H, dk=dk, dv=dv, steps=steps),
        grid=(steps.n,),
        in_specs=[tq(0), tq(1), tv(0), tv(1),
                  pl.BlockSpec((L, _LANE), lambda i: (i, gate_block)),
                  pl.BlockSpec((1, _LANE), lambda i: (0, 0)),
                  pl.BlockSpec((1, H * dv), lambda i: (0, 0)),
                  pl.BlockSpec((1, 1, H, dk, dv), lambda i: (layer, sseq(i), 0, 0, 0)),
                  pl.BlockSpec((1, 1, H, dk), lambda i: (layer, sseq(i), 0, 0)),
                  pl.BlockSpec((1, 1, 1, H), lambda i: (layer, sseq(i), 0, 0))],
        out_specs=[pl.BlockSpec((L, H * dv), lambda i: (i, 0)),
                   pl.BlockSpec((1, H, dk, dv), lambda i: (seq(i), 0, 0, 0)),
                   pl.BlockSpec((1, H, dk), lambda i: (seq(i), 0, 0)),
                   pl.BlockSpec((1, 1, _LANE), lambda i: (seq(i), 0, 0))],
        out_shape=[jax.ShapeDtypeStruct((ntok, H * dv), _BF16),
                   jax.ShapeDtypeStruct((nseq, H, dk, dv), _F32),
                   jax.ShapeDtypeStruct((nseq, H, dk), _F32),
                   jax.ShapeDtypeStruct((nseq, 1, _LANE), _F32)],
        scratch_shapes=[pltpu.VMEM((H, dk, dv), _F32), pltpu.VMEM((H, dk), _F32),
                        pltpu.VMEM((H, _SUBLANE, _LANE), _F32)],
        compiler_params=_params("arbitrary"),
        name="mlstm_mixer",
    )(h, h, h, h, hs, f_bias_row, norm_g.reshape(1, H * dv), c_in, n_in,
      m_in.reshape(m_in.shape[0], m_in.shape[1], 1, H))


def _ret_kernel(lg_ref, q_ref, k_ref, v_ref, og_ref, cos_ref, sin_ref, ng_ref, s0_ref,
                o_ref, sout_ref, s_scr, *, H, dk, dv, steps):
    L = _CHUNK
    i = pl.program_id(0)
    first, last = steps.first_last(i)
    in_p = steps.in_prompt(i)

    @pl.when(jnp.logical_and(first, in_p))
    def _():
        s_scr[...] = jnp.zeros(s_scr.shape, _F32)

    @pl.when(jnp.logical_and(first, jnp.logical_not(in_p)))
    def _():
        s_scr[...] = s0_ref[0, 0]

    row, col = _chunk_masks()
    incl = row >= col
    tdiff = (row - col).astype(_F32)
    jc = lax.broadcasted_iota(jnp.int32, (L, 1), 0).astype(_F32)
    cos, sin = cos_ref[...], sin_ref[...]

    def rotary(x):
        return x * cos + pltpu.roll(x, dk // 2, 1) * sin

    for h in range(H):
        ks, vs = slice(h * dk, (h + 1) * dk), slice(h * dv, (h + 1) * dv)
        lg = lg_ref[h]
        q = rotary(q_ref[:, ks])
        k = rotary(k_ref[:, ks]) * dk ** -0.5
        vb = _bf(v_ref[:, vs])
        D = jnp.where(incl, jnp.exp(jnp.where(incl, lg * tdiff, 0.0)), 0.0)
        qk = _dot_nt(_bf(q), _bf(k)) * D
        q_in = q * jnp.exp(lg * (jc + 1.0))
        k_st = k * jnp.exp(lg * (L - 1.0 - jc))
        gL = jnp.exp(lg * jnp.full((1, 1), float(L), _F32))
        S = s_scr[h]
        o = _dot(_bf(qk), vb) + _dot(_bf(q_in), _bf(S))
        s_scr[h] = gL * S + _dot_tn(_bf(k_st), vb)
        o_ref[:, vs] = (_ln_head(o, ng_ref[:, vs]) * _silu(og_ref[:, vs])).astype(o_ref.dtype)

    @pl.when(last)
    def _():
        sout_ref[0] = s_scr[...]


def _retention(h, col0, lg, cos_tab, sin_tab, norm_g, s_in, layer, steps):
    ntok = h.shape[0]
    _, _, H, dk, dv = s_in.shape
    assert dv % dk == 0 and 2 % (dv // dk) == 0 and col0 % (H * dv) == 0
    L = _CHUNK
    nseq = steps.bp + steps.bs
    jq, jv = col0 // (H * dk), col0 // (H * dv) + 2 * dk // dv
    tq = lambda j: pl.BlockSpec((L, H * dk), lambda i: (i, jq + j))
    tv = lambda j: pl.BlockSpec((L, H * dv), lambda i: (i, jv + j))
    tab = pl.BlockSpec((L, dk), lambda i: (steps.pos_block(i), 0))
    return pl.pallas_call(
        functools.partial(_ret_kernel, H=H, dk=dk, dv=dv, steps=steps),
        grid=(steps.n,),
        in_specs=[pl.BlockSpec(memory_space=pltpu.SMEM),
                  tq(0), tq(1), tv(0), tv(1), tab, tab,
                  pl.BlockSpec((1, H * dv), lambda i: (0, 0)),
                  pl.BlockSpec((1, 1, H, dk, dv), lambda i: (layer, steps.sample_seq(i), 0, 0, 0))],
        out_specs=[pl.BlockSpec((L, H * dv), lambda i: (i, 0)),
                   pl.BlockSpec((1, H, dk, dv), lambda i: (steps.seq(i), 0, 0, 0))],
        out_shape=[jax.ShapeDtypeStruct((ntok, H * dv), _BF16),
                   jax.ShapeDtypeStruct((nseq, H, dk, dv), _F32)],
        scratch_shapes=[pltpu.VMEM((H, dk, dv), _F32)],
        compiler_params=_params("arbitrary"),
        name="retention_mixer",
    )(lg, h, h, h, h, cos_tab, sin_tab, norm_g.reshape(1, H * dv), s_in)


def _rotary_tables(tp, ts, dk):
    half = dk // 2
    freq = _ROPE_BASE ** (-jnp.arange(half, dtype=_F32) / half)
    pos = jnp.concatenate([0.0 + jnp.arange(tp, dtype=_F32), float(_PAST_LEN) + jnp.arange(ts, dtype=_F32)])
    ang = pos[:, None] * freq[None, :]
    cos, sin = jnp.cos(ang), jnp.sin(ang)
    return jnp.concatenate([cos, cos], axis=1), jnp.concatenate([-sin, sin], axis=1)


def kernel(x_prompt, x_sample, state_gdn_S, state_gdn_conv, state_mlstm_C, state_mlstm_n, state_mlstm_m, state_ret_S, w_in, b_in, gdn_conv_w, gdn_A_log, gdn_dt_bias, gdn_norm_g, mlstm_f_bias, mlstm_norm_g, ret_norm_g, w_branch, w_out, ln1_g, ln1_b, ln2_g, ln2_b, w_ffn_up, w_ffn_down):
    bp, tp, D = x_prompt.shape
    bs, ts, _ = x_sample.shape
    depth = w_in.shape[0]
    alpha = (2 * depth) ** 0.25
    Hg, gdk, gdv = state_gdn_S.shape[2:]
    Hm, mdk, mdv = state_mlstm_C.shape[2:]
    Hr, rdk, rdv = state_ret_S.shape[2:]
    nconv = state_gdn_conv.shape[2]
    steps = _Steps(bp, tp, bs, ts)
    np_tok = bp * tp

    widths = (2 * Hg * gdk + Hg * gdv, Hg * gdv, Hg, Hg,
              Hm * mdk, Hm * mdk, Hm * mdv, Hm * mdv, Hm, Hm,
              Hr * rdk, Hr * rdk, Hr * rdv, Hr * rdv, 3 * D)
    offs = [0]
    for w_ in widths:
        offs.append(offs[-1] + w_)
    o_qkv, _, o_b, o_a, o_mq, _, _, _, o_mi, o_mf, o_rq, _, _, _, _, o_end = offs
    assert o_end == w_in.shape[2]
    qkv_w = widths[0]
    wide = ((o_qkv, o_b), (o_mq, o_mi), (o_rq, o_end))
    ml_col0 = o_b - o_qkv
    ret_col0 = ml_col0 + (o_mi - o_mq)
    gate_col0 = ret_col0 + sum(widths[10:14])

    hpg = min(Hg, 8)
    ngrp = Hg // hpg
    assert 2 * hpg <= _LANE

    def wide_cols(a):
        return jnp.concatenate([a[..., lo:hi] for lo, hi in wide], axis=-1)

    def gate_cols(a):
        def block(lo1, lo2, n):
            pad = jnp.zeros(a.shape[:-1] + (_LANE - 2 * n,), a.dtype)
            return [a[..., lo1:lo1 + n], a[..., lo2:lo2 + n], pad]
        parts = []
        for g in range(ngrp):
            parts += block(o_b + g * hpg, o_a + g * hpg, hpg)
        parts += block(o_mi, o_mf, Hm)
        return jnp.concatenate(parts, axis=-1)

    cos_tab, sin_tab = _rotary_tables(tp, ts, rdk)
    lg = jnp.log(1.0 - 2.0 ** (-5.0 - jnp.arange(Hr, dtype=_F32)))

    x = jnp.concatenate([x_prompt.reshape(np_tok, D), x_sample.reshape(bs * ts, D)], axis=0)
    xb = x.astype(_BF16)
    new_p, new_s = [], []
    for l in range(depth):
        wl, bl = w_in[l], b_in[l]
        h = _matmul(xb, wide_cols(wl).astype(_BF16), bias=wide_cols(bl), name="in_proj")
        hs = _small_proj(x, gate_cols(wl), gate_cols(bl))

        gp = jnp.zeros((ngrp, 2, _LANE), _F32)
        gp = gp.at[:, 0, hpg:2 * hpg].set(gdn_A_log[l].reshape(ngrp, hpg))
        gp = gp.at[:, 1, hpg:2 * hpg].set(gdn_dt_bias[l].reshape(ngrp, hpg))
        oa, gS = _gdn(h, hs, gdn_conv_w[l], gp, gdn_norm_g[l], state_gdn_S, state_gdn_conv, l, steps, hpg)
        conv_p = h[:np_tok].reshape(bp, tp, -1)[:, tp - nconv:, :qkv_w]
        conv_s = h[np_tok:].reshape(bs, ts, -1)[:, ts - nconv:, :qkv_w]

        fb = jnp.zeros((1, _LANE), _F32).at[0, Hm:2 * Hm].set(mlstm_f_bias[l])
        ob, mC, mn, mm = _mlstm(h, ml_col0, hs, ngrp, fb, mlstm_norm_g[l],
                                state_mlstm_C, state_mlstm_n, state_mlstm_m, l, steps)
        mm = mm[:, 0, :Hm]

        oc, rS = _retention(h, ret_col0, lg, cos_tab, sin_tab, ret_norm_g[l], state_ret_S, l, steps)

        merged = _merge(oa, ob, oc, w_branch[l].astype(_BF16), h, gate_col0)
        z1 = _matmul(merged, w_out[l].astype(_BF16), res=x, alpha=alpha, name="out_proj")
        x, xb = _layernorm(z1, ln1_g[l], ln1_b[l])

        act = _ffn_up(xb, w_ffn_up, l)
        z2 = _matmul(act, w_ffn_down[l].astype(_BF16), res=x, alpha=alpha, bm=512, bn=512, name="ffn_down")
        x, xb = _layernorm(z2, ln2_g[l], ln2_b[l])

        new_p.append((gS[:bp], conv_p, mC[:bp], mn[:bp], mm[:bp], rS[:bp]))
        new_s.append((gS[bp:], conv_s, mC[bp:], mn[bp:], mm[bp:], rS[bp:]))

    def stk(lst, i):
        return jnp.stack([s[i] for s in lst], axis=0)

    yp = x[:np_tok].reshape(bp, tp, D)
    ys = x[np_tok:].reshape(bs, ts, D)
    return (yp, ys,
            stk(new_p, 0), stk(new_p, 1), stk(new_p, 2), stk(new_p, 3), stk(new_p, 4), stk(new_p, 5),
            stk(new_s, 0), stk(new_s, 1), stk(new_s, 2), stk(new_s, 3), stk(new_s, 4), stk(new_s, 5))
```

```python
import functools

import jax
import jax.numpy as jnp
from jax import lax
from jax.experimental import pallas as pl
from jax.experimental.pallas import tpu as pltpu

_F32 = jnp.float32
_BF16 = jnp.bfloat16
_HI = lax.Precision.HIGHEST

_CHUNK = 64
_PAST_LEN = 1024
_ROPE_BASE = 10000.0
_LN_EPS = 1e-5
_LANE = 128
_SUBLANE = 8
_VMEM_LIMIT = 56 * 1024 * 1024


def _tile(dim, pref, align=_LANE):
    best = None
    t = align
    while t <= min(dim, pref):
        if dim % t == 0:
            best = t
        t += align
    return dim if best is None else best


def _params(*sem):
    return pltpu.CompilerParams(dimension_semantics=sem, vmem_limit_bytes=_VMEM_LIMIT)


def _bf(x):
    return x.astype(_BF16)


def _dot(a, b):
    return jnp.dot(a, b, preferred_element_type=_F32)


def _dot_nt(a, b):
    return lax.dot_general(a, b, (((1,), (1,)), ((), ())), preferred_element_type=_F32)


def _dot_tn(a, b):
    return lax.dot_general(a, b, (((0,), (0,)), ((), ())), preferred_element_type=_F32)


def _dot_hi(a, b):
    return jnp.dot(a, b, precision=_HI, preferred_element_type=_F32)


def _split(x):
    hi = x.astype(_BF16)
    return hi, (x - hi.astype(_F32)).astype(_BF16)


def _mm3(a, b):
    return _dot(a[0], b[0]) + (_dot(a[0], b[1]) + _dot(a[1], b[0]))


def _mm_kernel(*refs, nk, has_bias, has_res, alpha):
    it = iter(refs)
    x_ref, w_ref = next(it), next(it)
    b_ref = next(it) if has_bias else None
    r_ref = next(it) if has_res else None
    o_ref = next(it)
    acc_ref = next(it) if nk > 1 else None

    def finish(v):
        if has_bias:
            v = v + b_ref[...]
        if has_res:
            v = alpha * r_ref[...] + v
        o_ref[...] = v.astype(o_ref.dtype)

    part = jnp.dot(x_ref[...], w_ref[0], preferred_element_type=_F32)
    if nk == 1:
        finish(part)
    else:
        k = pl.program_id(2)

        @pl.when(k == 0)
        def _():
            acc_ref[...] = part

        @pl.when(k > 0)
        def _():
            acc_ref[...] += part

        @pl.when(k == nk - 1)
        def _():
            finish(acc_ref[...])


def _matmul(x, w, layer, *, bias=None, res=None, alpha=1.0, out_dtype=_F32, bm=1024, bn=1024, bk=None, name):
    M, K = x.shape
    N = w.shape[2]
    bm, bn = _tile(M, bm), _tile(N, bn)
    bk = K if bk is None else _tile(K, bk)
    nk = K // bk
    in_specs = [pl.BlockSpec((bm, bk), lambda i, j, k: (i, k)),
                pl.BlockSpec((1, bk, bn), lambda i, j, k: (layer, k, j))]
    args = [x, w]
    if bias is not None:
        in_specs.append(pl.BlockSpec((1, bn), lambda i, j, k: (0, j)))
        args.append(bias.reshape(1, N).astype(_F32))
    if res is not None:
        in_specs.append(pl.BlockSpec((bm, bn), lambda i, j, k: (i, j)))
        args.append(res)
    return pl.pallas_call(
        functools.partial(_mm_kernel, nk=nk, has_bias=bias is not None, has_res=res is not None, alpha=alpha),
        grid=(M // bm, N // bn, nk),
        in_specs=in_specs,
        out_specs=pl.BlockSpec((bm, bn), lambda i, j, k: (i, j)),
        out_shape=jax.ShapeDtypeStruct((M, N), out_dtype),
        scratch_shapes=[pltpu.VMEM((bm, bn), _F32)] if nk > 1 else [],
        compiler_params=_params("parallel", "parallel", "arbitrary"),
        name=name,
    )(*args)


def _small_proj_kernel(x_ref, w_ref, b_ref, o_ref):
    o_ref[...] = _mm3(_split(x_ref[...]), _split(w_ref[0])) + b_ref[...]


def _small_proj(x, w, layer, b):
    M, K = x.shape
    N = w.shape[2]
    bm = _tile(M, 256)
    return pl.pallas_call(
        _small_proj_kernel,
        grid=(M // bm,),
        in_specs=[pl.BlockSpec((bm, K), lambda i: (i, 0)),
                  pl.BlockSpec((1, K, N), lambda i: (layer, 0, 0)),
                  pl.BlockSpec((1, N), lambda i: (0, 0))],
        out_specs=pl.BlockSpec((bm, N), lambda i: (i, 0)),
        out_shape=jax.ShapeDtypeStruct((M, N), _F32),
        compiler_params=_params("parallel"),
        name="small_proj",
    )(x, w, b.reshape(1, N))


def _ln_kernel(z_ref, g_ref, b_ref, o_ref, ob_ref):
    z = z_ref[...]
    mu = jnp.mean(z, axis=-1, keepdims=True)
    d = z - mu
    var = jnp.mean(d * d, axis=-1, keepdims=True)
    y = d * lax.rsqrt(var + _LN_EPS) * g_ref[...] + b_ref[...]
    o_ref[...] = y
    ob_ref[...] = y.astype(_BF16)


def _layernorm(z, g, b):
    M, D = z.shape
    bm = _tile(M, 256)
    return pl.pallas_call(
        _ln_kernel,
        grid=(M // bm,),
        in_specs=[pl.BlockSpec((bm, D), lambda i: (i, 0)),
                  pl.BlockSpec((1, D), lambda i: (0, 0)),
                  pl.BlockSpec((1, D), lambda i: (0, 0))],
        out_specs=[pl.BlockSpec((bm, D), lambda i: (i, 0)),
                   pl.BlockSpec((bm, D), lambda i: (i, 0))],
        out_shape=[jax.ShapeDtypeStruct((M, D), _F32), jax.ShapeDtypeStruct((M, D), _BF16)],
        compiler_params=_params("parallel"),
        name="layernorm",
    )(z, g.reshape(1, D), b.reshape(1, D))


def _merge_kernel(a_ref, b_ref, c_ref, w_ref, g_ref, o_ref, acc_ref):
    br = pl.program_id(2)
    gate = jax.nn.sigmoid(g_ref[...])

    def contrib(x_ref):
        return gate * jnp.dot(x_ref[...], w_ref[0, 0], preferred_element_type=_F32)

    @pl.when(br == 0)
    def _():
        acc_ref[...] = contrib(a_ref)

    @pl.when(br == 1)
    def _():
        acc_ref[...] += contrib(b_ref)

    @pl.when(br == 2)
    def _():
        o_ref[...] = (acc_ref[...] + contrib(c_ref)).astype(o_ref.dtype)


def _merge(oa, ob, oc, w_branch, layer, h, gate_col0):
    M, BW = oa.shape
    D = w_branch.shape[3]
    bm, bn = _tile(M, 1024), _tile(D, 1024)
    nj = D // bn
    assert gate_col0 % bn == 0
    g0 = gate_col0 // bn
    xspec = pl.BlockSpec((bm, BW), lambda i, j, n: (i, 0))
    return pl.pallas_call(
        _merge_kernel,
        grid=(M // bm, nj, 3),
        in_specs=[xspec, xspec, xspec,
                  pl.BlockSpec((1, 1, BW, bn), lambda i, j, n: (layer, n, 0, j)),
                  pl.BlockSpec((bm, bn), lambda i, j, n: (i, g0 + n * nj + j))],
        out_specs=pl.BlockSpec((bm, bn), lambda i, j, n: (i, j)),
        out_shape=jax.ShapeDtypeStruct((M, D), _BF16),
        scratch_shapes=[pltpu.VMEM((bm, bn), _F32)],
        compiler_params=_params("parallel", "parallel", "arbitrary"),
        name="branch_merge",
    )(oa, ob, oc, w_branch, h)


def _ffn_up_kernel(x_ref, wg_ref, wv_ref, o_ref):
    x = x_ref[...]
    g = jnp.dot(x, _bf(wg_ref[0]), preferred_element_type=_F32)
    v = jnp.dot(x, _bf(wv_ref[0]), preferred_element_type=_F32)
    o_ref[...] = (g * jax.nn.sigmoid(g) * v).astype(o_ref.dtype)


def _ffn_up(x, w_up, layer):
    M, K = x.shape
    F = w_up.shape[2] // 2
    bm, bn = _tile(M, 1024), _tile(F, 256)
    nj = F // bn
    return pl.pallas_call(
        _ffn_up_kernel,
        grid=(M // bm, nj),
        in_specs=[pl.BlockSpec((bm, K), lambda i, j: (i, 0)),
                  pl.BlockSpec((1, K, bn), lambda i, j: (layer, 0, j)),
                  pl.BlockSpec((1, K, bn), lambda i, j: (layer, 0, nj + j))],
        out_specs=pl.BlockSpec((bm, bn), lambda i, j: (i, j)),
        out_shape=jax.ShapeDtypeStruct((M, F), _BF16),
        compiler_params=_params("parallel", "parallel"),
        name="ffn_up",
    )(x, w_up, w_up)


def _transpose_hi(x):
    n = x.shape[1]
    eye = (lax.broadcasted_iota(jnp.int32, (n, n), 0) == lax.broadcasted_iota(jnp.int32, (n, n), 1)).astype(_F32)
    return lax.dot_general(eye, x, (((1,), (1,)), ((), ())), precision=_HI, preferred_element_type=_F32)


def _silu(x):
    return x * jax.nn.sigmoid(x)


def _softplus(x):
    return jnp.maximum(x, 0.0) + jnp.log1p(jnp.exp(-jnp.abs(x)))


def _chunk_masks():
    L = _CHUNK
    row = lax.broadcasted_iota(jnp.int32, (L, L), 0)
    col = lax.broadcasted_iota(jnp.int32, (L, L), 1)
    return row, col


def _ln_head(o, g):
    mu = jnp.mean(o, axis=-1, keepdims=True)
    d = o - mu
    var = jnp.mean(d * d, axis=-1, keepdims=True)
    return d * lax.rsqrt(var + _LN_EPS) * g


def _unit_lower_inverses(As, row, col):
    same16 = (row // 16) == (col // 16)
    same32 = (row // 32) == (col // 32)
    eye = (row == col).astype(_F32)
    D = [jnp.where(same16, A, 0.0) for A in As]
    Ds = [_split(d) for d in D]
    D2s = [_split(_mm3(d, d)) for d in Ds]
    P = [eye - d for d in D]
    P = [p + _mm3(_split(p), d2) for p, d2 in zip(P, D2s)]
    D4s = [_split(_mm3(d2, d2)) for d2 in D2s]
    P = [p + _mm3(_split(p), d4) for p, d4 in zip(P, D4s)]
    D8s = [_split(_mm3(d4, d4)) for d4 in D4s]
    P = [p + _mm3(_split(p), d8) for p, d8 in zip(P, D8s)]
    for off_mask in (jnp.logical_and(same32, jnp.logical_not(same16)), jnp.logical_not(same32)):
        Es = [_split(jnp.where(off_mask, A, 0.0)) for A in As]
        Ps = [_split(p) for p in P]
        Qs = [_split(_mm3(e, ps)) for e, ps in zip(Es, Ps)]
        P = [p - _mm3(ps, q) for p, ps, q in zip(P, Ps, Qs)]
    return P


class _Steps:
    def __init__(self, bp, tp, bs, ts):
        assert tp % _CHUNK == 0 and ts % _CHUNK == 0
        self.bp, self.bs = bp, bs
        self.ncp, self.ncs = tp // _CHUNK, ts // _CHUNK
        self.np_steps = bp * self.ncp
        self.n = self.np_steps + bs * self.ncs

    def in_prompt(self, i):
        return i < self.np_steps

    def _sample_step(self, i):
        return jnp.maximum(i - self.np_steps, 0)

    def seq(self, i):
        return jnp.where(self.in_prompt(i), i // self.ncp, self.bp + self._sample_step(i) // self.ncs)

    def sample_seq(self, i):
        return self._sample_step(i) // self.ncs

    def prompt_seq(self, i):
        return jnp.minimum(i // self.ncp, self.bp - 1)

    def chunk(self, i):
        return jnp.where(self.in_prompt(i), i % self.ncp, self._sample_step(i) % self.ncs)

    def pos_block(self, i):
        return jnp.where(self.in_prompt(i), self.chunk(i), self.ncp + self.chunk(i))

    def first_last(self, i):
        c = self.chunk(i)
        return c == 0, jnp.where(self.in_prompt(i), c == self.ncp - 1, c == self.ncs - 1)


def _gdn_kernel(q_ref, k_ref, v_ref, z_ref, gs_ref, cwq_ref, cwk_ref, cwv_ref, gp_ref, ng_ref,
                s0_ref, c0q_ref, c0k_ref, c0v_ref, *rest, hpg, dk, dv, steps):
    o_ref, sp_ref, ss_ref, s_scr, bq, bk, bv = rest[-7:]
    L, P = _CHUNK, _SUBLANE
    i = pl.program_id(1)
    first, last = steps.first_last(i)
    in_p = steps.in_prompt(i)
    nc = c0q_ref.shape[2]

    @pl.when(jnp.logical_and(first, in_p))
    def _():
        s_scr[...] = jnp.zeros(s_scr.shape, _F32)
        for buf in (bq, bk, bv):
            buf[0:P, :] = jnp.zeros((P, buf.shape[1]), _F32)

    @pl.when(jnp.logical_and(first, jnp.logical_not(in_p)))
    def _():
        s_scr[...] = s0_ref[0, 0]
        bq[P - nc:P, :] = c0q_ref[0, 0]
        bk[P - nc:P, :] = c0k_ref[0, 0]
        bv[P - nc:P, :] = c0v_ref[0, 0]

    bq[P:P + L, :] = q_ref[...]
    bk[P:P + L, :] = k_ref[...]
    bv[P:P + L, :] = v_ref[...]

    row, col = _chunk_masks()
    incl, strict = row >= col, row > col
    tril = incl.astype(_F32)
    gs = gs_ref[...]
    gp = gp_ref[0]
    beta_blk = jax.nn.sigmoid(gs)
    g_blk = -jnp.exp(gp[0:1, :]) * _softplus(gs + gp[1:2, :])
    G_blk = _dot_hi(tril, g_blk)
    GT = _transpose_hi(G_blk)
    ng = ng_ref[...]
    heads = range(hpg)

    def conv(buf, cw_ref, cs):
        out = buf[P - nc:P - nc + L, cs] * cw_ref[0:1, cs]
        for j in range(1, nc + 1):
            out = out + buf[P - nc + j:P - nc + j + L, cs] * cw_ref[j:j + 1, cs]
        return out

    def l2norm(x):
        return x * lax.rsqrt(jnp.sum(x * x, axis=-1, keepdims=True) + 1e-6)

    ks = [slice(h * dk, (h + 1) * dk) for h in heads]
    vs = [slice(h * dv, (h + 1) * dv) for h in heads]
    q = [l2norm(_silu(conv(bq, cwq_ref, ks[h]))) * dk ** -0.5 for h in heads]
    k = [l2norm(_silu(conv(bk, cwk_ref, ks[h]))) for h in heads]
    v = [_silu(conv(bv, cwv_ref, vs[h])) for h in heads]
    beta = [beta_blk[:, h:h + 1] for h in heads]
    Gc = [G_blk[:, hpg + h:hpg + h + 1] for h in heads]
    gam = [jnp.where(incl, jnp.exp(jnp.where(incl, Gc[h] - GT[hpg + h:hpg + h + 1, :], 0.0)), 0.0) for h in heads]
    kb = [_bf(k[h]) for h in heads]
    kq = [_dot_nt(jnp.concatenate([kb[h], _bf(q[h])], axis=0), kb[h]) for h in heads]
    A = [jnp.where(strict, beta[h] * kq[h][0:L] * gam[h], 0.0) for h in heads]
    qk = [_bf(kq[h][L:2 * L] * gam[h]) for h in heads]
    eG = [jnp.exp(Gc[h]) for h in heads]
    T = _unit_lower_inverses(A, row, col)
    rhs = [jnp.concatenate([beta[h] * v[h], (beta[h] * eG[h]) * k[h]], axis=1) for h in heads]
    UW = [_mm3(_split(T[h]), _split(rhs[h])) for h in heads]
    G_last = [Gc[h][L - 1:L, :] for h in heads]
    S = [s_scr[h] for h in heads]
    Sb = [_bf(S[h]) for h in heads]
    WQ = [_dot(jnp.concatenate([_bf(UW[h][:, dv:]), _bf(q[h] * eG[h])], axis=0), Sb[h]) for h in heads]
    db = [_bf(UW[h][:, 0:dv] - WQ[h][0:L]) for h in heads]
    o = [WQ[h][L:2 * L] + _dot(qk[h], db[h]) for h in heads]
    for h in heads:
        k_dec = k[h] * jnp.exp(G_last[h] - Gc[h])
        s_scr[h] = jnp.exp(G_last[h]) * S[h] + _dot_tn(_bf(k_dec), db[h])
    for h in heads:
        on = o[h] * lax.rsqrt(jnp.mean(o[h] * o[h], axis=-1, keepdims=True) + 1e-6) * ng
        o_ref[:, vs[h]] = (on * _silu(z_ref[:, vs[h]])).astype(o_ref.dtype)

    bq[0:P, :] = bq[L:L + P, :]
    bk[0:P, :] = bk[L:L + P, :]
    bv[0:P, :] = bv[L:L + P, :]

    @pl.when(jnp.logical_and(last, in_p))
    def _():
        sp_ref[0, 0] = s_scr[...]

    @pl.when(jnp.logical_and(last, jnp.logical_not(in_p)))
    def _():
        ss_ref[0, 0] = s_scr[...]


def _state_outputs(prev, shapes_p, shapes_s, n_in, n_lead_out):
    shapes = [jax.ShapeDtypeStruct(s, _F32) for pair in zip(shapes_p, shapes_s) for s in pair]
    if prev is None:
        return shapes, [], {}, ()
    specs = [pl.BlockSpec(memory_space=pl.ANY)] * len(prev)
    aliases = {n_in + j: n_lead_out + j for j in range(len(prev))}
    return shapes, specs, aliases, tuple(prev)


def _gdn(h, hs, conv_w, gp, norm_g, s_in, conv_in, prev, layer, steps, hpg):
    ntok = h.shape[0]
    depth, _, H, dk, dv = s_in.shape
    nc = conv_in.shape[2]
    assert dk == dv and H % hpg == 0 and nc < _SUBLANE and conv_w.shape[0] == nc + 1
    ng_ = H // hpg
    L, P = _CHUNK, _SUBLANE
    wq = hpg * dk
    sseq, pseq = steps.sample_seq, steps.prompt_seq
    tok = lambda off: pl.BlockSpec((L, wq), lambda g, i: (i, off + g))
    cw = lambda off: pl.BlockSpec((nc + 1, wq), lambda g, i: (0, off + g))
    c0 = lambda off: pl.BlockSpec((1, 1, nc, wq), lambda g, i: (layer, sseq(i), 0, off + g))
    in_specs = [tok(0), tok(ng_), tok(2 * ng_), tok(3 * ng_),
                pl.BlockSpec((L, _LANE), lambda g, i: (i, g)),
                cw(0), cw(ng_), cw(2 * ng_),
                pl.BlockSpec((1, 2, _LANE), lambda g, i: (g, 0, 0)),
                pl.BlockSpec((1, dv), lambda g, i: (0, 0)),
                pl.BlockSpec((1, 1, hpg, dk, dv), lambda g, i: (layer, sseq(i), g, 0, 0)),
                c0(0), c0(ng_), c0(2 * ng_)]
    shapes, pspecs, aliases, pargs = _state_outputs(
        prev, [(depth, steps.bp, H, dk, dv)], [(depth, steps.bs, H, dk, dv)], len(in_specs), 1)
    out = pl.pallas_call(
        functools.partial(_gdn_kernel, hpg=hpg, dk=dk, dv=dv, steps=steps),
        grid=(ng_, steps.n),
        in_specs=in_specs + pspecs,
        out_specs=[pl.BlockSpec((L, hpg * dv), lambda g, i: (i, g)),
                   pl.BlockSpec((1, 1, hpg, dk, dv), lambda g, i: (layer, pseq(i), g, 0, 0)),
                   pl.BlockSpec((1, 1, hpg, dk, dv), lambda g, i: (layer, sseq(i), g, 0, 0))],
        out_shape=[jax.ShapeDtypeStruct((ntok, H * dv), _BF16)] + shapes,
        input_output_aliases=aliases,
        scratch_shapes=[pltpu.VMEM((hpg, dk, dv), _F32)] + [pltpu.VMEM((P + L, wq), _F32)] * 3,
        compiler_params=_params("parallel", "arbitrary"),
        name="gdn_mixer",
    )(h, h, h, h, hs, conv_w, conv_w, conv_w, gp, norm_g.reshape(1, dv), s_in, conv_in, conv_in, conv_in, *pargs)
    return out[0], out[1:]


def _mlstm_kernel(q_ref, k_ref, v_ref, og_ref, gs_ref, fb_ref, ng_ref, c0_ref, n0_ref, m0_ref,
                  *rest, H, dk, dv, steps):
    o_ref, cp_ref, cs_ref, np_ref, ns_ref, mp_ref, ms_ref, c_scr, n_scr, m_scr = rest[-10:]
    L, P = _CHUNK, _SUBLANE
    i = pl.program_id(0)
    first, last = steps.first_last(i)
    in_p = steps.in_prompt(i)

    @pl.when(jnp.logical_and(first, in_p))
    def _():
        c_scr[...] = jnp.zeros(c_scr.shape, _F32)
        n_scr[...] = jnp.zeros(n_scr.shape, _F32)
        m_scr[...] = jnp.zeros(m_scr.shape, _F32)

    @pl.when(jnp.logical_and(first, jnp.logical_not(in_p)))
    def _():
        c_scr[...] = c0_ref[0, 0]
        n_scr[...] = n0_ref[0, 0]
        m0 = m0_ref[0, 0]
        for h in range(H):
            m_scr[h] = jnp.broadcast_to(m0[:, h:h + 1], (P, _LANE))

    row, col = _chunk_masks()
    incl = row >= col
    tril = incl.astype(_F32)
    gs = gs_ref[...]
    lf_blk = -_softplus(-(gs + fb_ref[...]))
    F_blk = _dot_hi(tril, lf_blk)
    FT = _transpose_hi(F_blk)
    gsT = _transpose_hi(gs)

    for h in range(H):
        ks, vs = slice(h * dk, (h + 1) * dk), slice(h * dv, (h + 1) * dv)
        q = q_ref[:, ks] * dk ** -0.5
        k = k_ref[:, ks]
        qb, kb, vb = _bf(q), _bf(k), _bf(v_ref[:, vs])
        igc = gs[:, h:h + 1]
        Fc = F_blk[:, H + h:H + h + 1]
        logD = jnp.where(incl, Fc - FT[H + h:H + h + 1, :] + gsT[h:h + 1, :], -jnp.inf)
        m_intra = jnp.max(logD, axis=-1, keepdims=True)
        m_prev = m_scr[h][0:1, 0:1]
        m_t = jnp.maximum(Fc + m_prev, m_intra)
        Dm = jnp.exp(logD - m_t)
        Sm = Dm * _dot_nt(qb, kb)
        inter = jnp.exp(Fc + m_prev - m_t)
        C = c_scr[h]
        n = n_scr[h:h + 1, :]
        num = inter * _dot(qb, _bf(C)) + _dot(_bf(Sm), vb)
        den = inter * jnp.sum(q * n, axis=-1, keepdims=True) + jnp.sum(Sm, axis=-1, keepdims=True)
        hh = num / jnp.maximum(jnp.abs(den), jnp.exp(-m_t))
        m_last = m_t[L - 1:L, :]
        w_end = jnp.exp(Fc[L - 1:L, :] - Fc + igc - m_last)
        dec = inter[L - 1:L, :]
        wk = w_end * k
        c_scr[h] = dec * C + _dot_tn(_bf(wk), vb)
        n_scr[h:h + 1, :] = dec * n + jnp.sum(wk, axis=0, keepdims=True)
        m_scr[h] = jnp.broadcast_to(m_last, (P, _LANE))
        o_ref[:, vs] = (_ln_head(hh, ng_ref[:, vs]) * jax.nn.sigmoid(og_ref[:, vs])).astype(o_ref.dtype)

    def write_state(c_ref, n_ref, m_ref):
        c_ref[0, 0] = c_scr[...]
        n_ref[0, 0] = n_scr[...]
        lane = lax.broadcasted_iota(jnp.int32, (1, _LANE), 1)
        m_row = jnp.zeros((1, _LANE), _F32)
        for h in range(H):
            m_row = jnp.where(lane == h, m_scr[h][0:1, :], m_row)
        m_ref[0, 0] = m_row

    @pl.when(jnp.logical_and(last, in_p))
    def _():
        write_state(cp_ref, np_ref, mp_ref)

    @pl.when(jnp.logical_and(last, jnp.logical_not(in_p)))
    def _():
        write_state(cs_ref, ns_ref, ms_ref)


def _mlstm(h, col0, hs, gate_block, f_bias_row, norm_g, c_in, n_in, m_in, prev, layer, steps):
    ntok = h.shape[0]
    depth, _, H, dk, dv = c_in.shape
    assert dv % dk == 0 and 2 % (dv // dk) == 0 and 2 * H <= _LANE and col0 % (H * dv) == 0
    L = _CHUNK
    bp, bs = steps.bp, steps.bs
    jq, jv = col0 // (H * dk), col0 // (H * dv) + 2 * dk // dv
    tq = lambda j: pl.BlockSpec((L, H * dk), lambda i: (i, jq + j))
    tv = lambda j: pl.BlockSpec((L, H * dv), lambda i: (i, jv + j))
    sseq, pseq = steps.sample_seq, steps.prompt_seq
    in_specs = [tq(0), tq(1), tv(0), tv(1),
                pl.BlockSpec((L, _LANE), lambda i: (i, gate_block)),
                pl.BlockSpec((1, _LANE), lambda i: (0, 0)),
                pl.BlockSpec((1, H * dv), lambda i: (0, 0)),
                pl.BlockSpec((1, 1, H, dk, dv), lambda i: (layer, sseq(i), 0, 0, 0)),
                pl.BlockSpec((1, 1, H, dk), lambda i: (layer, sseq(i), 0, 0)),
                pl.BlockSpec((1, 1, 1, H), lambda i: (layer, sseq(i), 0, 0))]
    tails = [(H, dk, dv), (H, dk), (1, _LANE)]
    shapes, pspecs, aliases, pargs = _state_outputs(
        prev, [(depth, bp) + t for t in tails], [(depth, bs) + t for t in tails], len(in_specs), 1)
    state_specs = []
    for t in tails:
        zeros = (0,) * len(t)
        state_specs.append(pl.BlockSpec((1, 1) + t, lambda i, z=zeros: (layer, pseq(i)) + z))
        state_specs.append(pl.BlockSpec((1, 1) + t, lambda i, z=zeros: (layer, sseq(i)) + z))
    out = pl.pallas_call(
        functools.partial(_mlstm_kernel, H=H, dk=dk, dv=dv, steps=steps),
        grid=(steps.n,),
        in_specs=in_specs + pspecs,
        out_specs=[pl.BlockSpec((L, H * dv), lambda i: (i, 0))] + state_specs,
        out_shape=[jax.ShapeDtypeStruct((ntok, H * dv), _BF16)] + shapes,
        input_output_aliases=aliases,
        scratch_shapes=[pltpu.VMEM((H, dk, dv), _F32), pltpu.VMEM((H, dk), _F32),
                        pltpu.VMEM((H, _SUBLANE, _LANE), _F32)],
        compiler_params=_params("arbitrary"),
        name="mlstm_mixer",
    )(h, h, h, h, hs, f_bias_row, norm_g.reshape(1, H * dv), c_in, n_in,
      m_in.reshape(m_in.shape[0], m_in.shape[1], 1, H), *pargs)
    return out[0], out[1:]


def _ret_kernel(lg_ref, q_ref, k_ref, v_ref, og_ref, cos_ref, sin_ref, ng_ref, s0_ref,
                *rest, H, dk, dv, steps):
    o_ref, sp_ref, ss_ref, s_scr = rest[-4:]
    L = _CHUNK
    i = pl.program_id(0)
    first, last = steps.first_last(i)
    in_p = steps.in_prompt(i)

    @pl.when(jnp.logical_and(first, in_p))
    def _():
        s_scr[...] = jnp.zeros(s_scr.shape, _F32)

    @pl.when(jnp.logical_and(first, jnp.logical_not(in_p)))
    def _():
        s_scr[...] = s0_ref[0, 0]

    row, col = _chunk_masks()
    incl = row >= col
    tdiff = (row - col).astype(_F32)
    jc = lax.broadcasted_iota(jnp.int32, (L, 1), 0).astype(_F32)
    cos, sin = cos_ref[...], sin_ref[...]

    def rotary(x):
        return x * cos + pltpu.roll(x, dk // 2, 1) * sin

    for h in range(H):
        ks, vs = slice(h * dk, (h + 1) * dk), slice(h * dv, (h + 1) * dv)
        lg = lg_ref[h]
        q = rotary(q_ref[:, ks])
        k = rotary(k_ref[:, ks]) * dk ** -0.5
        vb = _bf(v_ref[:, vs])
        D = jnp.where(incl, jnp.exp(jnp.where(incl, lg * tdiff, 0.0)), 0.0)
        qk = _dot_nt(_bf(q), _bf(k)) * D
        q_in = q * jnp.exp(lg * (jc + 1.0))
        k_st = k * jnp.exp(lg * (L - 1.0 - jc))
        gL = jnp.exp(lg * jnp.full((1, 1), float(L), _F32))
        S = s_scr[h]
        o = _dot(_bf(qk), vb) + _dot(_bf(q_in), _bf(S))
        s_scr[h] = gL * S + _dot_tn(_bf(k_st), vb)
        o_ref[:, vs] = (_ln_head(o, ng_ref[:, vs]) * _silu(og_ref[:, vs])).astype(o_ref.dtype)

    @pl.when(jnp.logical_and(last, in_p))
    def _():
        sp_ref[0, 0] = s_scr[...]

    @pl.when(jnp.logical_and(last, jnp.logical_not(in_p)))
    def _():
        ss_ref[0, 0] = s_scr[...]


def _retention(h, col0, lg, cos_tab, sin_tab, norm_g, s_in, prev, layer, steps):
    ntok = h.shape[0]
    depth, _, H, dk, dv = s_in.shape
    assert dv % dk == 0 and 2 % (dv // dk) == 0 and col0 % (H * dv) == 0
    L = _CHUNK
    jq, jv = col0 // (H * dk), col0 // (H * dv) + 2 * dk // dv
    tq = lambda j: pl.BlockSpec((L, H * dk), lambda i: (i, jq + j))
    tv = lambda j: pl.BlockSpec((L, H * dv), lambda i: (i, jv + j))
    tab = pl.BlockSpec((L, dk), lambda i: (steps.pos_block(i), 0))
    sseq, pseq = steps.sample_seq, steps.prompt_seq
    in_specs = [pl.BlockSpec(memory_space=pltpu.SMEM),
                tq(0), tq(1), tv(0), tv(1), tab, tab,
                pl.BlockSpec((1, H * dv), lambda i: (0, 0)),
                pl.BlockSpec((1, 1, H, dk, dv), lambda i: (layer, sseq(i), 0, 0, 0))]
    shapes, pspecs, aliases, pargs = _state_outputs(
        prev, [(depth, steps.bp, H, dk, dv)], [(depth, steps.bs, H, dk, dv)], len(in_specs), 1)
    out = pl.pallas_call(
        functools.partial(_ret_kernel, H=H, dk=dk, dv=dv, steps=steps),
        grid=(steps.n,),
        in_specs=in_specs + pspecs,
        out_specs=[pl.BlockSpec((L, H * dv), lambda i: (i, 0)),
                   pl.BlockSpec((1, 1, H, dk, dv), lambda i: (layer, pseq(i), 0, 0, 0)),
                   pl.BlockSpec((1, 1, H, dk, dv), lambda i: (layer, sseq(i), 0, 0, 0))],
        out_shape=[jax.ShapeDtypeStruct((ntok, H * dv), _BF16)] + shapes,
        input_output_aliases=aliases,
        scratch_shapes=[pltpu.VMEM((H, dk, dv), _F32)],
        compiler_params=_params("arbitrary"),
        name="retention_mixer",
    )(lg, h, h, h, h, cos_tab, sin_tab, norm_g.reshape(1, H * dv), s_in, *pargs)
    return out[0], out[1:]


def _rotary_tables(tp, ts, dk):
    half = dk // 2
    freq = _ROPE_BASE ** (-jnp.arange(half, dtype=_F32) / half)
    pos = jnp.concatenate([0.0 + jnp.arange(tp, dtype=_F32), float(_PAST_LEN) + jnp.arange(ts, dtype=_F32)])
    ang = pos[:, None] * freq[None, :]
    cos, sin = jnp.cos(ang), jnp.sin(ang)
    return jnp.concatenate([cos, cos], axis=1), jnp.concatenate([-sin, sin], axis=1)


def kernel(x_prompt, x_sample, state_gdn_S, state_gdn_conv, state_mlstm_C, state_mlstm_n, state_mlstm_m, state_ret_S, w_in, b_in, gdn_conv_w, gdn_A_log, gdn_dt_bias, gdn_norm_g, mlstm_f_bias, mlstm_norm_g, ret_norm_g, w_branch, w_out, ln1_g, ln1_b, ln2_g, ln2_b, w_ffn_up, w_ffn_down):
    bp, tp, D = x_prompt.shape
    bs, ts, _ = x_sample.shape
    depth = w_in.shape[0]
    alpha = (2 * depth) ** 0.25
    Hg, gdk, gdv = state_gdn_S.shape[2:]
    Hm, mdk, mdv = state_mlstm_C.shape[2:]
    Hr, rdk, rdv = state_ret_S.shape[2:]
    nconv = state_gdn_conv.shape[2]
    steps = _Steps(bp, tp, bs, ts)
    np_tok = bp * tp

    widths = (2 * Hg * gdk + Hg * gdv, Hg * gdv, Hg, Hg,
              Hm * mdk, Hm * mdk, Hm * mdv, Hm * mdv, Hm, Hm,
              Hr * rdk, Hr * rdk, Hr * rdv, Hr * rdv, 3 * D)
    offs = [0]
    for w_ in widths:
        offs.append(offs[-1] + w_)
    o_qkv, _, o_b, o_a, o_mq, _, _, _, o_mi, o_mf, o_rq, _, _, _, _, o_end = offs
    assert o_end == w_in.shape[2]
    qkv_w = widths[0]
    wide = ((o_qkv, o_b), (o_mq, o_mi), (o_rq, o_end))
    ml_col0 = o_b - o_qkv
    ret_col0 = ml_col0 + (o_mi - o_mq)
    gate_col0 = ret_col0 + sum(widths[10:14])

    hpg = min(Hg, 16)
    ngrp = Hg // hpg
    assert 2 * hpg <= _LANE

    def wide_cols(a):
        return jnp.concatenate([a[..., lo:hi] for lo, hi in wide], axis=-1)

    def gate_cols(a):
        def block(lo1, lo2, n):
            pad = jnp.zeros(a.shape[:-1] + (_LANE - 2 * n,), a.dtype)
            return [a[..., lo1:lo1 + n], a[..., lo2:lo2 + n], pad]
        parts = []
        for g in range(ngrp):
            parts += block(o_b + g * hpg, o_a + g * hpg, hpg)
        parts += block(o_mi, o_mf, Hm)
        return jnp.concatenate(parts, axis=-1)

    cos_tab, sin_tab = _rotary_tables(tp, ts, rdk)
    lg = jnp.log(1.0 - 2.0 ** (-5.0 - jnp.arange(Hr, dtype=_F32)))

    w_wide, b_wide = _bf(wide_cols(w_in)), wide_cols(b_in)
    w_gate, b_gate = gate_cols(w_in), gate_cols(b_in)
    w_br, w_o, w_dn = _bf(w_branch), _bf(w_out), _bf(w_ffn_down)
    gp = jnp.zeros((depth, ngrp, 2, _LANE), _F32)
    gp = gp.at[:, :, 0, hpg:2 * hpg].set(gdn_A_log.reshape(depth, ngrp, hpg))
    gp = gp.at[:, :, 1, hpg:2 * hpg].set(gdn_dt_bias.reshape(depth, ngrp, hpg))
    fb = jnp.zeros((depth, 1, _LANE), _F32).at[:, 0, Hm:2 * Hm].set(mlstm_f_bias)

    x = jnp.concatenate([x_prompt.reshape(np_tok, D), x_sample.reshape(bs * ts, D)], axis=0)
    xb = x.astype(_BF16)
    g_st = m_st = r_st = None
    conv_p, conv_s = [], []
    for l in range(depth):
        h = _matmul(xb, w_wide, l, bias=b_wide[l], name="in_proj")
        hs = _small_proj(x, w_gate, l, b_gate[l])
        oa, g_st = _gdn(h, hs, gdn_conv_w[l], gp[l], gdn_norm_g[l], state_gdn_S, state_gdn_conv, g_st, l, steps, hpg)
        conv_p.append(h[:np_tok].reshape(bp, tp, -1)[:, tp - nconv:, :qkv_w])
        conv_s.append(h[np_tok:].reshape(bs, ts, -1)[:, ts - nconv:, :qkv_w])
        ob, m_st = _mlstm(h, ml_col0, hs, ngrp, fb[l], mlstm_norm_g[l],
                          state_mlstm_C, state_mlstm_n, state_mlstm_m, m_st, l, steps)
        oc, r_st = _retention(h, ret_col0, lg, cos_tab, sin_tab, ret_norm_g[l], state_ret_S, r_st, l, steps)

        merged = _merge(oa, ob, oc, w_br, l, h, gate_col0)
        z1 = _matmul(merged, w_o, l, res=x, alpha=alpha, name="out_proj")
        x, xb = _layernorm(z1, ln1_g[l], ln1_b[l])

        act = _ffn_up(xb, w_ffn_up, l)
        z2 = _matmul(act, w_dn, l, res=x, alpha=alpha, bm=512, bn=512, name="ffn_down")
        x, xb = _layernorm(z2, ln2_g[l], ln2_b[l])

    gS_p, gS_s = g_st
    mC_p, mC_s, mn_p, mn_s, mm_p, mm_s = m_st
    rS_p, rS_s = r_st
    yp = x[:np_tok].reshape(bp, tp, D)
    ys = x[np_tok:].reshape(bs, ts, D)
    return (yp, ys,
            gS_p, jnp.stack(conv_p), mC_p, mn_p, mm_p[:, :, 0, :Hm], rS_p,
            gS_s, jnp.stack(conv_s), mC_s, mn_s, mm_s[:, :, 0, :Hm], rS_s)
```

```python
import functools

import jax
import jax.numpy as jnp
from jax import lax
from jax.experimental import pallas as pl
from jax.experimental.pallas import tpu as pltpu

_F32 = jnp.float32
_BF16 = jnp.bfloat16
_HI = lax.Precision.HIGHEST

_CHUNK = 64
_PAST_LEN = 1024
_ROPE_BASE = 10000.0
_LN_EPS = 1e-5
_LANE = 128
_SUBLANE = 8
_VMEM_LIMIT = 56 * 1024 * 1024


def _tile(dim, pref, align=_LANE):
    best = None
    t = align
    while t <= min(dim, pref):
        if dim % t == 0:
            best = t
        t += align
    return dim if best is None else best


def _params(*sem):
    return pltpu.CompilerParams(dimension_semantics=sem, vmem_limit_bytes=_VMEM_LIMIT)


def _bf(x):
    return x.astype(_BF16)


def _dot(a, b):
    return jnp.dot(a, b, preferred_element_type=_F32)


def _dot_nt(a, b):
    return lax.dot_general(a, b, (((1,), (1,)), ((), ())), preferred_element_type=_F32)


def _dot_tn(a, b):
    return lax.dot_general(a, b, (((0,), (0,)), ((), ())), preferred_element_type=_F32)


def _dot_hi(a, b):
    return jnp.dot(a, b, precision=_HI, preferred_element_type=_F32)


def _split(x):
    hi = x.astype(_BF16)
    return hi, (x - hi.astype(_F32)).astype(_BF16)


def _mm3(a, b):
    return _dot(a[0], b[0]) + (_dot(a[0], b[1]) + _dot(a[1], b[0]))


def _mm_kernel(*refs, nk, has_bias, has_res, alpha):
    it = iter(refs)
    x_ref, w_ref = next(it), next(it)
    b_ref = next(it) if has_bias else None
    r_ref = next(it) if has_res else None
    o_ref = next(it)
    acc_ref = next(it) if nk > 1 else None

    def finish(v):
        if has_bias:
            v = v + b_ref[...]
        if has_res:
            v = alpha * r_ref[...] + v
        o_ref[...] = v.astype(o_ref.dtype)

    part = jnp.dot(x_ref[...], w_ref[0], preferred_element_type=_F32)
    if nk == 1:
        finish(part)
    else:
        k = pl.program_id(2)

        @pl.when(k == 0)
        def _():
            acc_ref[...] = part

        @pl.when(k > 0)
        def _():
            acc_ref[...] += part

        @pl.when(k == nk - 1)
        def _():
            finish(acc_ref[...])


def _matmul(x, w, layer, *, bias=None, res=None, alpha=1.0, out_dtype=_F32, bm=1024, bn=1024, bk=None, name):
    M, K = x.shape
    N = w.shape[2]
    bm, bn = _tile(M, bm), _tile(N, bn)
    bk = K if bk is None else _tile(K, bk)
    nk = K // bk
    in_specs = [pl.BlockSpec((bm, bk), lambda i, j, k: (i, k)),
                pl.BlockSpec((1, bk, bn), lambda i, j, k: (layer, k, j))]
    args = [x, w]
    if bias is not None:
        in_specs.append(pl.BlockSpec((1, bn), lambda i, j, k: (0, j)))
        args.append(bias.reshape(1, N).astype(_F32))
    if res is not None:
        in_specs.append(pl.BlockSpec((bm, bn), lambda i, j, k: (i, j)))
        args.append(res)
    return pl.pallas_call(
        functools.partial(_mm_kernel, nk=nk, has_bias=bias is not None, has_res=res is not None, alpha=alpha),
        grid=(M // bm, N // bn, nk),
        in_specs=in_specs,
        out_specs=pl.BlockSpec((bm, bn), lambda i, j, k: (i, j)),
        out_shape=jax.ShapeDtypeStruct((M, N), out_dtype),
        scratch_shapes=[pltpu.VMEM((bm, bn), _F32)] if nk > 1 else [],
        compiler_params=_params("parallel", "parallel", "arbitrary"),
        name=name,
    )(*args)


_REGROUP_EXTRA = 64


def _regroup_kernel(a_ref, b_ref, o_ref, *, regions):
    j = pl.program_id(2)

    def emit(lo, hi, shift):
        @pl.when(jnp.logical_and(j >= lo, j < hi))
        def _():
            if shift == 0:
                t = a_ref[0]
            else:
                t = jnp.concatenate([a_ref[0, shift:, :], b_ref[0, :shift, :]], axis=0)
            o_ref[0] = _bf(t.T)

    for lo, hi, shift in regions:
        emit(lo, hi, shift)


def _regroup_weights(wt, spans):
    depth, N, K = wt.shape
    starts, shifts, pos = [], [], 0
    for lo, hi in spans:
        starts.append(pos)
        shifts.append(lo - pos)
        pos += hi - lo
    assert all(0 <= s <= _REGROUP_EXTRA and s % _SUBLANE == 0 for s in shifts)
    bn = _LANE
    for t in range(_LANE, 512 + 1, _LANE):
        if all(v % t == 0 for v in starts + [pos]):
            bn = t
    bk = _tile(K, 1024)
    last_extra_block = (N - 1) // _REGROUP_EXTRA
    bounds = [s // bn for s in starts] + [pos // bn]
    regions = tuple((bounds[r], bounds[r + 1], shifts[r]) for r in range(len(spans)))
    return pl.pallas_call(
        functools.partial(_regroup_kernel, regions=regions),
        grid=(depth, K // bk, pos // bn),
        in_specs=[pl.BlockSpec((1, bn, bk), lambda l, i, j: (l, j, i)),
                  pl.BlockSpec((1, _REGROUP_EXTRA, bk),
                               lambda l, i, j: (l, jnp.minimum((j + 1) * (bn // _REGROUP_EXTRA), last_extra_block), i))],
        out_specs=pl.BlockSpec((1, bk, bn), lambda l, i, j: (l, i, j)),
        out_shape=jax.ShapeDtypeStruct((depth, K, pos), _BF16),
        compiler_params=_params("parallel", "parallel", "parallel"),
        name="regroup_w_in",
    )(wt, wt)


def _small_proj_kernel(x_ref, *refs, nblk):
    xs = _split(x_ref[...])
    w_refs, b_refs, o_ref = refs[:nblk], refs[nblk:2 * nblk], refs[2 * nblk]
    for n in range(nblk):
        wh, wl = _split(w_refs[n][0])
        acc = _dot_nt(xs[0], wh) + (_dot_nt(xs[0], wl) + _dot_nt(xs[1], wh))
        o_ref[:, n * _LANE:(n + 1) * _LANE] = acc + b_refs[n][0]


def _small_proj(x, wt, b, layer, row_blocks):
    M, K = x.shape
    depth, N = b.shape
    bm = _tile(M, 256)
    nblk = len(row_blocks)
    wspec = lambda c: pl.BlockSpec((1, _LANE, K), lambda i: (layer, c, 0))
    bspec = lambda c: pl.BlockSpec((1, 1, _LANE), lambda i: (layer, 0, c))
    return pl.pallas_call(
        functools.partial(_small_proj_kernel, nblk=nblk),
        grid=(M // bm,),
        in_specs=[pl.BlockSpec((bm, K), lambda i: (i, 0))] + [wspec(c) for c in row_blocks] + [bspec(c) for c in row_blocks],
        out_specs=pl.BlockSpec((bm, nblk * _LANE), lambda i: (i, 0)),
        out_shape=jax.ShapeDtypeStruct((M, nblk * _LANE), _F32),
        compiler_params=_params("parallel"),
        name="small_proj",
    )(x, *([wt] * nblk), *([b.reshape(depth, 1, N)] * nblk))


def _ln_kernel(z_ref, g_ref, b_ref, o_ref, ob_ref):
    z = z_ref[...]
    mu = jnp.mean(z, axis=-1, keepdims=True)
    d = z - mu
    var = jnp.mean(d * d, axis=-1, keepdims=True)
    y = d * lax.rsqrt(var + _LN_EPS) * g_ref[...] + b_ref[...]
    o_ref[...] = y
    ob_ref[...] = y.astype(_BF16)


def _layernorm(z, g, b):
    M, D = z.shape
    bm = _tile(M, 256)
    return pl.pallas_call(
        _ln_kernel,
        grid=(M // bm,),
        in_specs=[pl.BlockSpec((bm, D), lambda i: (i, 0)),
                  pl.BlockSpec((1, D), lambda i: (0, 0)),
                  pl.BlockSpec((1, D), lambda i: (0, 0))],
        out_specs=[pl.BlockSpec((bm, D), lambda i: (i, 0)),
                   pl.BlockSpec((bm, D), lambda i: (i, 0))],
        out_shape=[jax.ShapeDtypeStruct((M, D), _F32), jax.ShapeDtypeStruct((M, D), _BF16)],
        compiler_params=_params("parallel"),
        name="layernorm",
    )(z, g.reshape(1, D), b.reshape(1, D))


def _merge_kernel(a_ref, b_ref, c_ref, w_ref, g_ref, o_ref, acc_ref):
    br = pl.program_id(2)
    gate = jax.nn.sigmoid(g_ref[...])

    def contrib(x_ref):
        return gate * jnp.dot(x_ref[...], w_ref[0, 0], preferred_element_type=_F32)

    @pl.when(br == 0)
    def _():
        acc_ref[...] = contrib(a_ref)

    @pl.when(br == 1)
    def _():
        acc_ref[...] += contrib(b_ref)

    @pl.when(br == 2)
    def _():
        o_ref[...] = (acc_ref[...] + contrib(c_ref)).astype(o_ref.dtype)


def _merge(oa, ob, oc, w_branch, layer, h, gate_col0):
    M, BW = oa.shape
    D = w_branch.shape[3]
    bm, bn = _tile(M, 1024), _tile(D, 1024)
    nj = D // bn
    assert gate_col0 % bn == 0
    g0 = gate_col0 // bn
    xspec = pl.BlockSpec((bm, BW), lambda i, j, n: (i, 0))
    return pl.pallas_call(
        _merge_kernel,
        grid=(M // bm, nj, 3),
        in_specs=[xspec, xspec, xspec,
                  pl.BlockSpec((1, 1, BW, bn), lambda i, j, n: (layer, n, 0, j)),
                  pl.BlockSpec((bm, bn), lambda i, j, n: (i, g0 + n * nj + j))],
        out_specs=pl.BlockSpec((bm, bn), lambda i, j, n: (i, j)),
        out_shape=jax.ShapeDtypeStruct((M, D), _BF16),
        scratch_shapes=[pltpu.VMEM((bm, bn), _F32)],
        compiler_params=_params("parallel", "parallel", "arbitrary"),
        name="branch_merge",
    )(oa, ob, oc, w_branch, h)


def _ffn_up_kernel(x_ref, wg_ref, wv_ref, o_ref):
    x = x_ref[...]
    g = jnp.dot(x, _bf(wg_ref[0]), preferred_element_type=_F32)
    v = jnp.dot(x, _bf(wv_ref[0]), preferred_element_type=_F32)
    o_ref[...] = (g * jax.nn.sigmoid(g) * v).astype(o_ref.dtype)


def _ffn_up(x, w_up, layer):
    M, K = x.shape
    F = w_up.shape[2] // 2
    bm, bn = _tile(M, 1024), _tile(F, 256)
    nj = F // bn
    return pl.pallas_call(
        _ffn_up_kernel,
        grid=(M // bm, nj),
        in_specs=[pl.BlockSpec((bm, K), lambda i, j: (i, 0)),
                  pl.BlockSpec((1, K, bn), lambda i, j: (layer, 0, j)),
                  pl.BlockSpec((1, K, bn), lambda i, j: (layer, 0, nj + j))],
        out_specs=pl.BlockSpec((bm, bn), lambda i, j: (i, j)),
        out_shape=jax.ShapeDtypeStruct((M, F), _BF16),
        compiler_params=_params("parallel", "parallel"),
        name="ffn_up",
    )(x, w_up, w_up)


def _transpose_hi(x):
    n = x.shape[1]
    eye = (lax.broadcasted_iota(jnp.int32, (n, n), 0) == lax.broadcasted_iota(jnp.int32, (n, n), 1)).astype(_F32)
    return lax.dot_general(eye, x, (((1,), (1,)), ((), ())), precision=_HI, preferred_element_type=_F32)


def _silu(x):
    return x * jax.nn.sigmoid(x)


def _softplus(x):
    return jnp.maximum(x, 0.0) + jnp.log1p(jnp.exp(-jnp.abs(x)))


def _chunk_masks():
    L = _CHUNK
    row = lax.broadcasted_iota(jnp.int32, (L, L), 0)
    col = lax.broadcasted_iota(jnp.int32, (L, L), 1)
    return row, col


def _ln_head(o, g):
    mu = jnp.mean(o, axis=-1, keepdims=True)
    d = o - mu
    var = jnp.mean(d * d, axis=-1, keepdims=True)
    return d * lax.rsqrt(var + _LN_EPS) * g


def _unit_lower_inverses(As, row, col):
    same16 = (row // 16) == (col // 16)
    same32 = (row // 32) == (col // 32)
    eye = (row == col).astype(_F32)
    D = [jnp.where(same16, A, 0.0) for A in As]
    Ds = [_split(d) for d in D]
    D2s = [_split(_mm3(d, d)) for d in Ds]
    P = [eye - d for d in D]
    P = [p + _mm3(_split(p), d2) for p, d2 in zip(P, D2s)]
    D4s = [_split(_mm3(d2, d2)) for d2 in D2s]
    P = [p + _mm3(_split(p), d4) for p, d4 in zip(P, D4s)]
    D8s = [_split(_mm3(d4, d4)) for d4 in D4s]
    P = [p + _mm3(_split(p), d8) for p, d8 in zip(P, D8s)]
    for off_mask in (jnp.logical_and(same32, jnp.logical_not(same16)), jnp.logical_not(same32)):
        Es = [_split(jnp.where(off_mask, A, 0.0)) for A in As]
        Ps = [_split(p) for p in P]
        Qs = [_split(_mm3(e, ps)) for e, ps in zip(Es, Ps)]
        P = [p - _mm3(ps, q) for p, ps, q in zip(P, Ps, Qs)]
    return P


class _Steps:
    def __init__(self, bp, tp, bs, ts):
        assert tp % _CHUNK == 0 and ts % _CHUNK == 0
        self.bp, self.bs = bp, bs
        self.ncp, self.ncs = tp // _CHUNK, ts // _CHUNK
        self.np_steps = bp * self.ncp
        self.n = self.np_steps + bs * self.ncs

    def in_prompt(self, i):
        return i < self.np_steps

    def _sample_step(self, i):
        return jnp.maximum(i - self.np_steps, 0)

    def seq(self, i):
        return jnp.where(self.in_prompt(i), i // self.ncp, self.bp + self._sample_step(i) // self.ncs)

    def sample_seq(self, i):
        return self._sample_step(i) // self.ncs

    def prompt_seq(self, i):
        return jnp.minimum(i // self.ncp, self.bp - 1)

    def chunk(self, i):
        return jnp.where(self.in_prompt(i), i % self.ncp, self._sample_step(i) % self.ncs)

    def pos_block(self, i):
        return jnp.where(self.in_prompt(i), self.chunk(i), self.ncp + self.chunk(i))

    def first_last(self, i):
        c = self.chunk(i)
        return c == 0, jnp.where(self.in_prompt(i), c == self.ncp - 1, c == self.ncs - 1)


def _gdn_kernel(q_ref, k_ref, v_ref, z_ref, gs_ref, cwq_ref, cwk_ref, cwv_ref, gp_ref, ng_ref,
                s0_ref, c0q_ref, c0k_ref, c0v_ref, *rest, hpg, dk, dv, lane0, steps):
    o_ref, sp_ref, ss_ref, s_scr, bq, bk, bv = rest[-7:]
    L, P = _CHUNK, _SUBLANE
    i = pl.program_id(1)
    first, last = steps.first_last(i)
    in_p = steps.in_prompt(i)
    nc = c0q_ref.shape[2]

    @pl.when(jnp.logical_and(first, in_p))
    def _():
        s_scr[...] = jnp.zeros(s_scr.shape, _F32)
        for buf in (bq, bk, bv):
            buf[0:P, :] = jnp.zeros((P, buf.shape[1]), _F32)

    @pl.when(jnp.logical_and(first, jnp.logical_not(in_p)))
    def _():
        s_scr[...] = s0_ref[0, 0]
        bq[P - nc:P, :] = c0q_ref[0, 0]
        bk[P - nc:P, :] = c0k_ref[0, 0]
        bv[P - nc:P, :] = c0v_ref[0, 0]

    bq[P:P + L, :] = q_ref[...]
    bk[P:P + L, :] = k_ref[...]
    bv[P:P + L, :] = v_ref[...]

    row, col = _chunk_masks()
    incl, strict = row >= col, row > col
    tril = incl.astype(_F32)
    gs = gs_ref[...]
    gp = gp_ref[0]
    beta_blk = jax.nn.sigmoid(gs)
    g_blk = -jnp.exp(gp[0:1, :]) * _softplus(gs + gp[1:2, :])
    G_blk = _dot_hi(tril, g_blk)
    GT = _transpose_hi(G_blk)
    ng = ng_ref[...]
    heads = range(hpg)

    def conv(buf, cw_ref, cs):
        out = buf[P - nc:P - nc + L, cs] * cw_ref[0:1, cs]
        for j in range(1, nc + 1):
            out = out + buf[P - nc + j:P - nc + j + L, cs] * cw_ref[j:j + 1, cs]
        return out

    def l2norm(x):
        return x * lax.rsqrt(jnp.sum(x * x, axis=-1, keepdims=True) + 1e-6)

    ks = [slice(h * dk, (h + 1) * dk) for h in heads]
    vs = [slice(h * dv, (h + 1) * dv) for h in heads]
    q = [l2norm(_silu(conv(bq, cwq_ref, ks[h]))) * dk ** -0.5 for h in heads]
    k = [l2norm(_silu(conv(bk, cwk_ref, ks[h]))) for h in heads]
    v = [_silu(conv(bv, cwv_ref, vs[h])) for h in heads]
    la = lane0 + hpg
    beta = [beta_blk[:, lane0 + h:lane0 + h + 1] for h in heads]
    Gc = [G_blk[:, la + h:la + h + 1] for h in heads]
    gam = [jnp.where(incl, jnp.exp(jnp.where(incl, Gc[h] - GT[la + h:la + h + 1, :], 0.0)), 0.0) for h in heads]
    kb = [_bf(k[h]) for h in heads]
    kq = [_dot_nt(jnp.concatenate([kb[h], _bf(q[h])], axis=0), kb[h]) for h in heads]
    A = [jnp.where(strict, beta[h] * kq[h][0:L] * gam[h], 0.0) for h in heads]
    qk = [_bf(kq[h][L:2 * L] * gam[h]) for h in heads]
    eG = [jnp.exp(Gc[h]) for h in heads]
    T = _unit_lower_inverses(A, row, col)
    rhs = [jnp.concatenate([beta[h] * v[h], (beta[h] * eG[h]) * k[h]], axis=1) for h in heads]
    UW = [_mm3(_split(T[h]), _split(rhs[h])) for h in heads]
    G_last = [Gc[h][L - 1:L, :] for h in heads]
    S = [s_scr[h] for h in heads]
    Sb = [_bf(S[h]) for h in heads]
    WQ = [_dot(jnp.concatenate([_bf(UW[h][:, dv:]), _bf(q[h] * eG[h])], axis=0), Sb[h]) for h in heads]
    db = [_bf(UW[h][:, 0:dv] - WQ[h][0:L]) for h in heads]
    o = [WQ[h][L:2 * L] + _dot(qk[h], db[h]) for h in heads]
    for h in heads:
        k_dec = k[h] * jnp.exp(G_last[h] - Gc[h])
        s_scr[h] = jnp.exp(G_last[h]) * S[h] + _dot_tn(_bf(k_dec), db[h])
    for h in heads:
        on = o[h] * lax.rsqrt(jnp.mean(o[h] * o[h], axis=-1, keepdims=True) + 1e-6) * ng
        o_ref[:, vs[h]] = (on * _silu(z_ref[:, vs[h]])).astype(o_ref.dtype)

    bq[0:P, :] = bq[L:L + P, :]
    bk[0:P, :] = bk[L:L + P, :]
    bv[0:P, :] = bv[L:L + P, :]

    @pl.when(jnp.logical_and(last, in_p))
    def _():
        sp_ref[0, 0] = s_scr[...]

    @pl.when(jnp.logical_and(last, jnp.logical_not(in_p)))
    def _():
        ss_ref[0, 0] = s_scr[...]


def _state_outputs(prev, shapes_p, shapes_s, n_in, n_lead_out):
    shapes = [jax.ShapeDtypeStruct(s, _F32) for pair in zip(shapes_p, shapes_s) for s in pair]
    if prev is None:
        return shapes, [], {}, ()
    specs = [pl.BlockSpec(memory_space=pl.ANY)] * len(prev)
    aliases = {n_in + j: n_lead_out + j for j in range(len(prev))}
    return shapes, specs, aliases, tuple(prev)


def _gdn(h, hs, gate_block, lane0, conv_w, gp, norm_g, s_in, conv_in, prev, layer, steps):
    ntok = h.shape[0]
    depth, _, H, dk, dv = s_in.shape
    nc = conv_in.shape[2]
    assert dk == dv and nc < _SUBLANE and conv_w.shape[0] == nc + 1 and lane0 + 2 * H <= _LANE
    hpg, ng_ = H, 1
    L, P = _CHUNK, _SUBLANE
    wq = hpg * dk
    sseq, pseq = steps.sample_seq, steps.prompt_seq
    tok = lambda off: pl.BlockSpec((L, wq), lambda g, i: (i, off + g))
    cw = lambda off: pl.BlockSpec((nc + 1, wq), lambda g, i: (0, off + g))
    c0 = lambda off: pl.BlockSpec((1, 1, nc, wq), lambda g, i: (layer, sseq(i), 0, off + g))
    in_specs = [tok(0), tok(ng_), tok(2 * ng_), tok(3 * ng_),
                pl.BlockSpec((L, _LANE), lambda g, i: (i, gate_block)),
                cw(0), cw(ng_), cw(2 * ng_),
                pl.BlockSpec((1, 2, _LANE), lambda g, i: (layer, 0, 0)),
                pl.BlockSpec((1, dv), lambda g, i: (0, 0)),
                pl.BlockSpec((1, 1, hpg, dk, dv), lambda g, i: (layer, sseq(i), g, 0, 0)),
                c0(0), c0(ng_), c0(2 * ng_)]
    shapes, pspecs, aliases, pargs = _state_outputs(
        prev, [(depth, steps.bp, H, dk, dv)], [(depth, steps.bs, H, dk, dv)], len(in_specs), 1)
    out = pl.pallas_call(
        functools.partial(_gdn_kernel, hpg=hpg, dk=dk, dv=dv, lane0=lane0, steps=steps),
        grid=(ng_, steps.n),
        in_specs=in_specs + pspecs,
        out_specs=[pl.BlockSpec((L, hpg * dv), lambda g, i: (i, g)),
                   pl.BlockSpec((1, 1, hpg, dk, dv), lambda g, i: (layer, pseq(i), g, 0, 0)),
                   pl.BlockSpec((1, 1, hpg, dk, dv), lambda g, i: (layer, sseq(i), g, 0, 0))],
        out_shape=[jax.ShapeDtypeStruct((ntok, H * dv), _BF16)] + shapes,
        input_output_aliases=aliases,
        scratch_shapes=[pltpu.VMEM((hpg, dk, dv), _F32)] + [pltpu.VMEM((P + L, wq), _F32)] * 3,
        compiler_params=_params("parallel", "arbitrary"),
        name="gdn_mixer",
    )(h, h, h, h, hs, conv_w, conv_w, conv_w, gp, norm_g.reshape(1, dv), s_in, conv_in, conv_in, conv_in, *pargs)
    return out[0], out[1:]


def _mlstm_kernel(q_ref, k_ref, v_ref, og_ref, gs_ref, fb_ref, ng_ref, c0_ref, n0_ref, m0_ref,
                  *rest, H, dk, dv, lane0, steps):
    o_ref, cp_ref, cs_ref, np_ref, ns_ref, mp_ref, ms_ref, c_scr, n_scr, m_scr = rest[-10:]
    L, P = _CHUNK, _SUBLANE
    i = pl.program_id(0)
    first, last = steps.first_last(i)
    in_p = steps.in_prompt(i)

    @pl.when(jnp.logical_and(first, in_p))
    def _():
        c_scr[...] = jnp.zeros(c_scr.shape, _F32)
        n_scr[...] = jnp.zeros(n_scr.shape, _F32)
        m_scr[...] = jnp.zeros(m_scr.shape, _F32)

    @pl.when(jnp.logical_and(first, jnp.logical_not(in_p)))
    def _():
        c_scr[...] = c0_ref[0, 0]
        n_scr[...] = n0_ref[0, 0]
        m0 = m0_ref[0, 0]
        for h in range(H):
            m_scr[h] = jnp.broadcast_to(m0[:, h:h + 1], (P, _LANE))

    row, col = _chunk_masks()
    incl = row >= col
    tril = incl.astype(_F32)
    gs = gs_ref[...]
    li, lf = lane0, lane0 + H
    lf_blk = -_softplus(-(gs + fb_ref[0]))
    F_blk = _dot_hi(tril, lf_blk)
    FT = _transpose_hi(F_blk)
    gsT = _transpose_hi(gs)

    for h in range(H):
        ks, vs = slice(h * dk, (h + 1) * dk), slice(h * dv, (h + 1) * dv)
        q = q_ref[:, ks] * dk ** -0.5
        k = k_ref[:, ks]
        qb, kb, vb = _bf(q), _bf(k), _bf(v_ref[:, vs])
        igc = gs[:, li + h:li + h + 1]
        Fc = F_blk[:, lf + h:lf + h + 1]
        logD = jnp.where(incl, Fc - FT[lf + h:lf + h + 1, :] + gsT[li + h:li + h + 1, :], -jnp.inf)
        m_intra = jnp.max(logD, axis=-1, keepdims=True)
        m_prev = m_scr[h][0:1, 0:1]
        m_t = jnp.maximum(Fc + m_prev, m_intra)
        Dm = jnp.exp(logD - m_t)
        Sm = Dm * _dot_nt(qb, kb)
        inter = jnp.exp(Fc + m_prev - m_t)
        C = c_scr[h]
        n = n_scr[h:h + 1, :]
        num = inter * _dot(qb, _bf(C)) + _dot(_bf(Sm), vb)
        den = inter * jnp.sum(q * n, axis=-1, keepdims=True) + jnp.sum(Sm, axis=-1, keepdims=True)
        hh = num / jnp.maximum(jnp.abs(den), jnp.exp(-m_t))
        m_last = m_t[L - 1:L, :]
        w_end = jnp.exp(Fc[L - 1:L, :] - Fc + igc - m_last)
        dec = inter[L - 1:L, :]
        wk = w_end * k
        c_scr[h] = dec * C + _dot_tn(_bf(wk), vb)
        n_scr[h:h + 1, :] = dec * n + jnp.sum(wk, axis=0, keepdims=True)
        m_scr[h] = jnp.broadcast_to(m_last, (P, _LANE))
        o_ref[:, vs] = (_ln_head(hh, ng_ref[:, vs]) * jax.nn.sigmoid(og_ref[:, vs])).astype(o_ref.dtype)

    def write_state(c_ref, n_ref, m_ref):
        c_ref[0, 0] = c_scr[...]
        n_ref[0, 0] = n_scr[...]
        lane = lax.broadcasted_iota(jnp.int32, (1, _LANE), 1)
        m_row = jnp.zeros((1, _LANE), _F32)
        for h in range(H):
            m_row = jnp.where(lane == h, m_scr[h][0:1, :], m_row)
        m_ref[0, 0] = m_row

    @pl.when(jnp.logical_and(last, in_p))
    def _():
        write_state(cp_ref, np_ref, mp_ref)

    @pl.when(jnp.logical_and(last, jnp.logical_not(in_p)))
    def _():
        write_state(cs_ref, ns_ref, ms_ref)


def _mlstm(h, col0, hs, gate_block, lane0, f_bias, norm_g, c_in, n_in, m_in, prev, layer, steps):
    ntok = h.shape[0]
    depth, _, H, dk, dv = c_in.shape
    assert dv % dk == 0 and 2 % (dv // dk) == 0 and lane0 + 2 * H <= _LANE and col0 % (H * dv) == 0
    L = _CHUNK
    bp, bs = steps.bp, steps.bs
    jq, jv = col0 // (H * dk), col0 // (H * dv) + 2 * dk // dv
    tq = lambda j: pl.BlockSpec((L, H * dk), lambda i: (i, jq + j))
    tv = lambda j: pl.BlockSpec((L, H * dv), lambda i: (i, jv + j))
    sseq, pseq = steps.sample_seq, steps.prompt_seq
    in_specs = [tq(0), tq(1), tv(0), tv(1),
                pl.BlockSpec((L, _LANE), lambda i: (i, gate_block)),
                pl.BlockSpec((1, 1, _LANE), lambda i: (layer, 0, 0)),
                pl.BlockSpec((1, H * dv), lambda i: (0, 0)),
                pl.BlockSpec((1, 1, H, dk, dv), lambda i: (layer, sseq(i), 0, 0, 0)),
                pl.BlockSpec((1, 1, H, dk), lambda i: (layer, sseq(i), 0, 0)),
                pl.BlockSpec((1, 1, 1, H), lambda i: (layer, sseq(i), 0, 0))]
    tails = [(H, dk, dv), (H, dk), (1, _LANE)]
    shapes, pspecs, aliases, pargs = _state_outputs(
        prev, [(depth, bp) + t for t in tails], [(depth, bs) + t for t in tails], len(in_specs), 1)
    state_specs = []
    for t in tails:
        zeros = (0,) * len(t)
        state_specs.append(pl.BlockSpec((1, 1) + t, lambda i, z=zeros: (layer, pseq(i)) + z))
        state_specs.append(pl.BlockSpec((1, 1) + t, lambda i, z=zeros: (layer, sseq(i)) + z))
    out = pl.pallas_call(
        functools.partial(_mlstm_kernel, H=H, dk=dk, dv=dv, lane0=lane0, steps=steps),
        grid=(steps.n,),
        in_specs=in_specs + pspecs,
        out_specs=[pl.BlockSpec((L, H * dv), lambda i: (i, 0))] + state_specs,
        out_shape=[jax.ShapeDtypeStruct((ntok, H * dv), _BF16)] + shapes,
        input_output_aliases=aliases,
        scratch_shapes=[pltpu.VMEM((H, dk, dv), _F32), pltpu.VMEM((H, dk), _F32),
                        pltpu.VMEM((H, _SUBLANE, _LANE), _F32)],
        compiler_params=_params("arbitrary"),
        name="mlstm_mixer",
    )(h, h, h, h, hs, f_bias, norm_g.reshape(1, H * dv), c_in, n_in,
      m_in.reshape(m_in.shape[0], m_in.shape[1], 1, H), *pargs)
    return out[0], out[1:]


def _ret_kernel(lg_ref, q_ref, k_ref, v_ref, og_ref, cos_ref, sin_ref, ng_ref, s0_ref,
                *rest, H, dk, dv, steps):
    o_ref, sp_ref, ss_ref, s_scr = rest[-4:]
    L = _CHUNK
    i = pl.program_id(0)
    first, last = steps.first_last(i)
    in_p = steps.in_prompt(i)

    @pl.when(jnp.logical_and(first, in_p))
    def _():
        s_scr[...] = jnp.zeros(s_scr.shape, _F32)

    @pl.when(jnp.logical_and(first, jnp.logical_not(in_p)))
    def _():
        s_scr[...] = s0_ref[0, 0]

    row, col = _chunk_masks()
    incl = row >= col
    tdiff = (row - col).astype(_F32)
    jc = lax.broadcasted_iota(jnp.int32, (L, 1), 0).astype(_F32)
    cos, sin = cos_ref[...], sin_ref[...]

    def rotary(x):
        return x * cos + pltpu.roll(x, dk // 2, 1) * sin

    for h in range(H):
        ks, vs = slice(h * dk, (h + 1) * dk), slice(h * dv, (h + 1) * dv)
        lg = lg_ref[h]
        q = rotary(q_ref[:, ks])
        k = rotary(k_ref[:, ks]) * dk ** -0.5
        vb = _bf(v_ref[:, vs])
        D = jnp.where(incl, jnp.exp(jnp.where(incl, lg * tdiff, 0.0)), 0.0)
        qk = _dot_nt(_bf(q), _bf(k)) * D
        q_in = q * jnp.exp(lg * (jc + 1.0))
        k_st = k * jnp.exp(lg * (L - 1.0 - jc))
        gL = jnp.exp(lg * jnp.full((1, 1), float(L), _F32))
        S = s_scr[h]
        o = _dot(_bf(qk), vb) + _dot(_bf(q_in), _bf(S))
        s_scr[h] = gL * S + _dot_tn(_bf(k_st), vb)
        o_ref[:, vs] = (_ln_head(o, ng_ref[:, vs]) * _silu(og_ref[:, vs])).astype(o_ref.dtype)

    @pl.when(jnp.logical_and(last, in_p))
    def _():
        sp_ref[0, 0] = s_scr[...]

    @pl.when(jnp.logical_and(last, jnp.logical_not(in_p)))
    def _():
        ss_ref[0, 0] = s_scr[...]


def _retention(h, col0, lg, cos_tab, sin_tab, norm_g, s_in, prev, layer, steps):
    ntok = h.shape[0]
    depth, _, H, dk, dv = s_in.shape
    assert dv % dk == 0 and 2 % (dv // dk) == 0 and col0 % (H * dv) == 0
    L = _CHUNK
    jq, jv = col0 // (H * dk), col0 // (H * dv) + 2 * dk // dv
    tq = lambda j: pl.BlockSpec((L, H * dk), lambda i: (i, jq + j))
    tv = lambda j: pl.BlockSpec((L, H * dv), lambda i: (i, jv + j))
    tab = pl.BlockSpec((L, dk), lambda i: (steps.pos_block(i), 0))
    sseq, pseq = steps.sample_seq, steps.prompt_seq
    in_specs = [pl.BlockSpec(memory_space=pltpu.SMEM),
                tq(0), tq(1), tv(0), tv(1), tab, tab,
                pl.BlockSpec((1, H * dv), lambda i: (0, 0)),
                pl.BlockSpec((1, 1, H, dk, dv), lambda i: (layer, sseq(i), 0, 0, 0))]
    shapes, pspecs, aliases, pargs = _state_outputs(
        prev, [(depth, steps.bp, H, dk, dv)], [(depth, steps.bs, H, dk, dv)], len(in_specs), 1)
    out = pl.pallas_call(
        functools.partial(_ret_kernel, H=H, dk=dk, dv=dv, steps=steps),
        grid=(steps.n,),
        in_specs=in_specs + pspecs,
        out_specs=[pl.BlockSpec((L, H * dv), lambda i: (i, 0)),
                   pl.BlockSpec((1, 1, H, dk, dv), lambda i: (layer, pseq(i), 0, 0, 0)),
                   pl.BlockSpec((1, 1, H, dk, dv), lambda i: (layer, sseq(i), 0, 0, 0))],
        out_shape=[jax.ShapeDtypeStruct((ntok, H * dv), _BF16)] + shapes,
        input_output_aliases=aliases,
        scratch_shapes=[pltpu.VMEM((H, dk, dv), _F32)],
        compiler_params=_params("arbitrary"),
        name="retention_mixer",
    )(lg, h, h, h, h, cos_tab, sin_tab, norm_g.reshape(1, H * dv), s_in, *pargs)
    return out[0], out[1:]


def _rotary_tables(tp, ts, dk):
    half = dk // 2
    freq = _ROPE_BASE ** (-jnp.arange(half, dtype=_F32) / half)
    pos = jnp.concatenate([0.0 + jnp.arange(tp, dtype=_F32), float(_PAST_LEN) + jnp.arange(ts, dtype=_F32)])
    ang = pos[:, None] * freq[None, :]
    cos, sin = jnp.cos(ang), jnp.sin(ang)
    return jnp.concatenate([cos, cos], axis=1), jnp.concatenate([-sin, sin], axis=1)


def kernel(x_prompt, x_sample, state_gdn_S, state_gdn_conv, state_mlstm_C, state_mlstm_n, state_mlstm_m, state_ret_S, w_in, b_in, gdn_conv_w, gdn_A_log, gdn_dt_bias, gdn_norm_g, mlstm_f_bias, mlstm_norm_g, ret_norm_g, w_branch, w_out, ln1_g, ln1_b, ln2_g, ln2_b, w_ffn_up, w_ffn_down):
    bp, tp, D = x_prompt.shape
    bs, ts, _ = x_sample.shape
    depth = w_in.shape[0]
    alpha = (2 * depth) ** 0.25
    Hg, gdk, gdv = state_gdn_S.shape[2:]
    Hm, mdk, mdv = state_mlstm_C.shape[2:]
    Hr, rdk, rdv = state_ret_S.shape[2:]
    nconv = state_gdn_conv.shape[2]
    steps = _Steps(bp, tp, bs, ts)
    np_tok = bp * tp

    widths = (2 * Hg * gdk + Hg * gdv, Hg * gdv, Hg, Hg,
              Hm * mdk, Hm * mdk, Hm * mdv, Hm * mdv, Hm, Hm,
              Hr * rdk, Hr * rdk, Hr * rdv, Hr * rdv, 3 * D)
    offs = [0]
    for w_ in widths:
        offs.append(offs[-1] + w_)
    o_qkv, _, o_b, o_a, o_mq, _, _, _, o_mi, o_mf, o_rq, _, _, _, _, o_end = offs
    assert o_end == w_in.shape[2]
    qkv_w = widths[0]
    wide = ((o_qkv, o_b), (o_mq, o_mi), (o_rq, o_end))
    ml_col0 = o_b - o_qkv
    ret_col0 = ml_col0 + (o_mi - o_mq)
    gate_col0 = ret_col0 + sum(widths[10:14])

    gdn_blk, gdn_lane = o_b // _LANE, o_b % _LANE
    ml_blk, ml_lane = o_mi // _LANE, o_mi % _LANE
    assert o_a == o_b + Hg and o_mf == o_mi + Hm

    def wide_cols(a):
        return jnp.concatenate([a[..., lo:hi] for lo, hi in wide], axis=-1)

    cos_tab, sin_tab = _rotary_tables(tp, ts, rdk)
    lg = jnp.log(1.0 - 2.0 ** (-5.0 - jnp.arange(Hr, dtype=_F32)))

    w_in_t = jnp.swapaxes(w_in, 1, 2)
    w_wide, b_wide = _regroup_weights(w_in_t, wide), wide_cols(b_in)
    w_br, w_o, w_dn = _bf(w_branch), _bf(w_out), _bf(w_ffn_down)
    la = gdn_lane + Hg
    gp = jnp.zeros((depth, 2, _LANE), _F32)
    gp = gp.at[:, 0, la:la + Hg].set(gdn_A_log).at[:, 1, la:la + Hg].set(gdn_dt_bias)
    lf = ml_lane + Hm
    fb = jnp.zeros((depth, 1, _LANE), _F32).at[:, 0, lf:lf + Hm].set(mlstm_f_bias)

    x = jnp.concatenate([x_prompt.reshape(np_tok, D), x_sample.reshape(bs * ts, D)], axis=0)
    xb = x.astype(_BF16)
    g_st = m_st = r_st = None
    conv_p, conv_s = [], []
    for l in range(depth):
        h = _matmul(xb, w_wide, l, bias=b_wide[l], name="in_proj")
        hs = _small_proj(x, w_in_t, b_in, l, (gdn_blk, ml_blk))
        oa, g_st = _gdn(h, hs, 0, gdn_lane, gdn_conv_w[l], gp, gdn_norm_g[l], state_gdn_S, state_gdn_conv, g_st, l, steps)
        tails = h.reshape(steps.n, _CHUNK, -1)[:, _CHUNK - nconv:, :qkv_w]
        conv_p.append(tails[steps.ncp - 1:steps.np_steps:steps.ncp])
        conv_s.append(tails[steps.np_steps + steps.ncs - 1::steps.ncs])
        ob, m_st = _mlstm(h, ml_col0, hs, 1, ml_lane, fb, mlstm_norm_g[l],
                          state_mlstm_C, state_mlstm_n, state_mlstm_m, m_st, l, steps)
        oc, r_st = _retention(h, ret_col0, lg, cos_tab, sin_tab, ret_norm_g[l], state_ret_S, r_st, l, steps)

        merged = _merge(oa, ob, oc, w_br, l, h, gate_col0)
        z1 = _matmul(merged, w_o, l, res=x, alpha=alpha, name="out_proj")
        x, xb = _layernorm(z1, ln1_g[l], ln1_b[l])

        act = _ffn_up(xb, w_ffn_up, l)
        z2 = _matmul(act, w_dn, l, res=x, alpha=alpha, bm=512, bn=512, name="ffn_down")
        x, xb = _layernorm(z2, ln2_g[l], ln2_b[l])

    gS_p, gS_s = g_st
    mC_p, mC_s, mn_p, mn_s, mm_p, mm_s = m_st
    rS_p, rS_s = r_st
    yp = x[:np_tok].reshape(bp, tp, D)
    ys = x[np_tok:].reshape(bs, ts, D)
    return (yp, ys,
            gS_p, jnp.stack(conv_p), mC_p, mn_p, mm_p[:, :, 0, :Hm], rS_p,
            gS_s, jnp.stack(conv_s), mC_s, mn_s, mm_s[:, :, 0, :Hm], rS_s)
```

```python
import functools

import jax
import jax.numpy as jnp
from jax import lax
from jax.experimental import pallas as pl
from jax.experimental.pallas import tpu as pltpu

_F32 = jnp.float32
_BF16 = jnp.bfloat16
_HI = lax.Precision.HIGHEST

_CHUNK = 64
_PAST_LEN = 1024
_ROPE_BASE = 10000.0
_LN_EPS = 1e-5
_LANE = 128
_SUBLANE = 8
_VMEM_LIMIT = 56 * 1024 * 1024


def _tile(dim, pref, align=_LANE):
    best = None
    t = align
    while t <= min(dim, pref):
        if dim % t == 0:
            best = t
        t += align
    return dim if best is None else best


def _params(*sem):
    return pltpu.CompilerParams(dimension_semantics=sem, vmem_limit_bytes=_VMEM_LIMIT)


def _bf(x):
    return x.astype(_BF16)


def _dot(a, b):
    return jnp.dot(a, b, preferred_element_type=_F32)


def _dot_nt(a, b):
    return lax.dot_general(a, b, (((1,), (1,)), ((), ())), preferred_element_type=_F32)


def _dot_tn(a, b):
    return lax.dot_general(a, b, (((0,), (0,)), ((), ())), preferred_element_type=_F32)


def _dot_hi(a, b):
    return jnp.dot(a, b, precision=_HI, preferred_element_type=_F32)


def _split(x):
    hi = x.astype(_BF16)
    return hi, (x - hi.astype(_F32)).astype(_BF16)


def _mm3(a, b):
    return _dot(a[0], b[0]) + (_dot(a[0], b[1]) + _dot(a[1], b[0]))


def _mm_kernel(*refs, nk, has_bias, has_res, alpha):
    it = iter(refs)
    x_ref, w_ref = next(it), next(it)
    b_ref = next(it) if has_bias else None
    r_ref = next(it) if has_res else None
    o_ref = next(it)
    acc_ref = next(it) if nk > 1 else None

    def finish(v):
        if has_bias:
            v = v + b_ref[...]
        if has_res:
            v = alpha * r_ref[...] + v
        o_ref[...] = v.astype(o_ref.dtype)

    part = jnp.dot(x_ref[...], w_ref[0], preferred_element_type=_F32)
    if nk == 1:
        finish(part)
    else:
        k = pl.program_id(2)

        @pl.when(k == 0)
        def _():
            acc_ref[...] = part

        @pl.when(k > 0)
        def _():
            acc_ref[...] += part

        @pl.when(k == nk - 1)
        def _():
            finish(acc_ref[...])


def _matmul(x, w, layer, *, bias=None, res=None, alpha=1.0, out_dtype=_F32, bm=1024, bn=1024, bk=None, name):
    M, K = x.shape
    N = w.shape[2]
    bm, bn = _tile(M, bm), _tile(N, bn)
    bk = K if bk is None else _tile(K, bk)
    nk = K // bk
    in_specs = [pl.BlockSpec((bm, bk), lambda i, j, k: (i, k)),
                pl.BlockSpec((1, bk, bn), lambda i, j, k: (layer, k, j))]
    args = [x, w]
    if bias is not None:
        in_specs.append(pl.BlockSpec((1, bn), lambda i, j, k: (0, j)))
        args.append(bias.reshape(1, N).astype(_F32))
    if res is not None:
        in_specs.append(pl.BlockSpec((bm, bn), lambda i, j, k: (i, j)))
        args.append(res)
    return pl.pallas_call(
        functools.partial(_mm_kernel, nk=nk, has_bias=bias is not None, has_res=res is not None, alpha=alpha),
        grid=(M // bm, N // bn, nk),
        in_specs=in_specs,
        out_specs=pl.BlockSpec((bm, bn), lambda i, j, k: (i, j)),
        out_shape=jax.ShapeDtypeStruct((M, N), out_dtype),
        scratch_shapes=[pltpu.VMEM((bm, bn), _F32)] if nk > 1 else [],
        compiler_params=_params("parallel", "parallel", "arbitrary"),
        name=name,
    )(*args)


_REGROUP_EXTRA = 64


def _regroup_kernel(a_ref, b_ref, o_ref, *, regions):
    j = pl.program_id(2)

    def emit(lo, hi, shift):
        @pl.when(jnp.logical_and(j >= lo, j < hi))
        def _():
            if shift == 0:
                t = a_ref[0]
            else:
                t = jnp.concatenate([a_ref[0, shift:, :], b_ref[0, :shift, :]], axis=0)
            o_ref[0] = _bf(t.T)

    for lo, hi, shift in regions:
        emit(lo, hi, shift)


def _regroup_weights(wt, spans):
    depth, N, K = wt.shape
    starts, shifts, pos = [], [], 0
    for lo, hi in spans:
        starts.append(pos)
        shifts.append(lo - pos)
        pos += hi - lo
    assert all(0 <= s <= _REGROUP_EXTRA and s % _SUBLANE == 0 for s in shifts)
    bn = _LANE
    for t in range(_LANE, 512 + 1, _LANE):
        if all(v % t == 0 for v in starts + [pos]):
            bn = t
    bk = _tile(K, 2048)
    last_extra_block = (N - 1) // _REGROUP_EXTRA
    bounds = [s // bn for s in starts] + [pos // bn]
    regions = tuple((bounds[r], bounds[r + 1], shifts[r]) for r in range(len(spans)))
    return pl.pallas_call(
        functools.partial(_regroup_kernel, regions=regions),
        grid=(depth, K // bk, pos // bn),
        in_specs=[pl.BlockSpec((1, bn, bk), lambda l, i, j: (l, j, i)),
                  pl.BlockSpec((1, _REGROUP_EXTRA, bk),
                               lambda l, i, j: (l, jnp.minimum((j + 1) * (bn // _REGROUP_EXTRA), last_extra_block), i))],
        out_specs=pl.BlockSpec((1, bk, bn), lambda l, i, j: (l, i, j)),
        out_shape=jax.ShapeDtypeStruct((depth, K, pos), _BF16),
        compiler_params=_params("parallel", "parallel", "parallel"),
        name="regroup_w_in",
    )(wt, wt)


def _small_proj_kernel(x_ref, *refs, nblk):
    xs = _split(x_ref[...])
    w_refs, b_refs, o_ref = refs[:nblk], refs[nblk:2 * nblk], refs[2 * nblk]
    for n in range(nblk):
        wh, wl = _split(w_refs[n][0])
        acc = _dot_nt(xs[0], wh) + (_dot_nt(xs[0], wl) + _dot_nt(xs[1], wh))
        o_ref[:, n * _LANE:(n + 1) * _LANE] = acc + b_refs[n][0]


def _small_proj(x, wt, b, layer, row_blocks):
    M, K = x.shape
    depth, N = b.shape
    bm = _tile(M, 256)
    nblk = len(row_blocks)
    wspec = lambda c: pl.BlockSpec((1, _LANE, K), lambda i: (layer, c, 0))
    bspec = lambda c: pl.BlockSpec((1, 1, _LANE), lambda i: (layer, 0, c))
    return pl.pallas_call(
        functools.partial(_small_proj_kernel, nblk=nblk),
        grid=(M // bm,),
        in_specs=[pl.BlockSpec((bm, K), lambda i: (i, 0))] + [wspec(c) for c in row_blocks] + [bspec(c) for c in row_blocks],
        out_specs=pl.BlockSpec((bm, nblk * _LANE), lambda i: (i, 0)),
        out_shape=jax.ShapeDtypeStruct((M, nblk * _LANE), _F32),
        compiler_params=_params("parallel"),
        name="small_proj",
    )(x, *([wt] * nblk), *([b.reshape(depth, 1, N)] * nblk))


def _ln_kernel(z_ref, g_ref, b_ref, o_ref, ob_ref):
    z = z_ref[...]
    mu = jnp.mean(z, axis=-1, keepdims=True)
    d = z - mu
    var = jnp.mean(d * d, axis=-1, keepdims=True)
    y = d * lax.rsqrt(var + _LN_EPS) * g_ref[...] + b_ref[...]
    o_ref[...] = y
    ob_ref[...] = y.astype(_BF16)


def _layernorm(z, g, b):
    M, D = z.shape
    bm = _tile(M, 256)
    return pl.pallas_call(
        _ln_kernel,
        grid=(M // bm,),
        in_specs=[pl.BlockSpec((bm, D), lambda i: (i, 0)),
                  pl.BlockSpec((1, D), lambda i: (0, 0)),
                  pl.BlockSpec((1, D), lambda i: (0, 0))],
        out_specs=[pl.BlockSpec((bm, D), lambda i: (i, 0)),
                   pl.BlockSpec((bm, D), lambda i: (i, 0))],
        out_shape=[jax.ShapeDtypeStruct((M, D), _F32), jax.ShapeDtypeStruct((M, D), _BF16)],
        compiler_params=_params("parallel"),
        name="layernorm",
    )(z, g.reshape(1, D), b.reshape(1, D))


def _merge_kernel(a_ref, b_ref, c_ref, w_ref, g_ref, o_ref, acc_ref):
    br = pl.program_id(2)
    gate = jax.nn.sigmoid(g_ref[...])

    def contrib(x_ref):
        return gate * jnp.dot(x_ref[...], w_ref[0, 0], preferred_element_type=_F32)

    @pl.when(br == 0)
    def _():
        acc_ref[...] = contrib(a_ref)

    @pl.when(br == 1)
    def _():
        acc_ref[...] += contrib(b_ref)

    @pl.when(br == 2)
    def _():
        o_ref[...] = (acc_ref[...] + contrib(c_ref)).astype(o_ref.dtype)


def _merge(oa, ob, oc, w_branch, layer, h, gate_col0):
    M, BW = oa.shape
    D = w_branch.shape[3]
    bm, bn = _tile(M, 1024), _tile(D, 1024)
    nj = D // bn
    assert gate_col0 % bn == 0
    g0 = gate_col0 // bn
    xspec = pl.BlockSpec((bm, BW), lambda i, j, n: (i, 0))
    return pl.pallas_call(
        _merge_kernel,
        grid=(M // bm, nj, 3),
        in_specs=[xspec, xspec, xspec,
                  pl.BlockSpec((1, 1, BW, bn), lambda i, j, n: (layer, n, 0, j)),
                  pl.BlockSpec((bm, bn), lambda i, j, n: (i, g0 + n * nj + j))],
        out_specs=pl.BlockSpec((bm, bn), lambda i, j, n: (i, j)),
        out_shape=jax.ShapeDtypeStruct((M, D), _BF16),
        scratch_shapes=[pltpu.VMEM((bm, bn), _F32)],
        compiler_params=_params("parallel", "parallel", "arbitrary"),
        name="branch_merge",
    )(oa, ob, oc, w_branch, h)


def _ffn_up_kernel(x_ref, wg_ref, wv_ref, o_ref):
    x = x_ref[...]
    g = jnp.dot(x, _bf(wg_ref[0]), preferred_element_type=_F32)
    v = jnp.dot(x, _bf(wv_ref[0]), preferred_element_type=_F32)
    o_ref[...] = (g * jax.nn.sigmoid(g) * v).astype(o_ref.dtype)


def _ffn_up(x, w_up, layer):
    M, K = x.shape
    F = w_up.shape[2] // 2
    bm, bn = _tile(M, 1024), _tile(F, 256)
    nj = F // bn
    return pl.pallas_call(
        _ffn_up_kernel,
        grid=(M // bm, nj),
        in_specs=[pl.BlockSpec((bm, K), lambda i, j: (i, 0)),
                  pl.BlockSpec((1, K, bn), lambda i, j: (layer, 0, j)),
                  pl.BlockSpec((1, K, bn), lambda i, j: (layer, 0, nj + j))],
        out_specs=pl.BlockSpec((bm, bn), lambda i, j: (i, j)),
        out_shape=jax.ShapeDtypeStruct((M, F), _BF16),
        compiler_params=_params("parallel", "parallel"),
        name="ffn_up",
    )(x, w_up, w_up)


def _transpose_hi(x):
    n = x.shape[1]
    eye = (lax.broadcasted_iota(jnp.int32, (n, n), 0) == lax.broadcasted_iota(jnp.int32, (n, n), 1)).astype(_F32)
    return lax.dot_general(eye, x, (((1,), (1,)), ((), ())), precision=_HI, preferred_element_type=_F32)


def _silu(x):
    return x * jax.nn.sigmoid(x)


def _softplus(x):
    return jnp.maximum(x, 0.0) + jnp.log1p(jnp.exp(-jnp.abs(x)))


def _chunk_masks():
    L = _CHUNK
    row = lax.broadcasted_iota(jnp.int32, (L, L), 0)
    col = lax.broadcasted_iota(jnp.int32, (L, L), 1)
    return row, col


def _ln_head(o, g):
    mu = jnp.mean(o, axis=-1, keepdims=True)
    d = o - mu
    var = jnp.mean(d * d, axis=-1, keepdims=True)
    return d * lax.rsqrt(var + _LN_EPS) * g


def _unit_lower_inverses(As, row, col, left):
    def bdiag(ys):
        return tuple(jnp.concatenate([jnp.where(left, y, jnp.zeros_like(y)),
                                      jnp.where(left, jnp.zeros_like(y), y)], axis=0) for y in ys)

    same16 = (row // 16) == (col // 16)
    same32 = (row // 32) == (col // 32)
    eye = (row == col).astype(_F32)
    D = [jnp.where(same16, A, 0.0) for A in As]
    Ds = [_split(d) for d in D]
    D2s = [_split(_mm3(d, bdiag(d))) for d in Ds]
    D2b = [bdiag(d2) for d2 in D2s]
    P = [eye - d for d in D]
    P = [p + _mm3(_split(p), d2) for p, d2 in zip(P, D2b)]
    D4s = [_split(_mm3(d2, d2b)) for d2, d2b in zip(D2s, D2b)]
    D4b = [bdiag(d4) for d4 in D4s]
    P = [p + _mm3(_split(p), d4) for p, d4 in zip(P, D4b)]
    D8b = [bdiag(_split(_mm3(d4, d4b))) for d4, d4b in zip(D4s, D4b)]
    P = [p + _mm3(_split(p), d8) for p, d8 in zip(P, D8b)]
    for off_mask in (jnp.logical_and(same32, jnp.logical_not(same16)), jnp.logical_not(same32)):
        Es = [_split(jnp.where(off_mask, A, 0.0)) for A in As]
        Ps = [_split(p) for p in P]
        Qb = [bdiag(_split(_mm3(e, bdiag(ps)))) for e, ps in zip(Es, Ps)]
        P = [p - _mm3(ps, q) for p, ps, q in zip(P, Ps, Qb)]
    return P


class _Steps:
    def __init__(self, bp, tp, bs, ts):
        assert tp % _CHUNK == 0 and ts % _CHUNK == 0
        self.bp, self.bs = bp, bs
        self.ncp, self.ncs = tp // _CHUNK, ts // _CHUNK
        self.np_steps = bp * self.ncp
        self.n = self.np_steps + bs * self.ncs

    def in_prompt(self, i):
        return i < self.np_steps

    def _sample_step(self, i):
        return jnp.maximum(i - self.np_steps, 0)

    def seq(self, i):
        return jnp.where(self.in_prompt(i), i // self.ncp, self.bp + self._sample_step(i) // self.ncs)

    def sample_seq(self, i):
        return self._sample_step(i) // self.ncs

    def prompt_seq(self, i):
        return jnp.minimum(i // self.ncp, self.bp - 1)

    def chunk(self, i):
        return jnp.where(self.in_prompt(i), i % self.ncp, self._sample_step(i) % self.ncs)

    def pos_block(self, i):
        return jnp.where(self.in_prompt(i), self.chunk(i), self.ncp + self.chunk(i))

    def first_last(self, i):
        c = self.chunk(i)
        return c == 0, jnp.where(self.in_prompt(i), c == self.ncp - 1, c == self.ncs - 1)


def _gdn_kernel(q_ref, k_ref, v_ref, z_ref, gs_ref, cwq_ref, cwk_ref, cwv_ref, gp_ref, ng_ref,
                s0_ref, c0q_ref, c0k_ref, c0v_ref, *rest, hpg, dk, dv, lane0, steps):
    o_ref, sp_ref, ss_ref, s_scr, bq, bk, bv = rest[-7:]
    L, P = _CHUNK, _SUBLANE
    i = pl.program_id(1)
    first, last = steps.first_last(i)
    in_p = steps.in_prompt(i)
    nc = c0q_ref.shape[2]

    @pl.when(jnp.logical_and(first, in_p))
    def _():
        s_scr[...] = jnp.zeros(s_scr.shape, _F32)
        for buf in (bq, bk, bv):
            buf[0:P, :] = jnp.zeros((P, buf.shape[1]), _F32)

    @pl.when(jnp.logical_and(first, jnp.logical_not(in_p)))
    def _():
        s_scr[...] = s0_ref[0, 0]
        bq[P - nc:P, :] = c0q_ref[0, 0]
        bk[P - nc:P, :] = c0k_ref[0, 0]
        bv[P - nc:P, :] = c0v_ref[0, 0]

    bq[P:P + L, :] = q_ref[...]
    bk[P:P + L, :] = k_ref[...]
    bv[P:P + L, :] = v_ref[...]

    row, col = _chunk_masks()
    tril = (row >= col).astype(_F32)
    gs = gs_ref[...]
    gp = gp_ref[0]
    beta_blk = jax.nn.sigmoid(gs)
    g_blk = -jnp.exp(gp[0:1, :]) * _softplus(gs + gp[1:2, :])
    G_blk = _dot_hi(tril, g_blk)
    GT2 = _transpose_hi(jnp.concatenate([G_blk, G_blk], axis=0))
    ng = ng_ref[...]
    heads = range(hpg)

    def conv(buf, cw_ref, cs):
        out = buf[P - nc:P - nc + L, cs] * cw_ref[0:1, cs]
        for j in range(1, nc + 1):
            out = out + buf[P - nc + j:P - nc + j + L, cs] * cw_ref[j:j + 1, cs]
        return out

    def l2norm(x):
        return x * lax.rsqrt(jnp.sum(x * x, axis=-1, keepdims=True) + 1e-6)

    ks = [slice(h * dk, (h + 1) * dk) for h in heads]
    vs = [slice(h * dv, (h + 1) * dv) for h in heads]
    q = [l2norm(_silu(conv(bq, cwq_ref, ks[h]))) * dk ** -0.5 for h in heads]
    k = [l2norm(_silu(conv(bk, cwk_ref, ks[h]))) for h in heads]
    v = [_silu(conv(bv, cwv_ref, vs[h])) for h in heads]
    la = lane0 + hpg
    beta = [beta_blk[:, lane0 + h:lane0 + h + 1] for h in heads]
    Gc = [G_blk[:, la + h:la + h + 1] for h in heads]
    eG = [jnp.exp(Gc[h]) for h in heads]
    kb = [_bf(k[h]) for h in heads]
    qb = [_bf(q[h]) for h in heads]

    pairs = range(hpg // 2)
    lane = lax.broadcasted_iota(jnp.int32, (L, 2 * L), 1)
    left = lane < L
    rowp = lax.broadcasted_iota(jnp.int32, (L, 2 * L), 0)
    colp = jnp.where(left, lane, lane - L)
    inclp, strictp = rowp >= colp, rowp > colp

    def side(a0, a1):
        return jnp.where(left, a0, a1)

    def bdiag(y0, y1):
        return jnp.concatenate([jnp.concatenate([y0, jnp.zeros_like(y1)], axis=1),
                                jnp.concatenate([jnp.zeros_like(y0), y1], axis=1)], axis=0)

    Gcp = [side(Gc[2 * p], Gc[2 * p + 1]) for p in pairs]
    Grp = [side(GT2[la + 2 * p:la + 2 * p + 1, :], GT2[la + 2 * p + 1:la + 2 * p + 2, :]) for p in pairs]
    betap = [side(beta[2 * p], beta[2 * p + 1]) for p in pairs]
    gam = [jnp.where(inclp, jnp.exp(jnp.where(inclp, Gcp[p] - Grp[p], 0.0)), 0.0) for p in pairs]
    kq = [_dot_nt(jnp.concatenate([jnp.concatenate([kb[2 * p], kb[2 * p + 1]], axis=1),
                                   jnp.concatenate([qb[2 * p], qb[2 * p + 1]], axis=1)], axis=0),
                  bdiag(kb[2 * p], kb[2 * p + 1])) for p in pairs]
    A = [jnp.where(strictp, betap[p] * kq[p][0:L] * gam[p], 0.0) for p in pairs]
    qk = [_bf(kq[p][L:2 * L] * gam[p]) for p in pairs]
    T = _unit_lower_inverses(A, rowp, colp, left)
    rhs = [_split(jnp.concatenate([beta[h] * v[h], (beta[h] * eG[h]) * k[h]], axis=1)) for h in heads]
    UWp = [_mm3(_split(T[p]), tuple(bdiag(rhs[2 * p][t], rhs[2 * p + 1][t]) for t in range(2))) for p in pairs]
    UW = [UWp[h // 2][:, (h % 2) * (dv + dk):(h % 2 + 1) * (dv + dk)] for h in heads]
    G_last = [Gc[h][L - 1:L, :] for h in heads]
    S = [s_scr[h] for h in heads]
    Sb = [_bf(S[h]) for h in heads]
    WQ = [_dot(jnp.concatenate([_bf(UW[h][:, dv:]), _bf(q[h] * eG[h])], axis=0), Sb[h]) for h in heads]
    db = [_bf(UW[h][:, 0:dv] - WQ[h][0:L]) for h in heads]
    qd = [_dot(qk[p], bdiag(db[2 * p], db[2 * p + 1])) for p in pairs]
    o = [WQ[h][L:2 * L] + qd[h // 2][:, (h % 2) * dv:(h % 2 + 1) * dv] for h in heads]
    for h in heads:
        k_dec = k[h] * jnp.exp(G_last[h] - Gc[h])
        s_scr[h] = jnp.exp(G_last[h]) * S[h] + _dot_tn(_bf(k_dec), db[h])
    for h in heads:
        on = o[h] * lax.rsqrt(jnp.mean(o[h] * o[h], axis=-1, keepdims=True) + 1e-6) * ng
        o_ref[:, vs[h]] = (on * _silu(z_ref[:, vs[h]])).astype(o_ref.dtype)

    bq[0:P, :] = bq[L:L + P, :]
    bk[0:P, :] = bk[L:L + P, :]
    bv[0:P, :] = bv[L:L + P, :]

    @pl.when(jnp.logical_and(last, in_p))
    def _():
        sp_ref[0, 0] = s_scr[...]

    @pl.when(jnp.logical_and(last, jnp.logical_not(in_p)))
    def _():
        ss_ref[0, 0] = s_scr[...]


def _state_outputs(prev, shapes_p, shapes_s, n_in, n_lead_out):
    shapes = [jax.ShapeDtypeStruct(s, _F32) for pair in zip(shapes_p, shapes_s) for s in pair]
    if prev is None:
        return shapes, [], {}, ()
    specs = [pl.BlockSpec(memory_space=pl.ANY)] * len(prev)
    aliases = {n_in + j: n_lead_out + j for j in range(len(prev))}
    return shapes, specs, aliases, tuple(prev)


def _gdn(h, hs, gate_block, lane0, conv_w, gp, norm_g, s_in, conv_in, prev, layer, steps):
    ntok = h.shape[0]
    depth, _, H, dk, dv = s_in.shape
    nc = conv_in.shape[2]
    assert dk == dv and nc < _SUBLANE and conv_w.shape[0] == nc + 1 and lane0 + 2 * H <= _LANE and H % 2 == 0
    hpg, ng_ = H, 1
    L, P = _CHUNK, _SUBLANE
    wq = hpg * dk
    sseq, pseq = steps.sample_seq, steps.prompt_seq
    tok = lambda off: pl.BlockSpec((L, wq), lambda g, i: (i, off + g))
    cw = lambda off: pl.BlockSpec((nc + 1, wq), lambda g, i: (0, off + g))
    c0 = lambda off: pl.BlockSpec((1, 1, nc, wq), lambda g, i: (layer, sseq(i), 0, off + g))
    in_specs = [tok(0), tok(ng_), tok(2 * ng_), tok(3 * ng_),
                pl.BlockSpec((L, _LANE), lambda g, i: (i, gate_block)),
                cw(0), cw(ng_), cw(2 * ng_),
                pl.BlockSpec((1, 2, _LANE), lambda g, i: (layer, 0, 0)),
                pl.BlockSpec((1, dv), lambda g, i: (0, 0)),
                pl.BlockSpec((1, 1, hpg, dk, dv), lambda g, i: (layer, sseq(i), g, 0, 0)),
                c0(0), c0(ng_), c0(2 * ng_)]
    shapes, pspecs, aliases, pargs = _state_outputs(
        prev, [(depth, steps.bp, H, dk, dv)], [(depth, steps.bs, H, dk, dv)], len(in_specs), 1)
    out = pl.pallas_call(
        functools.partial(_gdn_kernel, hpg=hpg, dk=dk, dv=dv, lane0=lane0, steps=steps),
        grid=(ng_, steps.n),
        in_specs=in_specs + pspecs,
        out_specs=[pl.BlockSpec((L, hpg * dv), lambda g, i: (i, g)),
                   pl.BlockSpec((1, 1, hpg, dk, dv), lambda g, i: (layer, pseq(i), g, 0, 0)),
                   pl.BlockSpec((1, 1, hpg, dk, dv), lambda g, i: (layer, sseq(i), g, 0, 0))],
        out_shape=[jax.ShapeDtypeStruct((ntok, H * dv), _BF16)] + shapes,
        input_output_aliases=aliases,
        scratch_shapes=[pltpu.VMEM((hpg, dk, dv), _F32)] + [pltpu.VMEM((P + L, wq), _F32)] * 3,
        compiler_params=_params("parallel", "arbitrary"),
        name="gdn_mixer",
    )(h, h, h, h, hs, conv_w, conv_w, conv_w, gp, norm_g.reshape(1, dv), s_in, conv_in, conv_in, conv_in, *pargs)
    return out[0], out[1:]


def _mlstm_kernel(q_ref, k_ref, v_ref, og_ref, gs_ref, fb_ref, ng_ref, c0_ref, n0_ref, m0_ref,
                  *rest, H, dk, dv, lane0, steps):
    o_ref, cp_ref, cs_ref, np_ref, ns_ref, mp_ref, ms_ref, c_scr, n_scr, m_scr = rest[-10:]
    L, P = _CHUNK, _SUBLANE
    i = pl.program_id(0)
    first, last = steps.first_last(i)
    in_p = steps.in_prompt(i)

    @pl.when(jnp.logical_and(first, in_p))
    def _():
        c_scr[...] = jnp.zeros(c_scr.shape, _F32)
        n_scr[...] = jnp.zeros(n_scr.shape, _F32)
        m_scr[...] = jnp.zeros(m_scr.shape, _F32)

    @pl.when(jnp.logical_and(first, jnp.logical_not(in_p)))
    def _():
        c_scr[...] = c0_ref[0, 0]
        n_scr[...] = n0_ref[0, 0]
        m0 = m0_ref[0, 0]
        for h in range(H):
            m_scr[h] = jnp.broadcast_to(m0[:, h:h + 1], (P, _LANE))

    row, col = _chunk_masks()
    incl = row >= col
    tril = incl.astype(_F32)
    gs = gs_ref[...]
    li, lf = lane0, lane0 + H
    lf_blk = -_softplus(-(gs + fb_ref[0]))
    F_blk = _dot_hi(tril, lf_blk)
    FT = _transpose_hi(F_blk)
    gsT = _transpose_hi(gs)

    heads = range(H)
    ks = [slice(h * dk, (h + 1) * dk) for h in heads]
    vs = [slice(h * dv, (h + 1) * dv) for h in heads]
    q = [q_ref[:, ks[h]] * dk ** -0.5 for h in heads]
    k = [k_ref[:, ks[h]] for h in heads]
    qb = [_bf(q[h]) for h in heads]
    kb = [_bf(k[h]) for h in heads]
    vb = [_bf(v_ref[:, vs[h]]) for h in heads]
    qk = [_dot_nt(qb[h], kb[h]) for h in heads]
    C = [c_scr[h] for h in heads]
    qC = [_dot(qb[h], _bf(C[h])) for h in heads]
    igc = [gs[:, li + h:li + h + 1] for h in heads]
    Fc = [F_blk[:, lf + h:lf + h + 1] for h in heads]
    logD = [jnp.where(incl, Fc[h] - FT[lf + h:lf + h + 1, :] + gsT[li + h:li + h + 1, :], -jnp.inf) for h in heads]
    m_prev = [m_scr[h][0:1, 0:1] for h in heads]
    m_t = [jnp.maximum(Fc[h] + m_prev[h], jnp.max(logD[h], axis=-1, keepdims=True)) for h in heads]
    Sm = [jnp.exp(logD[h] - m_t[h]) * qk[h] for h in heads]
    Sv = [_dot(_bf(Sm[h]), vb[h]) for h in heads]
    inter = [jnp.exp(Fc[h] + m_prev[h] - m_t[h]) for h in heads]
    n = [n_scr[h:h + 1, :] for h in heads]
    m_last = [m_t[h][L - 1:L, :] for h in heads]
    wk = [jnp.exp(Fc[h][L - 1:L, :] - Fc[h] + igc[h] - m_last[h]) * k[h] for h in heads]
    kv = [_dot_tn(_bf(wk[h]), vb[h]) for h in heads]
    for h in heads:
        dec = inter[h][L - 1:L, :]
        c_scr[h] = dec * C[h] + kv[h]
        n_scr[h:h + 1, :] = dec * n[h] + jnp.sum(wk[h], axis=0, keepdims=True)
        m_scr[h] = jnp.broadcast_to(m_last[h], (P, _LANE))
    for h in heads:
        num = inter[h] * qC[h] + Sv[h]
        den = inter[h] * jnp.sum(q[h] * n[h], axis=-1, keepdims=True) + jnp.sum(Sm[h], axis=-1, keepdims=True)
        hh = num / jnp.maximum(jnp.abs(den), jnp.exp(-m_t[h]))
        o_ref[:, vs[h]] = (_ln_head(hh, ng_ref[:, vs[h]]) * jax.nn.sigmoid(og_ref[:, vs[h]])).astype(o_ref.dtype)

    def write_state(c_ref, n_ref, m_ref):
        c_ref[0, 0] = c_scr[...]
        n_ref[0, 0] = n_scr[...]
        lane = lax.broadcasted_iota(jnp.int32, (1, _LANE), 1)
        m_row = jnp.zeros((1, _LANE), _F32)
        for h in range(H):
            m_row = jnp.where(lane == h, m_scr[h][0:1, :], m_row)
        m_ref[0, 0] = m_row

    @pl.when(jnp.logical_and(last, in_p))
    def _():
        write_state(cp_ref, np_ref, mp_ref)

    @pl.when(jnp.logical_and(last, jnp.logical_not(in_p)))
    def _():
        write_state(cs_ref, ns_ref, ms_ref)


def _mlstm(h, col0, hs, gate_block, lane0, f_bias, norm_g, c_in, n_in, m_in, prev, layer, steps):
    ntok = h.shape[0]
    depth, _, H, dk, dv = c_in.shape
    assert dv % dk == 0 and 2 % (dv // dk) == 0 and lane0 + 2 * H <= _LANE and col0 % (H * dv) == 0
    L = _CHUNK
    bp, bs = steps.bp, steps.bs
    jq, jv = col0 // (H * dk), col0 // (H * dv) + 2 * dk // dv
    tq = lambda j: pl.BlockSpec((L, H * dk), lambda i: (i, jq + j))
    tv = lambda j: pl.BlockSpec((L, H * dv), lambda i: (i, jv + j))
    sseq, pseq = steps.sample_seq, steps.prompt_seq
    in_specs = [tq(0), tq(1), tv(0), tv(1),
                pl.BlockSpec((L, _LANE), lambda i: (i, gate_block)),
                pl.BlockSpec((1, 1, _LANE), lambda i: (layer, 0, 0)),
                pl.BlockSpec((1, H * dv), lambda i: (0, 0)),
                pl.BlockSpec((1, 1, H, dk, dv), lambda i: (layer, sseq(i), 0, 0, 0)),
                pl.BlockSpec((1, 1, H, dk), lambda i: (layer, sseq(i), 0, 0)),
                pl.BlockSpec((1, 1, 1, H), lambda i: (layer, sseq(i), 0, 0))]
    tails = [(H, dk, dv), (H, dk), (1, _LANE)]
    shapes, pspecs, aliases, pargs = _state_outputs(
        prev, [(depth, bp) + t for t in tails], [(depth, bs) + t for t in tails], len(in_specs), 1)
    state_specs = []
    for t in tails:
        zeros = (0,) * len(t)
        state_specs.append(pl.BlockSpec((1, 1) + t, lambda i, z=zeros: (layer, pseq(i)) + z))
        state_specs.append(pl.BlockSpec((1, 1) + t, lambda i, z=zeros: (layer, sseq(i)) + z))
    out = pl.pallas_call(
        functools.partial(_mlstm_kernel, H=H, dk=dk, dv=dv, lane0=lane0, steps=steps),
        grid=(steps.n,),
        in_specs=in_specs + pspecs,
        out_specs=[pl.BlockSpec((L, H * dv), lambda i: (i, 0))] + state_specs,
        out_shape=[jax.ShapeDtypeStruct((ntok, H * dv), _BF16)] + shapes,
        input_output_aliases=aliases,
        scratch_shapes=[pltpu.VMEM((H, dk, dv), _F32), pltpu.VMEM((H, dk), _F32),
                        pltpu.VMEM((H, _SUBLANE, _LANE), _F32)],
        compiler_params=_params("arbitrary"),
        name="mlstm_mixer",
    )(h, h, h, h, hs, f_bias, norm_g.reshape(1, H * dv), c_in, n_in,
      m_in.reshape(m_in.shape[0], m_in.shape[1], 1, H), *pargs)
    return out[0], out[1:]


def _ret_kernel(lg_ref, q_ref, k_ref, v_ref, og_ref, cos_ref, sin_ref, ng_ref, s0_ref,
                *rest, H, dk, dv, steps):
    o_ref, sp_ref, ss_ref, s_scr = rest[-4:]
    L = _CHUNK
    i = pl.program_id(0)
    first, last = steps.first_last(i)
    in_p = steps.in_prompt(i)

    @pl.when(jnp.logical_and(first, in_p))
    def _():
        s_scr[...] = jnp.zeros(s_scr.shape, _F32)

    @pl.when(jnp.logical_and(first, jnp.logical_not(in_p)))
    def _():
        s_scr[...] = s0_ref[0, 0]

    row, col = _chunk_masks()
    incl = row >= col
    tdiff = (row - col).astype(_F32)
    jc = lax.broadcasted_iota(jnp.int32, (L, 1), 0).astype(_F32)
    cos, sin = cos_ref[...], sin_ref[...]

    def rotary(x):
        return x * cos + pltpu.roll(x, dk // 2, 1) * sin

    heads = range(H)
    ks = [slice(h * dk, (h + 1) * dk) for h in heads]
    vs = [slice(h * dv, (h + 1) * dv) for h in heads]
    lg = [lg_ref[h] for h in heads]
    q = [rotary(q_ref[:, ks[h]]) for h in heads]
    k = [rotary(k_ref[:, ks[h]]) * dk ** -0.5 for h in heads]
    vb = [_bf(v_ref[:, vs[h]]) for h in heads]
    qk = [_dot_nt(_bf(q[h]), _bf(k[h])) for h in heads]
    S = [s_scr[h] for h in heads]
    qS = [_dot(_bf(q[h] * jnp.exp(lg[h] * (jc + 1.0))), _bf(S[h])) for h in heads]
    kv = [_dot_tn(_bf(k[h] * jnp.exp(lg[h] * (L - 1.0 - jc))), vb[h]) for h in heads]
    D = [jnp.where(incl, jnp.exp(jnp.where(incl, lg[h] * tdiff, 0.0)), 0.0) for h in heads]
    o = [_dot(_bf(qk[h] * D[h]), vb[h]) + qS[h] for h in heads]
    for h in heads:
        s_scr[h] = jnp.exp(lg[h] * jnp.full((1, 1), float(L), _F32)) * S[h] + kv[h]
    for h in heads:
        o_ref[:, vs[h]] = (_ln_head(o[h], ng_ref[:, vs[h]]) * _silu(og_ref[:, vs[h]])).astype(o_ref.dtype)

    @pl.when(jnp.logical_and(last, in_p))
    def _():
        sp_ref[0, 0] = s_scr[...]

    @pl.when(jnp.logical_and(last, jnp.logical_not(in_p)))
    def _():
        ss_ref[0, 0] = s_scr[...]


def _retention(h, col0, lg, cos_tab, sin_tab, norm_g, s_in, prev, layer, steps):
    ntok = h.shape[0]
    depth, _, H, dk, dv = s_in.shape
    assert dv % dk == 0 and 2 % (dv // dk) == 0 and col0 % (H * dv) == 0
    L = _CHUNK
    jq, jv = col0 // (H * dk), col0 // (H * dv) + 2 * dk // dv
    tq = lambda j: pl.BlockSpec((L, H * dk), lambda i: (i, jq + j))
    tv = lambda j: pl.BlockSpec((L, H * dv), lambda i: (i, jv + j))
    tab = pl.BlockSpec((L, dk), lambda i: (steps.pos_block(i), 0))
    sseq, pseq = steps.sample_seq, steps.prompt_seq
    in_specs = [pl.BlockSpec(memory_space=pltpu.SMEM),
                tq(0), tq(1), tv(0), tv(1), tab, tab,
                pl.BlockSpec((1, H * dv), lambda i: (0, 0)),
                pl.BlockSpec((1, 1, H, dk, dv), lambda i: (layer, sseq(i), 0, 0, 0))]
    shapes, pspecs, aliases, pargs = _state_outputs(
        prev, [(depth, steps.bp, H, dk, dv)], [(depth, steps.bs, H, dk, dv)], len(in_specs), 1)
    out = pl.pallas_call(
        functools.partial(_ret_kernel, H=H, dk=dk, dv=dv, steps=steps),
        grid=(steps.n,),
        in_specs=in_specs + pspecs,
        out_specs=[pl.BlockSpec((L, H * dv), lambda i: (i, 0)),
                   pl.BlockSpec((1, 1, H, dk, dv), lambda i: (layer, pseq(i), 0, 0, 0)),
                   pl.BlockSpec((1, 1, H, dk, dv), lambda i: (layer, sseq(i), 0, 0, 0))],
        out_shape=[jax.ShapeDtypeStruct((ntok, H * dv), _BF16)] + shapes,
        input_output_aliases=aliases,
        scratch_shapes=[pltpu.VMEM((H, dk, dv), _F32)],
        compiler_params=_params("arbitrary"),
        name="retention_mixer",
    )(lg, h, h, h, h, cos_tab, sin_tab, norm_g.reshape(1, H * dv), s_in, *pargs)
    return out[0], out[1:]


def _rotary_tables(tp, ts, dk):
    half = dk // 2
    freq = _ROPE_BASE ** (-jnp.arange(half, dtype=_F32) / half)
    pos = jnp.concatenate([0.0 + jnp.arange(tp, dtype=_F32), float(_PAST_LEN) + jnp.arange(ts, dtype=_F32)])
    ang = pos[:, None] * freq[None, :]
    cos, sin = jnp.cos(ang), jnp.sin(ang)
    return jnp.concatenate([cos, cos], axis=1), jnp.concatenate([-sin, sin], axis=1)


def kernel(x_prompt, x_sample, state_gdn_S, state_gdn_conv, state_mlstm_C, state_mlstm_n, state_mlstm_m, state_ret_S, w_in, b_in, gdn_conv_w, gdn_A_log, gdn_dt_bias, gdn_norm_g, mlstm_f_bias, mlstm_norm_g, ret_norm_g, w_branch, w_out, ln1_g, ln1_b, ln2_g, ln2_b, w_ffn_up, w_ffn_down):
    bp, tp, D = x_prompt.shape
    bs, ts, _ = x_sample.shape
    depth = w_in.shape[0]
    alpha = (2 * depth) ** 0.25
    Hg, gdk, gdv = state_gdn_S.shape[2:]
    Hm, mdk, mdv = state_mlstm_C.shape[2:]
    Hr, rdk, rdv = state_ret_S.shape[2:]
    nconv = state_gdn_conv.shape[2]
    steps = _Steps(bp, tp, bs, ts)
    np_tok = bp * tp

    widths = (2 * Hg * gdk + Hg * gdv, Hg * gdv, Hg, Hg,
              Hm * mdk, Hm * mdk, Hm * mdv, Hm * mdv, Hm, Hm,
              Hr * rdk, Hr * rdk, Hr * rdv, Hr * rdv, 3 * D)
    offs = [0]
    for w_ in widths:
        offs.append(offs[-1] + w_)
    o_qkv, _, o_b, o_a, o_mq, _, _, _, o_mi, o_mf, o_rq, _, _, _, _, o_end = offs
    assert o_end == w_in.shape[2]
    qkv_w = widths[0]
    wide = ((o_qkv, o_b), (o_mq, o_mi), (o_rq, o_end))
    ml_col0 = o_b - o_qkv
    ret_col0 = ml_col0 + (o_mi - o_mq)
    gate_col0 = ret_col0 + sum(widths[10:14])

    gdn_blk, gdn_lane = o_b // _LANE, o_b % _LANE
    ml_blk, ml_lane = o_mi // _LANE, o_mi % _LANE
    assert o_a == o_b + Hg and o_mf == o_mi + Hm

    def wide_cols(a):
        return jnp.concatenate([a[..., lo:hi] for lo, hi in wide], axis=-1)

    cos_tab, sin_tab = _rotary_tables(tp, ts, rdk)
    lg = jnp.log(1.0 - 2.0 ** (-5.0 - jnp.arange(Hr, dtype=_F32)))

    w_in_t = jnp.swapaxes(w_in, 1, 2)
    w_wide, b_wide = _regroup_weights(w_in_t, wide), wide_cols(b_in)
    w_br, w_o, w_dn = _bf(w_branch), _bf(w_out), _bf(w_ffn_down)
    la = gdn_lane + Hg
    gp = jnp.zeros((depth, 2, _LANE), _F32)
    gp = gp.at[:, 0, la:la + Hg].set(gdn_A_log).at[:, 1, la:la + Hg].set(gdn_dt_bias)
    lf = ml_lane + Hm
    fb = jnp.zeros((depth, 1, _LANE), _F32).at[:, 0, lf:lf + Hm].set(mlstm_f_bias)

    x = jnp.concatenate([x_prompt.reshape(np_tok, D), x_sample.reshape(bs * ts, D)], axis=0)
    xb = x.astype(_BF16)
    g_st = m_st = r_st = None
    conv_p, conv_s = [], []
    for l in range(depth):
        h = _matmul(xb, w_wide, l, bias=b_wide[l], name="in_proj")
        hs = _small_proj(x, w_in_t, b_in, l, (gdn_blk, ml_blk))
        oa, g_st = _gdn(h, hs, 0, gdn_lane, gdn_conv_w[l], gp, gdn_norm_g[l], state_gdn_S, state_gdn_conv, g_st, l, steps)
        tails = h.reshape(steps.n, _CHUNK, -1)[:, _CHUNK - nconv:, :qkv_w]
        conv_p.append(tails[steps.ncp - 1:steps.np_steps:steps.ncp])
        conv_s.append(tails[steps.np_steps + steps.ncs - 1::steps.ncs])
        ob, m_st = _mlstm(h, ml_col0, hs, 1, ml_lane, fb, mlstm_norm_g[l],
                          state_mlstm_C, state_mlstm_n, state_mlstm_m, m_st, l, steps)
        oc, r_st = _retention(h, ret_col0, lg, cos_tab, sin_tab, ret_norm_g[l], state_ret_S, r_st, l, steps)

        merged = _merge(oa, ob, oc, w_br, l, h, gate_col0)
        z1 = _matmul(merged, w_o, l, res=x, alpha=alpha, name="out_proj")
        x, xb = _layernorm(z1, ln1_g[l], ln1_b[l])

        act = _ffn_up(xb, w_ffn_up, l)
        z2 = _matmul(act, w_dn, l, res=x, alpha=alpha, bm=512, bn=512, name="ffn_down")
        x, xb = _layernorm(z2, ln2_g[l], ln2_b[l])

    gS_p, gS_s = g_st
    mC_p, mC_s, mn_p, mn_s, mm_p, mm_s = m_st
    rS_p, rS_s = r_st
    yp = x[:np_tok].reshape(bp, tp, D)
    ys = x[np_tok:].reshape(bs, ts, D)
    return (yp, ys,
            gS_p, jnp.stack(conv_p), mC_p, mn_p, mm_p[:, :, 0, :Hm], rS_p,
            gS_s, jnp.stack(conv_s), mC_s, mn_s, mm_s[:, :, 0, :Hm], rS_s)
```

```python
import functools

import jax
import jax.numpy as jnp
from jax import lax
from jax.experimental import pallas as pl
from jax.experimental.pallas import tpu as pltpu

_F32 = jnp.float32
_BF16 = jnp.bfloat16
_HI = lax.Precision.HIGHEST

_CHUNK = 64
_PAST_LEN = 1024
_ROPE_BASE = 10000.0
_LN_EPS = 1e-5
_LANE = 128
_SUBLANE = 8
_VMEM_LIMIT = 56 * 1024 * 1024


def _tile(dim, pref, align=_LANE):
    best = None
    t = align
    while t <= min(dim, pref):
        if dim % t == 0:
            best = t
        t += align
    return dim if best is None else best


def _params(*sem):
    return pltpu.CompilerParams(dimension_semantics=sem, vmem_limit_bytes=_VMEM_LIMIT)


def _bf(x):
    return x.astype(_BF16)


def _dot(a, b):
    return jnp.dot(a, b, preferred_element_type=_F32)


def _dot_nt(a, b):
    return lax.dot_general(a, b, (((1,), (1,)), ((), ())), preferred_element_type=_F32)


def _dot_tn(a, b):
    return lax.dot_general(a, b, (((0,), (0,)), ((), ())), preferred_element_type=_F32)


def _dot_hi(a, b):
    return jnp.dot(a, b, precision=_HI, preferred_element_type=_F32)


def _split(x):
    hi = x.astype(_BF16)
    return hi, (x - hi.astype(_F32)).astype(_BF16)


def _mm3(a, b):
    return _dot(a[0], b[0]) + (_dot(a[0], b[1]) + _dot(a[1], b[0]))


def _mm_kernel(*refs, nk, has_bias, has_res, alpha):
    it = iter(refs)
    x_ref, w_ref = next(it), next(it)
    b_ref = next(it) if has_bias else None
    r_ref = next(it) if has_res else None
    o_ref = next(it)
    acc_ref = next(it) if nk > 1 else None

    def finish(v):
        if has_bias:
            v = v + b_ref[...]
        if has_res:
            v = alpha * r_ref[...] + v
        o_ref[...] = v.astype(o_ref.dtype)

    part = jnp.dot(x_ref[...], _bf(w_ref[0]), preferred_element_type=_F32)
    if nk == 1:
        finish(part)
    else:
        k = pl.program_id(2)

        @pl.when(k == 0)
        def _():
            acc_ref[...] = part

        @pl.when(k > 0)
        def _():
            acc_ref[...] += part

        @pl.when(k == nk - 1)
        def _():
            finish(acc_ref[...])


def _matmul(x, w, layer, *, bias=None, res=None, alpha=1.0, out_dtype=_F32, bm=1024, bn=1024, bk=None, name):
    M, K = x.shape
    N = w.shape[2]
    bm, bn = _tile(M, bm), _tile(N, bn)
    bk = K if bk is None else _tile(K, bk)
    nk = K // bk
    in_specs = [pl.BlockSpec((bm, bk), lambda i, j, k: (i, k)),
                pl.BlockSpec((1, bk, bn), lambda i, j, k: (layer, k, j))]
    args = [x, w]
    if bias is not None:
        in_specs.append(pl.BlockSpec((1, bn), lambda i, j, k: (0, j)))
        args.append(bias.reshape(1, N).astype(_F32))
    if res is not None:
        in_specs.append(pl.BlockSpec((bm, bn), lambda i, j, k: (i, j)))
        args.append(res)
    return pl.pallas_call(
        functools.partial(_mm_kernel, nk=nk, has_bias=bias is not None, has_res=res is not None, alpha=alpha),
        grid=(M // bm, N // bn, nk),
        in_specs=in_specs,
        out_specs=pl.BlockSpec((bm, bn), lambda i, j, k: (i, j)),
        out_shape=jax.ShapeDtypeStruct((M, N), out_dtype),
        scratch_shapes=[pltpu.VMEM((bm, bn), _F32)] if nk > 1 else [],
        compiler_params=_params("parallel", "parallel", "arbitrary"),
        name=name,
    )(*args)


_REGROUP_EXTRA = 64


def _regroup_kernel(a_ref, b_ref, o_ref, *, regions):
    j = pl.program_id(2)

    def emit(lo, hi, shift):
        @pl.when(jnp.logical_and(j >= lo, j < hi))
        def _():
            if shift == 0:
                t = a_ref[0]
            else:
                t = jnp.concatenate([a_ref[0, shift:, :], b_ref[0, :shift, :]], axis=0)
            o_ref[0] = _bf(t.T)

    for lo, hi, shift in regions:
        emit(lo, hi, shift)


def _regroup_weights(wt, spans):
    depth, N, K = wt.shape
    starts, shifts, pos = [], [], 0
    for lo, hi in spans:
        starts.append(pos)
        shifts.append(lo - pos)
        pos += hi - lo
    assert all(0 <= s <= _REGROUP_EXTRA and s % _SUBLANE == 0 for s in shifts)
    bn = _LANE
    for t in range(_LANE, 512 + 1, _LANE):
        if all(v % t == 0 for v in starts + [pos]):
            bn = t
    bk = _tile(K, 2048)
    last_extra_block = (N - 1) // _REGROUP_EXTRA
    bounds = [s // bn for s in starts] + [pos // bn]
    regions = tuple((bounds[r], bounds[r + 1], shifts[r]) for r in range(len(spans)))
    return pl.pallas_call(
        functools.partial(_regroup_kernel, regions=regions),
        grid=(depth, K // bk, pos // bn),
        in_specs=[pl.BlockSpec((1, bn, bk), lambda l, i, j: (l, j, i)),
                  pl.BlockSpec((1, _REGROUP_EXTRA, bk),
                               lambda l, i, j: (l, jnp.minimum((j + 1) * (bn // _REGROUP_EXTRA), last_extra_block), i))],
        out_specs=pl.BlockSpec((1, bk, bn), lambda l, i, j: (l, i, j)),
        out_shape=jax.ShapeDtypeStruct((depth, K, pos), _BF16),
        compiler_params=_params("parallel", "parallel", "parallel"),
        name="regroup_w_in",
    )(wt, wt)


def _small_proj_kernel(x_ref, *refs, nblk):
    xs = _split(x_ref[...])
    w_refs, b_refs, o_ref = refs[:nblk], refs[nblk:2 * nblk], refs[2 * nblk]
    wh, wl = _split(jnp.concatenate([r[0] for r in w_refs], axis=0))
    acc = _dot_nt(xs[0], wh) + (_dot_nt(xs[0], wl) + _dot_nt(xs[1], wh))
    o_ref[...] = acc + jnp.concatenate([r[0] for r in b_refs], axis=1)


def _small_proj(x, wt, b, layer, row_blocks):
    M, K = x.shape
    depth, N = b.shape
    bm = _tile(M, 256)
    nblk = len(row_blocks)
    wspec = lambda c: pl.BlockSpec((1, _LANE, K), lambda i: (layer, c, 0))
    bspec = lambda c: pl.BlockSpec((1, 1, _LANE), lambda i: (layer, 0, c))
    return pl.pallas_call(
        functools.partial(_small_proj_kernel, nblk=nblk),
        grid=(M // bm,),
        in_specs=[pl.BlockSpec((bm, K), lambda i: (i, 0))] + [wspec(c) for c in row_blocks] + [bspec(c) for c in row_blocks],
        out_specs=pl.BlockSpec((bm, nblk * _LANE), lambda i: (i, 0)),
        out_shape=jax.ShapeDtypeStruct((M, nblk * _LANE), _F32),
        compiler_params=_params("parallel"),
        name="small_proj",
    )(x, *([wt] * nblk), *([b.reshape(depth, 1, N)] * nblk))


def _ln_kernel(z_ref, g_ref, b_ref, o_ref, ob_ref):
    z = z_ref[...]
    mu = jnp.mean(z, axis=-1, keepdims=True)
    d = z - mu
    var = jnp.mean(d * d, axis=-1, keepdims=True)
    y = d * lax.rsqrt(var + _LN_EPS) * g_ref[...] + b_ref[...]
    o_ref[...] = y
    ob_ref[...] = y.astype(_BF16)


def _layernorm(z, g, b):
    M, D = z.shape
    bm = _tile(M, 256)
    return pl.pallas_call(
        _ln_kernel,
        grid=(M // bm,),
        in_specs=[pl.BlockSpec((bm, D), lambda i: (i, 0)),
                  pl.BlockSpec((1, D), lambda i: (0, 0)),
                  pl.BlockSpec((1, D), lambda i: (0, 0))],
        out_specs=[pl.BlockSpec((bm, D), lambda i: (i, 0)),
                   pl.BlockSpec((bm, D), lambda i: (i, 0))],
        out_shape=[jax.ShapeDtypeStruct((M, D), _F32), jax.ShapeDtypeStruct((M, D), _BF16)],
        compiler_params=_params("parallel"),
        name="layernorm",
    )(z, g.reshape(1, D), b.reshape(1, D))


def _merge_kernel(a_ref, b_ref, c_ref, w_ref, g_ref, o_ref, acc_ref):
    br = pl.program_id(2)

    def contrib(x_ref):
        return jax.nn.sigmoid(g_ref[...]) * jnp.dot(x_ref[...], w_ref[0, 0], preferred_element_type=_F32)

    @pl.when(br == 0)
    def _():
        acc_ref[...] = contrib(a_ref)

    @pl.when(br == 1)
    def _():
        acc_ref[...] += contrib(b_ref)

    @pl.when(br == 2)
    def _():
        o_ref[...] = (acc_ref[...] + contrib(c_ref)).astype(o_ref.dtype)


def _merge(oa, ob, oc, w_branch, layer, h, gate_col0):
    M, BW = oa.shape
    D = w_branch.shape[3]
    bm, bn = _tile(M, 1024), _tile(D, 1024)
    nj = D // bn
    assert gate_col0 % bn == 0
    g0 = gate_col0 // bn
    xspec = pl.BlockSpec((bm, BW), lambda i, j, n: (i, 0))
    return pl.pallas_call(
        _merge_kernel,
        grid=(M // bm, nj, 3),
        in_specs=[xspec, xspec, xspec,
                  pl.BlockSpec((1, 1, BW, bn), lambda i, j, n: (layer, n, 0, j)),
                  pl.BlockSpec((bm, bn), lambda i, j, n: (i, g0 + n * nj + j))],
        out_specs=pl.BlockSpec((bm, bn), lambda i, j, n: (i, j)),
        out_shape=jax.ShapeDtypeStruct((M, D), _BF16),
        scratch_shapes=[pltpu.VMEM((bm, bn), _F32)],
        compiler_params=_params("parallel", "parallel", "arbitrary"),
        name="branch_merge",
    )(oa, ob, oc, w_branch, h)


def _ffn_up_kernel(x_ref, wg_ref, wv_ref, o_ref):
    x = x_ref[...]
    g = jnp.dot(x, _bf(wg_ref[0]), preferred_element_type=_F32)
    v = jnp.dot(x, _bf(wv_ref[0]), preferred_element_type=_F32)
    o_ref[...] = (g * jax.nn.sigmoid(g) * v).astype(o_ref.dtype)


def _ffn_up(x, w_up, layer):
    M, K = x.shape
    F = w_up.shape[2] // 2
    bm, bn = _tile(M, 1024), _tile(F, 256)
    nj = F // bn
    return pl.pallas_call(
        _ffn_up_kernel,
        grid=(M // bm, nj),
        in_specs=[pl.BlockSpec((bm, K), lambda i, j: (i, 0)),
                  pl.BlockSpec((1, K, bn), lambda i, j: (layer, 0, j)),
                  pl.BlockSpec((1, K, bn), lambda i, j: (layer, 0, nj + j))],
        out_specs=pl.BlockSpec((bm, bn), lambda i, j: (i, j)),
        out_shape=jax.ShapeDtypeStruct((M, F), _BF16),
        compiler_params=_params("parallel", "parallel"),
        name="ffn_up",
    )(x, w_up, w_up)


def _transpose_hi(x):
    n = x.shape[1]
    eye = (lax.broadcasted_iota(jnp.int32, (n, n), 0) == lax.broadcasted_iota(jnp.int32, (n, n), 1)).astype(_F32)
    return lax.dot_general(eye, x, (((1,), (1,)), ((), ())), precision=_HI, preferred_element_type=_F32)


def _silu(x):
    return x * jax.nn.sigmoid(x)


def _softplus(x):
    return jnp.maximum(x, 0.0) + jnp.log1p(jnp.exp(-jnp.abs(x)))


def _chunk_masks():
    L = _CHUNK
    row = lax.broadcasted_iota(jnp.int32, (L, L), 0)
    col = lax.broadcasted_iota(jnp.int32, (L, L), 1)
    return row, col


def _ln_head(o, g):
    mu = jnp.mean(o, axis=-1, keepdims=True)
    d = o - mu
    var = jnp.mean(d * d, axis=-1, keepdims=True)
    return d * lax.rsqrt(var + _LN_EPS) * g


def _unit_lower_inverses(As, row, col, left):
    def bdiag(ys):
        return tuple(jnp.concatenate([jnp.where(left, y, jnp.zeros_like(y)),
                                      jnp.where(left, jnp.zeros_like(y), y)], axis=0) for y in ys)

    same16 = (row // 16) == (col // 16)
    same32 = (row // 32) == (col // 32)
    eye = (row == col).astype(_F32)
    D = [jnp.where(same16, A, 0.0) for A in As]
    Ds = [_split(d) for d in D]
    D2s = [_split(_mm3(d, bdiag(d))) for d in Ds]
    D2b = [bdiag(d2) for d2 in D2s]
    P = [eye - d for d in D]
    P = [p + _mm3(_split(p), d2) for p, d2 in zip(P, D2b)]
    D4s = [_split(_mm3(d2, d2b)) for d2, d2b in zip(D2s, D2b)]
    D4b = [bdiag(d4) for d4 in D4s]
    P = [p + _mm3(_split(p), d4) for p, d4 in zip(P, D4b)]
    D8b = [bdiag(_split(_mm3(d4, d4b))) for d4, d4b in zip(D4s, D4b)]
    P = [p + _mm3(_split(p), d8) for p, d8 in zip(P, D8b)]
    for off_mask in (jnp.logical_and(same32, jnp.logical_not(same16)), jnp.logical_not(same32)):
        Es = [_split(jnp.where(off_mask, A, 0.0)) for A in As]
        Ps = [_split(p) for p in P]
        Qb = [bdiag(_split(_mm3(e, bdiag(ps)))) for e, ps in zip(Es, Ps)]
        P = [p - _mm3(ps, q) for p, ps, q in zip(P, Ps, Qb)]
    return P


class _Steps:
    def __init__(self, bp, tp, bs, ts):
        assert tp % _CHUNK == 0 and ts % _CHUNK == 0
        self.bp, self.bs = bp, bs
        self.ncp, self.ncs = tp // _CHUNK, ts // _CHUNK
        self.np_steps = bp * self.ncp
        self.n = self.np_steps + bs * self.ncs

    def in_prompt(self, i):
        return i < self.np_steps

    def _sample_step(self, i):
        return jnp.maximum(i - self.np_steps, 0)

    def seq(self, i):
        return jnp.where(self.in_prompt(i), i // self.ncp, self.bp + self._sample_step(i) // self.ncs)

    def sample_seq(self, i):
        return self._sample_step(i) // self.ncs

    def prompt_seq(self, i):
        return jnp.minimum(i // self.ncp, self.bp - 1)

    def chunk(self, i):
        return jnp.where(self.in_prompt(i), i % self.ncp, self._sample_step(i) % self.ncs)

    def pos_block(self, i):
        return jnp.where(self.in_prompt(i), self.chunk(i), self.ncp + self.chunk(i))

    def first_last(self, i):
        c = self.chunk(i)
        return c == 0, jnp.where(self.in_prompt(i), c == self.ncp - 1, c == self.ncs - 1)


def _gdn_kernel(q_ref, k_ref, v_ref, z_ref, gs_ref, cwq_ref, cwk_ref, cwv_ref, gp_ref, ng_ref,
                s0_ref, c0q_ref, c0k_ref, c0v_ref, *rest, hpg, dk, dv, lane0, steps):
    o_ref, sp_ref, ss_ref, s_scr, bq, bk, bv = rest[-7:]
    L, P = _CHUNK, _SUBLANE
    i = pl.program_id(1)
    first, last = steps.first_last(i)
    in_p = steps.in_prompt(i)
    nc = c0q_ref.shape[2]

    @pl.when(jnp.logical_and(first, in_p))
    def _():
        s_scr[...] = jnp.zeros(s_scr.shape, _F32)
        for buf in (bq, bk, bv):
            buf[0:P, :] = jnp.zeros((P, buf.shape[1]), _F32)

    @pl.when(jnp.logical_and(first, jnp.logical_not(in_p)))
    def _():
        s_scr[...] = s0_ref[0, 0]
        bq[P - nc:P, :] = c0q_ref[0, 0]
        bk[P - nc:P, :] = c0k_ref[0, 0]
        bv[P - nc:P, :] = c0v_ref[0, 0]

    bq[P:P + L, :] = q_ref[...]
    bk[P:P + L, :] = k_ref[...]
    bv[P:P + L, :] = v_ref[...]

    row, col = _chunk_masks()
    tril = (row >= col).astype(_F32)
    gs = gs_ref[...]
    gp = gp_ref[0]
    beta_blk = jax.nn.sigmoid(gs)
    g_blk = -jnp.exp(gp[0:1, :]) * _softplus(gs + gp[1:2, :])
    G_blk = _dot_hi(tril, g_blk)
    GT2 = _transpose_hi(jnp.concatenate([G_blk, G_blk], axis=0))
    ng = ng_ref[...]
    heads = range(hpg)

    def conv(buf, cw_ref, cs):
        out = buf[P - nc:P - nc + L, cs] * cw_ref[0:1, cs]
        for j in range(1, nc + 1):
            out = out + buf[P - nc + j:P - nc + j + L, cs] * cw_ref[j:j + 1, cs]
        return out

    def l2norm(x):
        return x * lax.rsqrt(jnp.sum(x * x, axis=-1, keepdims=True) + 1e-6)

    ks = [slice(h * dk, (h + 1) * dk) for h in heads]
    vs = [slice(h * dv, (h + 1) * dv) for h in heads]
    q = [l2norm(_silu(conv(bq, cwq_ref, ks[h]))) * dk ** -0.5 for h in heads]
    k = [l2norm(_silu(conv(bk, cwk_ref, ks[h]))) for h in heads]
    v = [_silu(conv(bv, cwv_ref, vs[h])) for h in heads]
    la = lane0 + hpg
    beta = [beta_blk[:, lane0 + h:lane0 + h + 1] for h in heads]
    Gc = [G_blk[:, la + h:la + h + 1] for h in heads]
    eG = [jnp.exp(Gc[h]) for h in heads]
    kb = [_bf(k[h]) for h in heads]
    qb = [_bf(q[h]) for h in heads]

    pairs = range(hpg // 2)
    lane = lax.broadcasted_iota(jnp.int32, (L, 2 * L), 1)
    left = lane < L
    rowp = lax.broadcasted_iota(jnp.int32, (L, 2 * L), 0)
    colp = jnp.where(left, lane, lane - L)
    inclp, strictp = rowp >= colp, rowp > colp

    def side(a0, a1):
        return jnp.where(left, a0, a1)

    def bdiag(y0, y1):
        return jnp.concatenate([jnp.concatenate([y0, jnp.zeros_like(y1)], axis=1),
                                jnp.concatenate([jnp.zeros_like(y0), y1], axis=1)], axis=0)

    Gcp = [side(Gc[2 * p], Gc[2 * p + 1]) for p in pairs]
    Grp = [side(GT2[la + 2 * p:la + 2 * p + 1, :], GT2[la + 2 * p + 1:la + 2 * p + 2, :]) for p in pairs]
    betap = [side(beta[2 * p], beta[2 * p + 1]) for p in pairs]
    gam = [jnp.where(inclp, jnp.exp(jnp.where(inclp, Gcp[p] - Grp[p], 0.0)), 0.0) for p in pairs]
    kq = [_dot_nt(jnp.concatenate([jnp.concatenate([kb[2 * p], kb[2 * p + 1]], axis=1),
                                   jnp.concatenate([qb[2 * p], qb[2 * p + 1]], axis=1)], axis=0),
                  bdiag(kb[2 * p], kb[2 * p + 1])) for p in pairs]
    A = [jnp.where(strictp, betap[p] * kq[p][0:L] * gam[p], 0.0) for p in pairs]
    qk = [_bf(kq[p][L:2 * L] * gam[p]) for p in pairs]
    T = _unit_lower_inverses(A, rowp, colp, left)
    rhs = [_split(jnp.concatenate([beta[h] * v[h], (beta[h] * eG[h]) * k[h]], axis=1)) for h in heads]
    UWp = [_mm3(_split(T[p]), tuple(bdiag(rhs[2 * p][t], rhs[2 * p + 1][t]) for t in range(2))) for p in pairs]
    UW = [UWp[h // 2][:, (h % 2) * (dv + dk):(h % 2 + 1) * (dv + dk)] for h in heads]
    G_last = [Gc[h][L - 1:L, :] for h in heads]
    S = [s_scr[h] for h in heads]
    Sb = [_bf(S[h]) for h in heads]
    WQ = [_dot(jnp.concatenate([_bf(UW[h][:, dv:]), _bf(q[h] * eG[h])], axis=0), Sb[h]) for h in heads]
    db = [_bf(UW[h][:, 0:dv] - WQ[h][0:L]) for h in heads]
    qd = [_dot(qk[p], bdiag(db[2 * p], db[2 * p + 1])) for p in pairs]
    o = [WQ[h][L:2 * L] + qd[h // 2][:, (h % 2) * dv:(h % 2 + 1) * dv] for h in heads]
    for h in heads:
        k_dec = k[h] * jnp.exp(G_last[h] - Gc[h])
        s_scr[h] = jnp.exp(G_last[h]) * S[h] + _dot_tn(_bf(k_dec), db[h])
    for h in heads:
        on = o[h] * lax.rsqrt(jnp.mean(o[h] * o[h], axis=-1, keepdims=True) + 1e-6) * ng
        o_ref[:, vs[h]] = (on * _silu(z_ref[:, vs[h]])).astype(o_ref.dtype)

    bq[0:P, :] = bq[L:L + P, :]
    bk[0:P, :] = bk[L:L + P, :]
    bv[0:P, :] = bv[L:L + P, :]

    @pl.when(jnp.logical_and(last, in_p))
    def _():
        sp_ref[0, 0] = s_scr[...]

    @pl.when(jnp.logical_and(last, jnp.logical_not(in_p)))
    def _():
        ss_ref[0, 0] = s_scr[...]


def _state_outputs(prev, shapes_p, shapes_s, n_in, n_lead_out):
    shapes = [jax.ShapeDtypeStruct(s, _F32) for pair in zip(shapes_p, shapes_s) for s in pair]
    if prev is None:
        return shapes, [], {}, ()
    specs = [pl.BlockSpec(memory_space=pl.ANY)] * len(prev)
    aliases = {n_in + j: n_lead_out + j for j in range(len(prev))}
    return shapes, specs, aliases, tuple(prev)


def _gdn(h, hs, gate_block, lane0, conv_w, gp, norm_g, s_in, conv_in, prev, layer, steps):
    ntok = h.shape[0]
    depth, _, H, dk, dv = s_in.shape
    nc = conv_in.shape[2]
    assert dk == dv and nc < _SUBLANE and conv_w.shape[0] == nc + 1 and lane0 + 2 * H <= _LANE and H % 2 == 0
    hpg, ng_ = H, 1
    L, P = _CHUNK, _SUBLANE
    wq = hpg * dk
    sseq, pseq = steps.sample_seq, steps.prompt_seq
    tok = lambda off: pl.BlockSpec((L, wq), lambda g, i: (i, off + g))
    cw = lambda off: pl.BlockSpec((nc + 1, wq), lambda g, i: (0, off + g))
    c0 = lambda off: pl.BlockSpec((1, 1, nc, wq), lambda g, i: (layer, sseq(i), 0, off + g))
    in_specs = [tok(0), tok(ng_), tok(2 * ng_), tok(3 * ng_),
                pl.BlockSpec((L, _LANE), lambda g, i: (i, gate_block)),
                cw(0), cw(ng_), cw(2 * ng_),
                pl.BlockSpec((1, 2, _LANE), lambda g, i: (layer, 0, 0)),
                pl.BlockSpec((1, dv), lambda g, i: (0, 0)),
                pl.BlockSpec((1, 1, hpg, dk, dv), lambda g, i: (layer, sseq(i), g, 0, 0)),
                c0(0), c0(ng_), c0(2 * ng_)]
    shapes, pspecs, aliases, pargs = _state_outputs(
        prev, [(depth, steps.bp, H, dk, dv)], [(depth, steps.bs, H, dk, dv)], len(in_specs), 1)
    out = pl.pallas_call(
        functools.partial(_gdn_kernel, hpg=hpg, dk=dk, dv=dv, lane0=lane0, steps=steps),
        grid=(ng_, steps.n),
        in_specs=in_specs + pspecs,
        out_specs=[pl.BlockSpec((L, hpg * dv), lambda g, i: (i, g)),
                   pl.BlockSpec((1, 1, hpg, dk, dv), lambda g, i: (layer, pseq(i), g, 0, 0)),
                   pl.BlockSpec((1, 1, hpg, dk, dv), lambda g, i: (layer, sseq(i), g, 0, 0))],
        out_shape=[jax.ShapeDtypeStruct((ntok, H * dv), _BF16)] + shapes,
        input_output_aliases=aliases,
        scratch_shapes=[pltpu.VMEM((hpg, dk, dv), _F32)] + [pltpu.VMEM((P + L, wq), _F32)] * 3,
        compiler_params=_params("parallel", "arbitrary"),
        name="gdn_mixer",
    )(h, h, h, h, hs, conv_w, conv_w, conv_w, gp, norm_g.reshape(1, dv), s_in, conv_in, conv_in, conv_in, *pargs)
    return out[0], out[1:]


def _mlstm_kernel(q_ref, k_ref, v_ref, og_ref, gs_ref, fb_ref, ng_ref, c0_ref, n0_ref, m0_ref,
                  *rest, H, dk, dv, lane0, steps):
    o_ref, cp_ref, cs_ref, np_ref, ns_ref, mp_ref, ms_ref, c_scr, n_scr, m_scr = rest[-10:]
    L, P = _CHUNK, _SUBLANE
    i = pl.program_id(0)
    first, last = steps.first_last(i)
    in_p = steps.in_prompt(i)

    @pl.when(jnp.logical_and(first, in_p))
    def _():
        c_scr[...] = jnp.zeros(c_scr.shape, _F32)
        n_scr[...] = jnp.zeros(n_scr.shape, _F32)
        m_scr[...] = jnp.zeros(m_scr.shape, _F32)

    @pl.when(jnp.logical_and(first, jnp.logical_not(in_p)))
    def _():
        c_scr[...] = c0_ref[0, 0]
        n_scr[...] = n0_ref[0, 0]
        m0 = m0_ref[0, 0]
        for h in range(H):
            m_scr[h] = jnp.broadcast_to(m0[:, h:h + 1], (P, _LANE))

    row, col = _chunk_masks()
    incl = row >= col
    tril = incl.astype(_F32)
    gs = gs_ref[...]
    li, lf = lane0, lane0 + H
    lf_blk = -_softplus(-(gs + fb_ref[0]))
    F_blk = _dot_hi(tril, lf_blk)
    FT = _transpose_hi(F_blk)
    gsT = _transpose_hi(gs)

    heads = range(H)
    ks = [slice(h * dk, (h + 1) * dk) for h in heads]
    vs = [slice(h * dv, (h + 1) * dv) for h in heads]
    q = [q_ref[:, ks[h]] * dk ** -0.5 for h in heads]
    k = [k_ref[:, ks[h]] for h in heads]
    qb = [_bf(q[h]) for h in heads]
    kb = [_bf(k[h]) for h in heads]
    vb = [_bf(v_ref[:, vs[h]]) for h in heads]
    qk = [_dot_nt(qb[h], kb[h]) for h in heads]
    C = [c_scr[h] for h in heads]
    qC = [_dot(qb[h], _bf(C[h])) for h in heads]
    igc = [gs[:, li + h:li + h + 1] for h in heads]
    Fc = [F_blk[:, lf + h:lf + h + 1] for h in heads]
    logD = [jnp.where(incl, Fc[h] - FT[lf + h:lf + h + 1, :] + gsT[li + h:li + h + 1, :], -jnp.inf) for h in heads]
    m_prev = [m_scr[h][0:1, 0:1] for h in heads]
    m_t = [jnp.maximum(Fc[h] + m_prev[h], jnp.max(logD[h], axis=-1, keepdims=True)) for h in heads]
    Sm = [jnp.exp(logD[h] - m_t[h]) * qk[h] for h in heads]
    Sv = [_dot(_bf(Sm[h]), vb[h]) for h in heads]
    inter = [jnp.exp(Fc[h] + m_prev[h] - m_t[h]) for h in heads]
    n = [n_scr[h:h + 1, :] for h in heads]
    m_last = [m_t[h][L - 1:L, :] for h in heads]
    wk = [jnp.exp(Fc[h][L - 1:L, :] - Fc[h] + igc[h] - m_last[h]) * k[h] for h in heads]
    kv = [_dot_tn(_bf(wk[h]), vb[h]) for h in heads]
    for h in heads:
        dec = inter[h][L - 1:L, :]
        c_scr[h] = dec * C[h] + kv[h]
        n_scr[h:h + 1, :] = dec * n[h] + jnp.sum(wk[h], axis=0, keepdims=True)
        m_scr[h] = jnp.broadcast_to(m_last[h], (P, _LANE))
    for h in heads:
        num = inter[h] * qC[h] + Sv[h]
        den = inter[h] * jnp.sum(q[h] * n[h], axis=-1, keepdims=True) + jnp.sum(Sm[h], axis=-1, keepdims=True)
        hh = num / jnp.maximum(jnp.abs(den), jnp.exp(-m_t[h]))
        o_ref[:, vs[h]] = (_ln_head(hh, ng_ref[:, vs[h]]) * jax.nn.sigmoid(og_ref[:, vs[h]])).astype(o_ref.dtype)

    def write_state(c_ref, n_ref, m_ref):
        c_ref[0, 0] = c_scr[...]
        n_ref[0, 0] = n_scr[...]
        lane = lax.broadcasted_iota(jnp.int32, (1, _LANE), 1)
        m_row = jnp.zeros((1, _LANE), _F32)
        for h in range(H):
            m_row = jnp.where(lane == h, m_scr[h][0:1, :], m_row)
        m_ref[0, 0] = m_row

    @pl.when(jnp.logical_and(last, in_p))
    def _():
        write_state(cp_ref, np_ref, mp_ref)

    @pl.when(jnp.logical_and(last, jnp.logical_not(in_p)))
    def _():
        write_state(cs_ref, ns_ref, ms_ref)


def _mlstm(h, col0, hs, gate_block, lane0, f_bias, norm_g, c_in, n_in, m_in, prev, layer, steps):
    ntok = h.shape[0]
    depth, _, H, dk, dv = c_in.shape
    assert dv % dk == 0 and 2 % (dv // dk) == 0 and lane0 + 2 * H <= _LANE and col0 % (H * dv) == 0
    L = _CHUNK
    bp, bs = steps.bp, steps.bs
    jq, jv = col0 // (H * dk), col0 // (H * dv) + 2 * dk // dv
    tq = lambda j: pl.BlockSpec((L, H * dk), lambda i: (i, jq + j))
    tv = lambda j: pl.BlockSpec((L, H * dv), lambda i: (i, jv + j))
    sseq, pseq = steps.sample_seq, steps.prompt_seq
    in_specs = [tq(0), tq(1), tv(0), tv(1),
                pl.BlockSpec((L, _LANE), lambda i: (i, gate_block)),
                pl.BlockSpec((1, 1, _LANE), lambda i: (layer, 0, 0)),
                pl.BlockSpec((1, H * dv), lambda i: (0, 0)),
                pl.BlockSpec((1, 1, H, dk, dv), lambda i: (layer, sseq(i), 0, 0, 0)),
                pl.BlockSpec((1, 1, H, dk), lambda i: (layer, sseq(i), 0, 0)),
                pl.BlockSpec((1, 1, 1, H), lambda i: (layer, sseq(i), 0, 0))]
    tails = [(H, dk, dv), (H, dk), (1, _LANE)]
    shapes, pspecs, aliases, pargs = _state_outputs(
        prev, [(depth, bp) + t for t in tails], [(depth, bs) + t for t in tails], len(in_specs), 1)
    state_specs = []
    for t in tails:
        zeros = (0,) * len(t)
        state_specs.append(pl.BlockSpec((1, 1) + t, lambda i, z=zeros: (layer, pseq(i)) + z))
        state_specs.append(pl.BlockSpec((1, 1) + t, lambda i, z=zeros: (layer, sseq(i)) + z))
    out = pl.pallas_call(
        functools.partial(_mlstm_kernel, H=H, dk=dk, dv=dv, lane0=lane0, steps=steps),
        grid=(steps.n,),
        in_specs=in_specs + pspecs,
        out_specs=[pl.BlockSpec((L, H * dv), lambda i: (i, 0))] + state_specs,
        out_shape=[jax.ShapeDtypeStruct((ntok, H * dv), _BF16)] + shapes,
        input_output_aliases=aliases,
        scratch_shapes=[pltpu.VMEM((H, dk, dv), _F32), pltpu.VMEM((H, dk), _F32),
                        pltpu.VMEM((H, _SUBLANE, _LANE), _F32)],
        compiler_params=_params("arbitrary"),
        name="mlstm_mixer",
    )(h, h, h, h, hs, f_bias, norm_g.reshape(1, H * dv), c_in, n_in,
      m_in.reshape(m_in.shape[0], m_in.shape[1], 1, H), *pargs)
    return out[0], out[1:]


def _ret_kernel(lg_ref, q_ref, k_ref, v_ref, og_ref, cos_ref, sin_ref, ng_ref, s0_ref,
                *rest, H, dk, dv, steps):
    o_ref, sp_ref, ss_ref, s_scr = rest[-4:]
    L = _CHUNK
    i = pl.program_id(0)
    first, last = steps.first_last(i)
    in_p = steps.in_prompt(i)

    @pl.when(jnp.logical_and(first, in_p))
    def _():
        s_scr[...] = jnp.zeros(s_scr.shape, _F32)

    @pl.when(jnp.logical_and(first, jnp.logical_not(in_p)))
    def _():
        s_scr[...] = s0_ref[0, 0]

    row, col = _chunk_masks()
    incl = row >= col
    tdiff = (row - col).astype(_F32)
    jc = lax.broadcasted_iota(jnp.int32, (L, 1), 0).astype(_F32)
    cos, sin = cos_ref[...], sin_ref[...]

    def rotary(x):
        return x * cos + pltpu.roll(x, dk // 2, 1) * sin

    heads = range(H)
    ks = [slice(h * dk, (h + 1) * dk) for h in heads]
    vs = [slice(h * dv, (h + 1) * dv) for h in heads]
    lg = [lg_ref[h] for h in heads]
    q = [rotary(q_ref[:, ks[h]]) for h in heads]
    k = [rotary(k_ref[:, ks[h]]) * dk ** -0.5 for h in heads]
    vb = [_bf(v_ref[:, vs[h]]) for h in heads]
    qk = [_dot_nt(_bf(q[h]), _bf(k[h])) for h in heads]
    S = [s_scr[h] for h in heads]
    qS = [_dot(_bf(q[h] * jnp.exp(lg[h] * (jc + 1.0))), _bf(S[h])) for h in heads]
    kv = [_dot_tn(_bf(k[h] * jnp.exp(lg[h] * (L - 1.0 - jc))), vb[h]) for h in heads]
    D = [jnp.where(incl, jnp.exp(jnp.where(incl, lg[h] * tdiff, 0.0)), 0.0) for h in heads]
    o = [_dot(_bf(qk[h] * D[h]), vb[h]) + qS[h] for h in heads]
    for h in heads:
        s_scr[h] = jnp.exp(lg[h] * jnp.full((1, 1), float(L), _F32)) * S[h] + kv[h]
    for h in heads:
        o_ref[:, vs[h]] = (_ln_head(o[h], ng_ref[:, vs[h]]) * _silu(og_ref[:, vs[h]])).astype(o_ref.dtype)

    @pl.when(jnp.logical_and(last, in_p))
    def _():
        sp_ref[0, 0] = s_scr[...]

    @pl.when(jnp.logical_and(last, jnp.logical_not(in_p)))
    def _():
        ss_ref[0, 0] = s_scr[...]


def _retention(h, col0, lg, cos_tab, sin_tab, norm_g, s_in, prev, layer, steps):
    ntok = h.shape[0]
    depth, _, H, dk, dv = s_in.shape
    assert dv % dk == 0 and 2 % (dv // dk) == 0 and col0 % (H * dv) == 0
    L = _CHUNK
    jq, jv = col0 // (H * dk), col0 // (H * dv) + 2 * dk // dv
    tq = lambda j: pl.BlockSpec((L, H * dk), lambda i: (i, jq + j))
    tv = lambda j: pl.BlockSpec((L, H * dv), lambda i: (i, jv + j))
    tab = pl.BlockSpec((L, dk), lambda i: (steps.pos_block(i), 0))
    sseq, pseq = steps.sample_seq, steps.prompt_seq
    in_specs = [pl.BlockSpec(memory_space=pltpu.SMEM),
                tq(0), tq(1), tv(0), tv(1), tab, tab,
                pl.BlockSpec((1, H * dv), lambda i: (0, 0)),
                pl.BlockSpec((1, 1, H, dk, dv), lambda i: (layer, sseq(i), 0, 0, 0))]
    shapes, pspecs, aliases, pargs = _state_outputs(
        prev, [(depth, steps.bp, H, dk, dv)], [(depth, steps.bs, H, dk, dv)], len(in_specs), 1)
    out = pl.pallas_call(
        functools.partial(_ret_kernel, H=H, dk=dk, dv=dv, steps=steps),
        grid=(steps.n,),
        in_specs=in_specs + pspecs,
        out_specs=[pl.BlockSpec((L, H * dv), lambda i: (i, 0)),
                   pl.BlockSpec((1, 1, H, dk, dv), lambda i: (layer, pseq(i), 0, 0, 0)),
                   pl.BlockSpec((1, 1, H, dk, dv), lambda i: (layer, sseq(i), 0, 0, 0))],
        out_shape=[jax.ShapeDtypeStruct((ntok, H * dv), _BF16)] + shapes,
        input_output_aliases=aliases,
        scratch_shapes=[pltpu.VMEM((H, dk, dv), _F32)],
        compiler_params=_params("arbitrary"),
        name="retention_mixer",
    )(lg, h, h, h, h, cos_tab, sin_tab, norm_g.reshape(1, H * dv), s_in, *pargs)
    return out[0], out[1:]


def _rotary_tables(tp, ts, dk):
    half = dk // 2
    freq = _ROPE_BASE ** (-jnp.arange(half, dtype=_F32) / half)
    pos = jnp.concatenate([0.0 + jnp.arange(tp, dtype=_F32), float(_PAST_LEN) + jnp.arange(ts, dtype=_F32)])
    ang = pos[:, None] * freq[None, :]
    cos, sin = jnp.cos(ang), jnp.sin(ang)
    return jnp.concatenate([cos, cos], axis=1), jnp.concatenate([-sin, sin], axis=1)


def kernel(x_prompt, x_sample, state_gdn_S, state_gdn_conv, state_mlstm_C, state_mlstm_n, state_mlstm_m, state_ret_S, w_in, b_in, gdn_conv_w, gdn_A_log, gdn_dt_bias, gdn_norm_g, mlstm_f_bias, mlstm_norm_g, ret_norm_g, w_branch, w_out, ln1_g, ln1_b, ln2_g, ln2_b, w_ffn_up, w_ffn_down):
    bp, tp, D = x_prompt.shape
    bs, ts, _ = x_sample.shape
    depth = w_in.shape[0]
    alpha = (2 * depth) ** 0.25
    Hg, gdk, gdv = state_gdn_S.shape[2:]
    Hm, mdk, mdv = state_mlstm_C.shape[2:]
    Hr, rdk, rdv = state_ret_S.shape[2:]
    nconv = state_gdn_conv.shape[2]
    steps = _Steps(bp, tp, bs, ts)
    np_tok = bp * tp

    widths = (2 * Hg * gdk + Hg * gdv, Hg * gdv, Hg, Hg,
              Hm * mdk, Hm * mdk, Hm * mdv, Hm * mdv, Hm, Hm,
              Hr * rdk, Hr * rdk, Hr * rdv, Hr * rdv, 3 * D)
    offs = [0]
    for w_ in widths:
        offs.append(offs[-1] + w_)
    o_qkv, _, o_b, o_a, o_mq, _, _, _, o_mi, o_mf, o_rq, _, _, _, _, o_end = offs
    assert o_end == w_in.shape[2]
    qkv_w = widths[0]
    wide = ((o_qkv, o_b), (o_mq, o_mi), (o_rq, o_end))
    ml_col0 = o_b - o_qkv
    ret_col0 = ml_col0 + (o_mi - o_mq)
    gate_col0 = ret_col0 + sum(widths[10:14])

    gdn_blk, gdn_lane = o_b // _LANE, o_b % _LANE
    ml_blk, ml_lane = o_mi // _LANE, o_mi % _LANE
    assert o_a == o_b + Hg and o_mf == o_mi + Hm

    def wide_cols(a):
        return jnp.concatenate([a[..., lo:hi] for lo, hi in wide], axis=-1)

    cos_tab, sin_tab = _rotary_tables(tp, ts, rdk)
    lg = jnp.log(1.0 - 2.0 ** (-5.0 - jnp.arange(Hr, dtype=_F32)))

    w_in_t = jnp.swapaxes(w_in, 1, 2)
    w_wide, b_wide = _regroup_weights(w_in_t, wide), wide_cols(b_in)
    w_br, w_dn = _bf(w_branch), _bf(w_ffn_down)
    la = gdn_lane + Hg
    gp = jnp.zeros((depth, 2, _LANE), _F32)
    gp = gp.at[:, 0, la:la + Hg].set(gdn_A_log).at[:, 1, la:la + Hg].set(gdn_dt_bias)
    lf = ml_lane + Hm
    fb = jnp.zeros((depth, 1, _LANE), _F32).at[:, 0, lf:lf + Hm].set(mlstm_f_bias)

    x = jnp.concatenate([x_prompt.reshape(np_tok, D), x_sample.reshape(bs * ts, D)], axis=0)
    xb = x.astype(_BF16)
    g_st = m_st = r_st = None
    conv_p, conv_s = [], []
    for l in range(depth):
        h = _matmul(xb, w_wide, l, bias=b_wide[l], name="in_proj")
        hs = _small_proj(x, w_in_t, b_in, l, (gdn_blk, ml_blk))
        oa, g_st = _gdn(h, hs, 0, gdn_lane, gdn_conv_w[l], gp, gdn_norm_g[l], state_gdn_S, state_gdn_conv, g_st, l, steps)
        tails = h.reshape(steps.n, _CHUNK, -1)[:, _CHUNK - nconv:, :qkv_w]
        conv_p.append(tails[steps.ncp - 1:steps.np_steps:steps.ncp])
        conv_s.append(tails[steps.np_steps + steps.ncs - 1::steps.ncs])
        ob, m_st = _mlstm(h, ml_col0, hs, 1, ml_lane, fb, mlstm_norm_g[l],
                          state_mlstm_C, state_mlstm_n, state_mlstm_m, m_st, l, steps)
        oc, r_st = _retention(h, ret_col0, lg, cos_tab, sin_tab, ret_norm_g[l], state_ret_S, r_st, l, steps)

        merged = _merge(oa, ob, oc, w_br, l, h, gate_col0)
        z1 = _matmul(merged, w_out, l, res=x, alpha=alpha, bn=512, name="out_proj")
        x, xb = _layernorm(z1, ln1_g[l], ln1_b[l])

        act = _ffn_up(xb, w_ffn_up, l)
        z2 = _matmul(act, w_dn, l, res=x, alpha=alpha, bm=512, bn=512, name="ffn_down")
        x, xb = _layernorm(z2, ln2_g[l], ln2_b[l])

    gS_p, gS_s = g_st
    mC_p, mC_s, mn_p, mn_s, mm_p, mm_s = m_st
    rS_p, rS_s = r_st
    yp = x[:np_tok].reshape(bp, tp, D)
    ys = x[np_tok:].reshape(bs, ts, D)
    return (yp, ys,
            gS_p, jnp.stack(conv_p), mC_p, mn_p, mm_p[:, :, 0, :Hm], rS_p,
            gS_s, jnp.stack(conv_s), mC_s, mn_s, mm_s[:, :, 0, :Hm], rS_s)
```

```python
import functools

import jax
import jax.numpy as jnp
from jax import lax
from jax.experimental import pallas as pl
from jax.experimental.pallas import tpu as pltpu

_F32 = jnp.float32
_BF16 = jnp.bfloat16
_HI = lax.Precision.HIGHEST

_CHUNK = 64
_PAST_LEN = 1024
_ROPE_BASE = 10000.0
_LN_EPS = 1e-5
_LANE = 128
_SUBLANE = 8
_VMEM_LIMIT = 56 * 1024 * 1024


def _tile(dim, pref, align=_LANE):
    best = None
    t = align
    while t <= min(dim, pref):
        if dim % t == 0:
            best = t
        t += align
    return dim if best is None else best


def _params(*sem):
    return pltpu.CompilerParams(dimension_semantics=sem, vmem_limit_bytes=_VMEM_LIMIT)


def _bf(x):
    return x.astype(_BF16)


def _dot(a, b):
    return jnp.dot(a, b, preferred_element_type=_F32)


def _dot_nt(a, b):
    return lax.dot_general(a, b, (((1,), (1,)), ((), ())), preferred_element_type=_F32)


def _dot_tn(a, b):
    return lax.dot_general(a, b, (((0,), (0,)), ((), ())), preferred_element_type=_F32)


def _dot_hi(a, b):
    return jnp.dot(a, b, precision=_HI, preferred_element_type=_F32)


def _split(x):
    hi = x.astype(_BF16)
    return hi, (x - hi.astype(_F32)).astype(_BF16)


def _mm3(a, b):
    return _dot(a[0], b[0]) + (_dot(a[0], b[1]) + _dot(a[1], b[0]))


def _mm_kernel(*refs, nk, has_bias, has_res, alpha):
    it = iter(refs)
    x_ref, w_ref = next(it), next(it)
    b_ref = next(it) if has_bias else None
    r_ref = next(it) if has_res else None
    o_ref = next(it)
    acc_ref = next(it) if nk > 1 else None

    def finish(v):
        if has_bias:
            v = v + b_ref[...]
        if has_res:
            v = alpha * r_ref[...] + v
        o_ref[...] = v.astype(o_ref.dtype)

    part = jnp.dot(x_ref[...], w_ref[0], preferred_element_type=_F32)
    if nk == 1:
        finish(part)
    else:
        k = pl.program_id(2)

        @pl.when(k == 0)
        def _():
            acc_ref[...] = part

        @pl.when(k > 0)
        def _():
            acc_ref[...] += part

        @pl.when(k == nk - 1)
        def _():
            finish(acc_ref[...])


def _matmul(x, w, layer, *, bias=None, res=None, alpha=1.0, out_dtype=_F32, bm=1024, bn=1024, bk=None, name):
    M, K = x.shape
    N = w.shape[2]
    bm, bn = _tile(M, bm), _tile(N, bn)
    bk = K if bk is None else _tile(K, bk)
    nk = K // bk
    in_specs = [pl.BlockSpec((bm, bk), lambda i, j, k: (i, k)),
                pl.BlockSpec((1, bk, bn), lambda i, j, k: (layer, k, j))]
    args = [x, w]
    if bias is not None:
        in_specs.append(pl.BlockSpec((1, bn), lambda i, j, k: (0, j)))
        args.append(bias.reshape(1, N).astype(_F32))
    if res is not None:
        in_specs.append(pl.BlockSpec((bm, bn), lambda i, j, k: (i, j)))
        args.append(res)
    return pl.pallas_call(
        functools.partial(_mm_kernel, nk=nk, has_bias=bias is not None, has_res=res is not None, alpha=alpha),
        grid=(M // bm, N // bn, nk),
        in_specs=in_specs,
        out_specs=pl.BlockSpec((bm, bn), lambda i, j, k: (i, j)),
        out_shape=jax.ShapeDtypeStruct((M, N), out_dtype),
        scratch_shapes=[pltpu.VMEM((bm, bn), _F32)] if nk > 1 else [],
        compiler_params=_params("parallel", "parallel", "arbitrary"),
        name=name,
    )(*args)


_REGROUP_EXTRA = 64


def _regroup_kernel(a_ref, b_ref, o_ref, *, regions):
    j = pl.program_id(2)

    def emit(lo, hi, shift):
        @pl.when(jnp.logical_and(j >= lo, j < hi))
        def _():
            if shift == 0:
                t = a_ref[0]
            else:
                t = jnp.concatenate([a_ref[0, shift:, :], b_ref[0, :shift, :]], axis=0)
            o_ref[0] = _bf(t.T)

    for lo, hi, shift in regions:
        emit(lo, hi, shift)


def _regroup_weights(wt, spans):
    depth, N, K = wt.shape
    starts, shifts, pos = [], [], 0
    for lo, hi in spans:
        starts.append(pos)
        shifts.append(lo - pos)
        pos += hi - lo
    assert all(0 <= s <= _REGROUP_EXTRA and s % _SUBLANE == 0 for s in shifts)
    bn = _LANE
    for t in range(_LANE, 512 + 1, _LANE):
        if all(v % t == 0 for v in starts + [pos]):
            bn = t
    bk = _tile(K, 2048)
    last_extra_block = (N - 1) // _REGROUP_EXTRA
    bounds = [s // bn for s in starts] + [pos // bn]
    regions = tuple((bounds[r], bounds[r + 1], shifts[r]) for r in range(len(spans)))
    return pl.pallas_call(
        functools.partial(_regroup_kernel, regions=regions),
        grid=(depth, K // bk, pos // bn),
        in_specs=[pl.BlockSpec((1, bn, bk), lambda l, i, j: (l, j, i)),
                  pl.BlockSpec((1, _REGROUP_EXTRA, bk),
                               lambda l, i, j: (l, jnp.minimum((j + 1) * (bn // _REGROUP_EXTRA), last_extra_block), i))],
        out_specs=pl.BlockSpec((1, bk, bn), lambda l, i, j: (l, i, j)),
        out_shape=jax.ShapeDtypeStruct((depth, K, pos), _BF16),
        compiler_params=_params("parallel", "parallel", "parallel"),
        name="regroup_w_in",
    )(wt, wt)


def _small_proj_kernel(x_ref, *refs, nblk):
    xs = _split(x_ref[...])
    w_refs, b_refs, o_ref = refs[:nblk], refs[nblk:2 * nblk], refs[2 * nblk]
    wh, wl = _split(jnp.concatenate([r[0] for r in w_refs], axis=0))
    acc = _dot_nt(xs[0], wh) + (_dot_nt(xs[0], wl) + _dot_nt(xs[1], wh))
    o_ref[...] = acc + jnp.concatenate([r[0] for r in b_refs], axis=1)


def _small_proj(x, wt, b, layer, row_blocks):
    M, K = x.shape
    depth, N = b.shape
    bm = _tile(M, 256)
    nblk = len(row_blocks)
    wspec = lambda c: pl.BlockSpec((1, _LANE, K), lambda i: (layer, c, 0))
    bspec = lambda c: pl.BlockSpec((1, 1, _LANE), lambda i: (layer, 0, c))
    return pl.pallas_call(
        functools.partial(_small_proj_kernel, nblk=nblk),
        grid=(M // bm,),
        in_specs=[pl.BlockSpec((bm, K), lambda i: (i, 0))] + [wspec(c) for c in row_blocks] + [bspec(c) for c in row_blocks],
        out_specs=pl.BlockSpec((bm, nblk * _LANE), lambda i: (i, 0)),
        out_shape=jax.ShapeDtypeStruct((M, nblk * _LANE), _F32),
        compiler_params=_params("parallel"),
        name="small_proj",
    )(x, *([wt] * nblk), *([b.reshape(depth, 1, N)] * nblk))


def _ln_kernel(z_ref, g_ref, b_ref, o_ref, ob_ref):
    z = z_ref[...]
    mu = jnp.mean(z, axis=-1, keepdims=True)
    d = z - mu
    var = jnp.mean(d * d, axis=-1, keepdims=True)
    y = d * lax.rsqrt(var + _LN_EPS) * g_ref[...] + b_ref[...]
    o_ref[...] = y
    ob_ref[...] = y.astype(_BF16)


def _layernorm(z, g, b):
    M, D = z.shape
    bm = _tile(M, 256)
    return pl.pallas_call(
        _ln_kernel,
        grid=(M // bm,),
        in_specs=[pl.BlockSpec((bm, D), lambda i: (i, 0)),
                  pl.BlockSpec((1, D), lambda i: (0, 0)),
                  pl.BlockSpec((1, D), lambda i: (0, 0))],
        out_specs=[pl.BlockSpec((bm, D), lambda i: (i, 0)),
                   pl.BlockSpec((bm, D), lambda i: (i, 0))],
        out_shape=[jax.ShapeDtypeStruct((M, D), _F32), jax.ShapeDtypeStruct((M, D), _BF16)],
        compiler_params=_params("parallel"),
        name="layernorm",
    )(z, g.reshape(1, D), b.reshape(1, D))


def _merge_kernel(a_ref, b_ref, c_ref, w_ref, g_ref, o_ref, acc_ref):
    br = pl.program_id(2)

    def contrib(x_ref):
        return jax.nn.sigmoid(g_ref[...]) * jnp.dot(x_ref[...], w_ref[0, 0], preferred_element_type=_F32)

    @pl.when(br == 0)
    def _():
        acc_ref[...] = contrib(a_ref)

    @pl.when(br == 1)
    def _():
        acc_ref[...] += contrib(b_ref)

    @pl.when(br == 2)
    def _():
        o_ref[...] = (acc_ref[...] + contrib(c_ref)).astype(o_ref.dtype)


def _merge(oa, ob, oc, w_branch, layer, h, gate_col0):
    M, BW = oa.shape
    D = w_branch.shape[3]
    bm, bn = _tile(M, 1024), _tile(D, 1024)
    nj = D // bn
    assert gate_col0 % bn == 0
    g0 = gate_col0 // bn
    xspec = pl.BlockSpec((bm, BW), lambda i, j, n: (i, 0))
    return pl.pallas_call(
        _merge_kernel,
        grid=(M // bm, nj, 3),
        in_specs=[xspec, xspec, xspec,
                  pl.BlockSpec((1, 1, BW, bn), lambda i, j, n: (layer, n, 0, j)),
                  pl.BlockSpec((bm, bn), lambda i, j, n: (i, g0 + n * nj + j))],
        out_specs=pl.BlockSpec((bm, bn), lambda i, j, n: (i, j)),
        out_shape=jax.ShapeDtypeStruct((M, D), _BF16),
        scratch_shapes=[pltpu.VMEM((bm, bn), _F32)],
        compiler_params=_params("parallel", "parallel", "arbitrary"),
        name="branch_merge",
    )(oa, ob, oc, w_branch, h)


def _ffn_up_kernel(x_ref, wg_ref, wv_ref, o_ref):
    x = x_ref[...]
    g = jnp.dot(x, _bf(wg_ref[0]), preferred_element_type=_F32)
    v = jnp.dot(x, _bf(wv_ref[0]), preferred_element_type=_F32)
    o_ref[...] = (g * jax.nn.sigmoid(g) * v).astype(o_ref.dtype)


def _ffn_up(x, w_up, layer):
    M, K = x.shape
    F = w_up.shape[2] // 2
    bm, bn = _tile(M, 1024), _tile(F, 256)
    nj = F // bn
    return pl.pallas_call(
        _ffn_up_kernel,
        grid=(M // bm, nj),
        in_specs=[pl.BlockSpec((bm, K), lambda i, j: (i, 0)),
                  pl.BlockSpec((1, K, bn), lambda i, j: (layer, 0, j)),
                  pl.BlockSpec((1, K, bn), lambda i, j: (layer, 0, nj + j))],
        out_specs=pl.BlockSpec((bm, bn), lambda i, j: (i, j)),
        out_shape=jax.ShapeDtypeStruct((M, F), _BF16),
        compiler_params=_params("parallel", "parallel"),
        name="ffn_up",
    )(x, w_up, w_up)


def _transpose_hi(x):
    n = x.shape[1]
    eye = (lax.broadcasted_iota(jnp.int32, (n, n), 0) == lax.broadcasted_iota(jnp.int32, (n, n), 1)).astype(_F32)
    return lax.dot_general(eye, x, (((1,), (1,)), ((), ())), precision=_HI, preferred_element_type=_F32)


def _silu(x):
    return x * jax.nn.sigmoid(x)


def _softplus(x):
    return jnp.maximum(x, 0.0) + jnp.log1p(jnp.exp(-jnp.abs(x)))


def _chunk_masks():
    L = _CHUNK
    row = lax.broadcasted_iota(jnp.int32, (L, L), 0)
    col = lax.broadcasted_iota(jnp.int32, (L, L), 1)
    return row, col


def _ln_head(o, g):
    mu = jnp.mean(o, axis=-1, keepdims=True)
    d = o - mu
    var = jnp.mean(d * d, axis=-1, keepdims=True)
    return d * lax.rsqrt(var + _LN_EPS) * g


def _unit_lower_inverses(As, row, col, left):
    def bdiag(ys):
        return tuple(jnp.concatenate([jnp.where(left, y, jnp.zeros_like(y)),
                                      jnp.where(left, jnp.zeros_like(y), y)], axis=0) for y in ys)

    same16 = (row // 16) == (col // 16)
    same32 = (row // 32) == (col // 32)
    eye = (row == col).astype(_F32)
    D = [jnp.where(same16, A, 0.0) for A in As]
    Ds = [_split(d) for d in D]
    D2s = [_split(_mm3(d, bdiag(d))) for d in Ds]
    D2b = [bdiag(d2) for d2 in D2s]
    P = [eye - d for d in D]
    P = [p + _mm3(_split(p), d2) for p, d2 in zip(P, D2b)]
    D4s = [_split(_mm3(d2, d2b)) for d2, d2b in zip(D2s, D2b)]
    D4b = [bdiag(d4) for d4 in D4s]
    P = [p + _mm3(_split(p), d4) for p, d4 in zip(P, D4b)]
    D8b = [bdiag(_split(_mm3(d4, d4b))) for d4, d4b in zip(D4s, D4b)]
    P = [p + _mm3(_split(p), d8) for p, d8 in zip(P, D8b)]
    for off_mask in (jnp.logical_and(same32, jnp.logical_not(same16)), jnp.logical_not(same32)):
        Es = [_split(jnp.where(off_mask, A, 0.0)) for A in As]
        Ps = [_split(p) for p in P]
        Qb = [bdiag(_split(_mm3(e, bdiag(ps)))) for e, ps in zip(Es, Ps)]
        P = [p - _mm3(ps, q) for p, ps, q in zip(P, Ps, Qb)]
    return P


class _Steps:
    def __init__(self, bp, tp, bs, ts):
        assert tp % _CHUNK == 0 and ts % _CHUNK == 0
        self.bp, self.bs = bp, bs
        self.ncp, self.ncs = tp // _CHUNK, ts // _CHUNK
        self.np_steps = bp * self.ncp
        self.n = self.np_steps + bs * self.ncs

    def in_prompt(self, i):
        return i < self.np_steps

    def _sample_step(self, i):
        return jnp.maximum(i - self.np_steps, 0)

    def seq(self, i):
        return jnp.where(self.in_prompt(i), i // self.ncp, self.bp + self._sample_step(i) // self.ncs)

    def sample_seq(self, i):
        return self._sample_step(i) // self.ncs

    def prompt_seq(self, i):
        return jnp.minimum(i // self.ncp, self.bp - 1)

    def chunk(self, i):
        return jnp.where(self.in_prompt(i), i % self.ncp, self._sample_step(i) % self.ncs)

    def pos_block(self, i):
        return jnp.where(self.in_prompt(i), self.chunk(i), self.ncp + self.chunk(i))

    def first_last(self, i):
        c = self.chunk(i)
        return c == 0, jnp.where(self.in_prompt(i), c == self.ncp - 1, c == self.ncs - 1)


def _gdn_kernel(q_ref, k_ref, v_ref, z_ref, gs_ref, cwq_ref, cwk_ref, cwv_ref, gp_ref, ng_ref,
                s0_ref, c0q_ref, c0k_ref, c0v_ref, *rest, hpg, dk, dv, lane0, steps):
    o_ref, sp_ref, ss_ref, s_scr, bq, bk, bv = rest[-7:]
    L, P = _CHUNK, _SUBLANE
    i = pl.program_id(1)
    first, last = steps.first_last(i)
    in_p = steps.in_prompt(i)
    nc = c0q_ref.shape[2]

    @pl.when(jnp.logical_and(first, in_p))
    def _():
        s_scr[...] = jnp.zeros(s_scr.shape, _F32)
        for buf in (bq, bk, bv):
            buf[0:P, :] = jnp.zeros((P, buf.shape[1]), _F32)

    @pl.when(jnp.logical_and(first, jnp.logical_not(in_p)))
    def _():
        s_scr[...] = s0_ref[0, 0]
        bq[P - nc:P, :] = c0q_ref[0, 0]
        bk[P - nc:P, :] = c0k_ref[0, 0]
        bv[P - nc:P, :] = c0v_ref[0, 0]

    bq[P:P + L, :] = q_ref[...]
    bk[P:P + L, :] = k_ref[...]
    bv[P:P + L, :] = v_ref[...]

    row, col = _chunk_masks()
    tril = (row >= col).astype(_F32)
    gs = gs_ref[...]
    gp = gp_ref[0]
    beta_blk = jax.nn.sigmoid(gs)
    g_blk = -jnp.exp(gp[0:1, :]) * _softplus(gs + gp[1:2, :])
    G_blk = _dot_hi(tril, g_blk)
    GT2 = _transpose_hi(jnp.concatenate([G_blk, G_blk], axis=0))
    ng = ng_ref[...]
    heads = range(hpg)

    def conv(buf, cw_ref, cs):
        out = buf[P - nc:P - nc + L, cs] * cw_ref[0:1, cs]
        for j in range(1, nc + 1):
            out = out + buf[P - nc + j:P - nc + j + L, cs] * cw_ref[j:j + 1, cs]
        return out

    def l2norm(x):
        return x * lax.rsqrt(jnp.sum(x * x, axis=-1, keepdims=True) + 1e-6)

    ks = [slice(h * dk, (h + 1) * dk) for h in heads]
    vs = [slice(h * dv, (h + 1) * dv) for h in heads]
    q = [l2norm(_silu(conv(bq, cwq_ref, ks[h]))) * dk ** -0.5 for h in heads]
    k = [l2norm(_silu(conv(bk, cwk_ref, ks[h]))) for h in heads]
    v = [_silu(conv(bv, cwv_ref, vs[h])) for h in heads]
    la = lane0 + hpg
    beta = [beta_blk[:, lane0 + h:lane0 + h + 1] for h in heads]
    Gc = [G_blk[:, la + h:la + h + 1] for h in heads]
    eG = [jnp.exp(Gc[h]) for h in heads]
    kb = [_bf(k[h]) for h in heads]
    qb = [_bf(q[h]) for h in heads]

    pairs = range(hpg // 2)
    lane = lax.broadcasted_iota(jnp.int32, (L, 2 * L), 1)
    left = lane < L
    rowp = lax.broadcasted_iota(jnp.int32, (L, 2 * L), 0)
    colp = jnp.where(left, lane, lane - L)
    inclp, strictp = rowp >= colp, rowp > colp

    def side(a0, a1):
        return jnp.where(left, a0, a1)

    def bdiag(y0, y1):
        return jnp.concatenate([jnp.concatenate([y0, jnp.zeros_like(y1)], axis=1),
                                jnp.concatenate([jnp.zeros_like(y0), y1], axis=1)], axis=0)

    Gcp = [side(Gc[2 * p], Gc[2 * p + 1]) for p in pairs]
    Grp = [side(GT2[la + 2 * p:la + 2 * p + 1, :], GT2[la + 2 * p + 1:la + 2 * p + 2, :]) for p in pairs]
    betap = [side(beta[2 * p], beta[2 * p + 1]) for p in pairs]
    gam = [jnp.where(inclp, jnp.exp(jnp.where(inclp, Gcp[p] - Grp[p], 0.0)), 0.0) for p in pairs]
    kq = [_dot_nt(jnp.concatenate([jnp.concatenate([kb[2 * p], kb[2 * p + 1]], axis=1),
                                   jnp.concatenate([qb[2 * p], qb[2 * p + 1]], axis=1)], axis=0),
                  bdiag(kb[2 * p], kb[2 * p + 1])) for p in pairs]
    A = [jnp.where(strictp, betap[p] * kq[p][0:L] * gam[p], 0.0) for p in pairs]
    qk = [_bf(kq[p][L:2 * L] * gam[p]) for p in pairs]
    T = _unit_lower_inverses(A, rowp, colp, left)
    rhs = [_split(jnp.concatenate([beta[h] * v[h], (beta[h] * eG[h]) * k[h]], axis=1)) for h in heads]
    UWp = [_mm3(_split(T[p]), tuple(bdiag(rhs[2 * p][t], rhs[2 * p + 1][t]) for t in range(2))) for p in pairs]
    UW = [UWp[h // 2][:, (h % 2) * (dv + dk):(h % 2 + 1) * (dv + dk)] for h in heads]
    G_last = [Gc[h][L - 1:L, :] for h in heads]
    S = [s_scr[h] for h in heads]
    Sb = [_bf(S[h]) for h in heads]
    WQ = [_dot(jnp.concatenate([_bf(UW[h][:, dv:]), _bf(q[h] * eG[h])], axis=0), Sb[h]) for h in heads]
    db = [_bf(UW[h][:, 0:dv] - WQ[h][0:L]) for h in heads]
    qd = [_dot(qk[p], bdiag(db[2 * p], db[2 * p + 1])) for p in pairs]
    o = [WQ[h][L:2 * L] + qd[h // 2][:, (h % 2) * dv:(h % 2 + 1) * dv] for h in heads]
    for h in heads:
        k_dec = k[h] * jnp.exp(G_last[h] - Gc[h])
        s_scr[h] = jnp.exp(G_last[h]) * S[h] + _dot_tn(_bf(k_dec), db[h])
    for h in heads:
        on = o[h] * lax.rsqrt(jnp.mean(o[h] * o[h], axis=-1, keepdims=True) + 1e-6) * ng
        o_ref[:, vs[h]] = (on * _silu(z_ref[:, vs[h]])).astype(o_ref.dtype)

    bq[0:P, :] = bq[L:L + P, :]
    bk[0:P, :] = bk[L:L + P, :]
    bv[0:P, :] = bv[L:L + P, :]

    @pl.when(jnp.logical_and(last, in_p))
    def _():
        sp_ref[0, 0] = s_scr[...]

    @pl.when(jnp.logical_and(last, jnp.logical_not(in_p)))
    def _():
        ss_ref[0, 0] = s_scr[...]


def _state_outputs(prev, shapes_p, shapes_s, n_in, n_lead_out):
    shapes = [jax.ShapeDtypeStruct(s, _F32) for pair in zip(shapes_p, shapes_s) for s in pair]
    if prev is None:
        return shapes, [], {}, ()
    specs = [pl.BlockSpec(memory_space=pl.ANY)] * len(prev)
    aliases = {n_in + j: n_lead_out + j for j in range(len(prev))}
    return shapes, specs, aliases, tuple(prev)


def _gdn(h, hs, gate_block, lane0, conv_w, gp, norm_g, s_in, conv_in, prev, layer, steps):
    ntok = h.shape[0]
    depth, _, H, dk, dv = s_in.shape
    nc = conv_in.shape[2]
    assert dk == dv and nc < _SUBLANE and conv_w.shape[0] == nc + 1 and lane0 + 2 * H <= _LANE and H % 2 == 0
    hpg, ng_ = H, 1
    L, P = _CHUNK, _SUBLANE
    wq = hpg * dk
    sseq, pseq = steps.sample_seq, steps.prompt_seq
    tok = lambda off: pl.BlockSpec((L, wq), lambda g, i: (i, off + g))
    cw = lambda off: pl.BlockSpec((nc + 1, wq), lambda g, i: (0, off + g))
    c0 = lambda off: pl.BlockSpec((1, 1, nc, wq), lambda g, i: (layer, sseq(i), 0, off + g))
    in_specs = [tok(0), tok(ng_), tok(2 * ng_), tok(3 * ng_),
                pl.BlockSpec((L, _LANE), lambda g, i: (i, gate_block)),
                cw(0), cw(ng_), cw(2 * ng_),
                pl.BlockSpec((1, 2, _LANE), lambda g, i: (layer, 0, 0)),
                pl.BlockSpec((1, dv), lambda g, i: (0, 0)),
                pl.BlockSpec((1, 1, hpg, dk, dv), lambda g, i: (layer, sseq(i), g, 0, 0)),
                c0(0), c0(ng_), c0(2 * ng_)]
    shapes, pspecs, aliases, pargs = _state_outputs(
        prev, [(depth, steps.bp, H, dk, dv)], [(depth, steps.bs, H, dk, dv)], len(in_specs), 1)
    out = pl.pallas_call(
        functools.partial(_gdn_kernel, hpg=hpg, dk=dk, dv=dv, lane0=lane0, steps=steps),
        grid=(ng_, steps.n),
        in_specs=in_specs + pspecs,
        out_specs=[pl.BlockSpec((L, hpg * dv), lambda g, i: (i, g)),
                   pl.BlockSpec((1, 1, hpg, dk, dv), lambda g, i: (layer, pseq(i), g, 0, 0)),
                   pl.BlockSpec((1, 1, hpg, dk, dv), lambda g, i: (layer, sseq(i), g, 0, 0))],
        out_shape=[jax.ShapeDtypeStruct((ntok, H * dv), _BF16)] + shapes,
        input_output_aliases=aliases,
        scratch_shapes=[pltpu.VMEM((hpg, dk, dv), _F32)] + [pltpu.VMEM((P + L, wq), _F32)] * 3,
        compiler_params=_params("parallel", "arbitrary"),
        name="gdn_mixer",
    )(h, h, h, h, hs, conv_w, conv_w, conv_w, gp, norm_g.reshape(1, dv), s_in, conv_in, conv_in, conv_in, *pargs)
    return out[0], out[1:]


def _mlstm_kernel(q_ref, k_ref, v_ref, og_ref, gs_ref, fb_ref, ng_ref, c0_ref, n0_ref, m0_ref,
                  *rest, H, dk, dv, lane0, steps):
    o_ref, cp_ref, cs_ref, np_ref, ns_ref, mp_ref, ms_ref, c_scr, n_scr, m_scr = rest[-10:]
    L, P = _CHUNK, _SUBLANE
    i = pl.program_id(0)
    first, last = steps.first_last(i)
    in_p = steps.in_prompt(i)

    @pl.when(jnp.logical_and(first, in_p))
    def _():
        c_scr[...] = jnp.zeros(c_scr.shape, _F32)
        n_scr[...] = jnp.zeros(n_scr.shape, _F32)
        m_scr[...] = jnp.zeros(m_scr.shape, _F32)

    @pl.when(jnp.logical_and(first, jnp.logical_not(in_p)))
    def _():
        c_scr[...] = c0_ref[0, 0]
        n_scr[...] = n0_ref[0, 0]
        m0 = m0_ref[0, 0]
        for h in range(H):
            m_scr[h] = jnp.broadcast_to(m0[:, h:h + 1], (P, _LANE))

    row, col = _chunk_masks()
    incl = row >= col
    tril = incl.astype(_F32)
    gs = gs_ref[...]
    li, lf = lane0, lane0 + H
    lf_blk = -_softplus(-(gs + fb_ref[0]))
    F_blk = _dot_hi(tril, lf_blk)
    FT = _transpose_hi(F_blk)
    gsT = _transpose_hi(gs)

    heads = range(H)
    ks = [slice(h * dk, (h + 1) * dk) for h in heads]
    vs = [slice(h * dv, (h + 1) * dv) for h in heads]
    q = [q_ref[:, ks[h]] * dk ** -0.5 for h in heads]
    k = [k_ref[:, ks[h]] for h in heads]
    qb = [_bf(q[h]) for h in heads]
    kb = [_bf(k[h]) for h in heads]
    vb = [_bf(v_ref[:, vs[h]]) for h in heads]
    qk = [_dot_nt(qb[h], kb[h]) for h in heads]
    C = [c_scr[h] for h in heads]
    qC = [_dot(qb[h], _bf(C[h])) for h in heads]
    igc = [gs[:, li + h:li + h + 1] for h in heads]
    Fc = [F_blk[:, lf + h:lf + h + 1] for h in heads]
    logD = [jnp.where(incl, Fc[h] - FT[lf + h:lf + h + 1, :] + gsT[li + h:li + h + 1, :], -jnp.inf) for h in heads]
    m_prev = [m_scr[h][0:1, 0:1] for h in heads]
    m_t = [jnp.maximum(Fc[h] + m_prev[h], jnp.max(logD[h], axis=-1, keepdims=True)) for h in heads]
    Sm = [jnp.exp(logD[h] - m_t[h]) * qk[h] for h in heads]
    Sv = [_dot(_bf(Sm[h]), vb[h]) for h in heads]
    inter = [jnp.exp(Fc[h] + m_prev[h] - m_t[h]) for h in heads]
    n = [n_scr[h:h + 1, :] for h in heads]
    m_last = [m_t[h][L - 1:L, :] for h in heads]
    wk = [jnp.exp(Fc[h][L - 1:L, :] - Fc[h] + igc[h] - m_last[h]) * k[h] for h in heads]
    kv = [_dot_tn(_bf(wk[h]), vb[h]) for h in heads]
    for h in heads:
        dec = inter[h][L - 1:L, :]
        c_scr[h] = dec * C[h] + kv[h]
        n_scr[h:h + 1, :] = dec * n[h] + jnp.sum(wk[h], axis=0, keepdims=True)
        m_scr[h] = jnp.broadcast_to(m_last[h], (P, _LANE))
    for h in heads:
        num = inter[h] * qC[h] + Sv[h]
        den = inter[h] * jnp.sum(q[h] * n[h], axis=-1, keepdims=True) + jnp.sum(Sm[h], axis=-1, keepdims=True)
        hh = num / jnp.maximum(jnp.abs(den), jnp.exp(-m_t[h]))
        o_ref[:, vs[h]] = (_ln_head(hh, ng_ref[:, vs[h]]) * jax.nn.sigmoid(og_ref[:, vs[h]])).astype(o_ref.dtype)

    def write_state(c_ref, n_ref, m_ref):
        c_ref[0, 0] = c_scr[...]
        n_ref[0, 0] = n_scr[...]
        lane = lax.broadcasted_iota(jnp.int32, (1, _LANE), 1)
        m_row = jnp.zeros((1, _LANE), _F32)
        for h in range(H):
            m_row = jnp.where(lane == h, m_scr[h][0:1, :], m_row)
        m_ref[0, 0] = m_row

    @pl.when(jnp.logical_and(last, in_p))
    def _():
        write_state(cp_ref, np_ref, mp_ref)

    @pl.when(jnp.logical_and(last, jnp.logical_not(in_p)))
    def _():
        write_state(cs_ref, ns_ref, ms_ref)


def _mlstm(h, col0, hs, gate_block, lane0, f_bias, norm_g, c_in, n_in, m_in, prev, layer, steps):
    ntok = h.shape[0]
    depth, _, H, dk, dv = c_in.shape
    assert dv % dk == 0 and 2 % (dv // dk) == 0 and lane0 + 2 * H <= _LANE and col0 % (H * dv) == 0
    L = _CHUNK
    bp, bs = steps.bp, steps.bs
    jq, jv = col0 // (H * dk), col0 // (H * dv) + 2 * dk // dv
    tq = lambda j: pl.BlockSpec((L, H * dk), lambda i: (i, jq + j))
    tv = lambda j: pl.BlockSpec((L, H * dv), lambda i: (i, jv + j))
    sseq, pseq = steps.sample_seq, steps.prompt_seq
    in_specs = [tq(0), tq(1), tv(0), tv(1),
                pl.BlockSpec((L, _LANE), lambda i: (i, gate_block)),
                pl.BlockSpec((1, 1, _LANE), lambda i: (layer, 0, 0)),
                pl.BlockSpec((1, H * dv), lambda i: (0, 0)),
                pl.BlockSpec((1, 1, H, dk, dv), lambda i: (layer, sseq(i), 0, 0, 0)),
                pl.BlockSpec((1, 1, H, dk), lambda i: (layer, sseq(i), 0, 0)),
                pl.BlockSpec((1, 1, 1, H), lambda i: (layer, sseq(i), 0, 0))]
    tails = [(H, dk, dv), (H, dk), (1, _LANE)]
    shapes, pspecs, aliases, pargs = _state_outputs(
        prev, [(depth, bp) + t for t in tails], [(depth, bs) + t for t in tails], len(in_specs), 1)
    state_specs = []
    for t in tails:
        zeros = (0,) * len(t)
        state_specs.append(pl.BlockSpec((1, 1) + t, lambda i, z=zeros: (layer, pseq(i)) + z))
        state_specs.append(pl.BlockSpec((1, 1) + t, lambda i, z=zeros: (layer, sseq(i)) + z))
    out = pl.pallas_call(
        functools.partial(_mlstm_kernel, H=H, dk=dk, dv=dv, lane0=lane0, steps=steps),
        grid=(steps.n,),
        in_specs=in_specs + pspecs,
        out_specs=[pl.BlockSpec((L, H * dv), lambda i: (i, 0))] + state_specs,
        out_shape=[jax.ShapeDtypeStruct((ntok, H * dv), _BF16)] + shapes,
        input_output_aliases=aliases,
        scratch_shapes=[pltpu.VMEM((H, dk, dv), _F32), pltpu.VMEM((H, dk), _F32),
                        pltpu.VMEM((H, _SUBLANE, _LANE), _F32)],
        compiler_params=_params("arbitrary"),
        name="mlstm_mixer",
    )(h, h, h, h, hs, f_bias, norm_g.reshape(1, H * dv), c_in, n_in,
      m_in.reshape(m_in.shape[0], m_in.shape[1], 1, H), *pargs)
    return out[0], out[1:]


def _ret_kernel(lg_ref, q_ref, k_ref, v_ref, og_ref, cos_ref, sin_ref, ng_ref, s0_ref,
                *rest, H, dk, dv, steps):
    o_ref, sp_ref, ss_ref, s_scr = rest[-4:]
    L = _CHUNK
    i = pl.program_id(0)
    first, last = steps.first_last(i)
    in_p = steps.in_prompt(i)

    @pl.when(jnp.logical_and(first, in_p))
    def _():
        s_scr[...] = jnp.zeros(s_scr.shape, _F32)

    @pl.when(jnp.logical_and(first, jnp.logical_not(in_p)))
    def _():
        s_scr[...] = s0_ref[0, 0]

    row, col = _chunk_masks()
    incl = row >= col
    tdiff = (row - col).astype(_F32)
    jc = lax.broadcasted_iota(jnp.int32, (L, 1), 0).astype(_F32)
    cos, sin = cos_ref[...], sin_ref[...]

    def rotary(x):
        return x * cos + pltpu.roll(x, dk // 2, 1) * sin

    heads = range(H)
    ks = [slice(h * dk, (h + 1) * dk) for h in heads]
    vs = [slice(h * dv, (h + 1) * dv) for h in heads]
    lg = [lg_ref[h] for h in heads]
    q = [rotary(q_ref[:, ks[h]]) for h in heads]
    k = [rotary(k_ref[:, ks[h]]) * dk ** -0.5 for h in heads]
    vb = [_bf(v_ref[:, vs[h]]) for h in heads]
    qk = [_dot_nt(_bf(q[h]), _bf(k[h])) for h in heads]
    S = [s_scr[h] for h in heads]
    qS = [_dot(_bf(q[h] * jnp.exp(lg[h] * (jc + 1.0))), _bf(S[h])) for h in heads]
    kv = [_dot_tn(_bf(k[h] * jnp.exp(lg[h] * (L - 1.0 - jc))), vb[h]) for h in heads]
    D = [jnp.where(incl, jnp.exp(jnp.where(incl, lg[h] * tdiff, 0.0)), 0.0) for h in heads]
    o = [_dot(_bf(qk[h] * D[h]), vb[h]) + qS[h] for h in heads]
    for h in heads:
        s_scr[h] = jnp.exp(lg[h] * jnp.full((1, 1), float(L), _F32)) * S[h] + kv[h]
    for h in heads:
        o_ref[:, vs[h]] = (_ln_head(o[h], ng_ref[:, vs[h]]) * _silu(og_ref[:, vs[h]])).astype(o_ref.dtype)

    @pl.when(jnp.logical_and(last, in_p))
    def _():
        sp_ref[0, 0] = s_scr[...]

    @pl.when(jnp.logical_and(last, jnp.logical_not(in_p)))
    def _():
        ss_ref[0, 0] = s_scr[...]


def _retention(h, col0, lg, cos_tab, sin_tab, norm_g, s_in, prev, layer, steps):
    ntok = h.shape[0]
    depth, _, H, dk, dv = s_in.shape
    assert dv % dk == 0 and 2 % (dv // dk) == 0 and col0 % (H * dv) == 0
    L = _CHUNK
    jq, jv = col0 // (H * dk), col0 // (H * dv) + 2 * dk // dv
    tq = lambda j: pl.BlockSpec((L, H * dk), lambda i: (i, jq + j))
    tv = lambda j: pl.BlockSpec((L, H * dv), lambda i: (i, jv + j))
    tab = pl.BlockSpec((L, dk), lambda i: (steps.pos_block(i), 0))
    sseq, pseq = steps.sample_seq, steps.prompt_seq
    in_specs = [pl.BlockSpec(memory_space=pltpu.SMEM),
                tq(0), tq(1), tv(0), tv(1), tab, tab,
                pl.BlockSpec((1, H * dv), lambda i: (0, 0)),
                pl.BlockSpec((1, 1, H, dk, dv), lambda i: (layer, sseq(i), 0, 0, 0))]
    shapes, pspecs, aliases, pargs = _state_outputs(
        prev, [(depth, steps.bp, H, dk, dv)], [(depth, steps.bs, H, dk, dv)], len(in_specs), 1)
    out = pl.pallas_call(
        functools.partial(_ret_kernel, H=H, dk=dk, dv=dv, steps=steps),
        grid=(steps.n,),
        in_specs=in_specs + pspecs,
        out_specs=[pl.BlockSpec((L, H * dv), lambda i: (i, 0)),
                   pl.BlockSpec((1, 1, H, dk, dv), lambda i: (layer, pseq(i), 0, 0, 0)),
                   pl.BlockSpec((1, 1, H, dk, dv), lambda i: (layer, sseq(i), 0, 0, 0))],
        out_shape=[jax.ShapeDtypeStruct((ntok, H * dv), _BF16)] + shapes,
        input_output_aliases=aliases,
        scratch_shapes=[pltpu.VMEM((H, dk, dv), _F32)],
        compiler_params=_params("arbitrary"),
        name="retention_mixer",
    )(lg, h, h, h, h, cos_tab, sin_tab, norm_g.reshape(1, H * dv), s_in, *pargs)
    return out[0], out[1:]


def _rotary_tables(tp, ts, dk):
    half = dk // 2
    freq = _ROPE_BASE ** (-jnp.arange(half, dtype=_F32) / half)
    pos = jnp.concatenate([0.0 + jnp.arange(tp, dtype=_F32), float(_PAST_LEN) + jnp.arange(ts, dtype=_F32)])
    ang = pos[:, None] * freq[None, :]
    cos, sin = jnp.cos(ang), jnp.sin(ang)
    return jnp.concatenate([cos, cos], axis=1), jnp.concatenate([-sin, sin], axis=1)


def kernel(x_prompt, x_sample, state_gdn_S, state_gdn_conv, state_mlstm_C, state_mlstm_n, state_mlstm_m, state_ret_S, w_in, b_in, gdn_conv_w, gdn_A_log, gdn_dt_bias, gdn_norm_g, mlstm_f_bias, mlstm_norm_g, ret_norm_g, w_branch, w_out, ln1_g, ln1_b, ln2_g, ln2_b, w_ffn_up, w_ffn_down):
    bp, tp, D = x_prompt.shape
    bs, ts, _ = x_sample.shape
    depth = w_in.shape[0]
    alpha = (2 * depth) ** 0.25
    Hg, gdk, gdv = state_gdn_S.shape[2:]
    Hm, mdk, mdv = state_mlstm_C.shape[2:]
    Hr, rdk, rdv = state_ret_S.shape[2:]
    nconv = state_gdn_conv.shape[2]
    steps = _Steps(bp, tp, bs, ts)
    np_tok = bp * tp

    widths = (2 * Hg * gdk + Hg * gdv, Hg * gdv, Hg, Hg,
              Hm * mdk, Hm * mdk, Hm * mdv, Hm * mdv, Hm, Hm,
              Hr * rdk, Hr * rdk, Hr * rdv, Hr * rdv, 3 * D)
    offs = [0]
    for w_ in widths:
        offs.append(offs[-1] + w_)
    o_qkv, _, o_b, o_a, o_mq, _, _, _, o_mi, o_mf, o_rq, _, _, _, _, o_end = offs
    assert o_end == w_in.shape[2]
    qkv_w = widths[0]
    wide = ((o_qkv, o_b), (o_mq, o_mi), (o_rq, o_end))
    ml_col0 = o_b - o_qkv
    ret_col0 = ml_col0 + (o_mi - o_mq)
    gate_col0 = ret_col0 + sum(widths[10:14])

    gdn_blk, gdn_lane = o_b // _LANE, o_b % _LANE
    ml_blk, ml_lane = o_mi // _LANE, o_mi % _LANE
    assert o_a == o_b + Hg and o_mf == o_mi + Hm

    def wide_cols(a):
        return jnp.concatenate([a[..., lo:hi] for lo, hi in wide], axis=-1)

    cos_tab, sin_tab = _rotary_tables(tp, ts, rdk)
    lg = jnp.log(1.0 - 2.0 ** (-5.0 - jnp.arange(Hr, dtype=_F32)))

    w_in_t = jnp.swapaxes(w_in, 1, 2)
    w_wide, b_wide = _regroup_weights(w_in_t, wide), wide_cols(b_in)
    w_br, w_o, w_dn = _bf(w_branch), _bf(w_out), _bf(w_ffn_down)
    la = gdn_lane + Hg
    gp = jnp.zeros((depth, 2, _LANE), _F32)
    gp = gp.at[:, 0, la:la + Hg].set(gdn_A_log).at[:, 1, la:la + Hg].set(gdn_dt_bias)
    lf = ml_lane + Hm
    fb = jnp.zeros((depth, 1, _LANE), _F32).at[:, 0, lf:lf + Hm].set(mlstm_f_bias)

    x = jnp.concatenate([x_prompt.reshape(np_tok, D), x_sample.reshape(bs * ts, D)], axis=0)
    xb = x.astype(_BF16)
    g_st = m_st = r_st = None
    conv_p, conv_s = [], []
    for l in range(depth):
        h = _matmul(xb, w_wide, l, bias=b_wide[l], name="in_proj")
        hs = _small_proj(x, w_in_t, b_in, l, (gdn_blk, ml_blk))
        oa, g_st = _gdn(h, hs, 0, gdn_lane, gdn_conv_w[l], gp, gdn_norm_g[l], state_gdn_S, state_gdn_conv, g_st, l, steps)
        tails = h.reshape(steps.n, _CHUNK, -1)[:, _CHUNK - nconv:, :qkv_w]
        conv_p.append(tails[steps.ncp - 1:steps.np_steps:steps.ncp])
        conv_s.append(tails[steps.np_steps + steps.ncs - 1::steps.ncs])
        ob, m_st = _mlstm(h, ml_col0, hs, 1, ml_lane, fb, mlstm_norm_g[l],
                          state_mlstm_C, state_mlstm_n, state_mlstm_m, m_st, l, steps)
        oc, r_st = _retention(h, ret_col0, lg, cos_tab, sin_tab, ret_norm_g[l], state_ret_S, r_st, l, steps)

        merged = _merge(oa, ob, oc, w_br, l, h, gate_col0)
        z1 = _matmul(merged, w_o, l, res=x, alpha=alpha, name="out_proj")
        x, xb = _layernorm(z1, ln1_g[l], ln1_b[l])

        act = _ffn_up(xb, w_ffn_up, l)
        z2 = _matmul(act, w_dn, l, res=x, alpha=alpha, bm=512, bn=512, name="ffn_down")
        x, xb = _layernorm(z2, ln2_g[l], ln2_b[l])

    gS_p, gS_s = g_st
    mC_p, mC_s, mn_p, mn_s, mm_p, mm_s = m_st
    rS_p, rS_s = r_st
    yp = x[:np_tok].reshape(bp, tp, D)
    ys = x[np_tok:].reshape(bs, ts, D)
    return (yp, ys,
            gS_p, jnp.stack(conv_p), mC_p, mn_p, mm_p[:, :, 0, :Hm], rS_p,
            gS_s, jnp.stack(conv_s), mC_s, mn_s, mm_s[:, :, 0, :Hm], rS_s)
```

```python
import functools

import jax
import jax.numpy as jnp
from jax import lax
from jax.experimental import pallas as pl
from jax.experimental.pallas import tpu as pltpu

_F32 = jnp.float32
_BF16 = jnp.bfloat16
_HI = lax.Precision.HIGHEST

_CHUNK = 64
_PAST_LEN = 1024
_ROPE_BASE = 10000.0
_LN_EPS = 1e-5
_LANE = 128
_SUBLANE = 8
_VMEM_LIMIT = 56 * 1024 * 1024


def _tile(dim, pref, align=_LANE):
    best = None
    t = align
    while t <= min(dim, pref):
        if dim % t == 0:
            best = t
        t += align
    return dim if best is None else best


def _params(*sem):
    return pltpu.CompilerParams(dimension_semantics=sem, vmem_limit_bytes=_VMEM_LIMIT)


def _bf(x):
    return x.astype(_BF16)


def _dot(a, b):
    return jnp.dot(a, b, preferred_element_type=_F32)


def _dot_nt(a, b):
    return lax.dot_general(a, b, (((1,), (1,)), ((), ())), preferred_element_type=_F32)


def _dot_tn(a, b):
    return lax.dot_general(a, b, (((0,), (0,)), ((), ())), preferred_element_type=_F32)


def _dot_hi(a, b):
    return jnp.dot(a, b, precision=_HI, preferred_element_type=_F32)


def _split(x):
    hi = x.astype(_BF16)
    return hi, (x - hi.astype(_F32)).astype(_BF16)


def _mm3(a, b):
    return _dot(a[0], b[0]) + (_dot(a[0], b[1]) + _dot(a[1], b[0]))


def _mm_kernel(*refs, nk, has_bias, has_res, alpha):
    it = iter(refs)
    x_ref, w_ref = next(it), next(it)
    b_ref = next(it) if has_bias else None
    r_ref = next(it) if has_res else None
    o_ref = next(it)
    acc_ref = next(it) if nk > 1 else None

    def finish(v):
        if has_bias:
            v = v + b_ref[...]
        if has_res:
            v = alpha * r_ref[...] + v
        o_ref[...] = v.astype(o_ref.dtype)

    part = jnp.dot(x_ref[...], w_ref[0], preferred_element_type=_F32)
    if nk == 1:
        finish(part)
    else:
        k = pl.program_id(2)

        @pl.when(k == 0)
        def _():
            acc_ref[...] = part

        @pl.when(k > 0)
        def _():
            acc_ref[...] += part

        @pl.when(k == nk - 1)
        def _():
            finish(acc_ref[...])


def _matmul(x, w, layer, *, bias=None, res=None, alpha=1.0, out_dtype=_F32, bm=1024, bn=1024, bk=None, name):
    M, K = x.shape
    N = w.shape[2]
    bm, bn = _tile(M, bm), _tile(N, bn)
    bk = K if bk is None else _tile(K, bk)
    nk = K // bk
    in_specs = [pl.BlockSpec((bm, bk), lambda i, j, k: (i, k)),
                pl.BlockSpec((1, bk, bn), lambda i, j, k: (layer, k, j))]
    args = [x, w]
    if bias is not None:
        in_specs.append(pl.BlockSpec((1, bn), lambda i, j, k: (0, j)))
        args.append(bias.reshape(1, N).astype(_F32))
    if res is not None:
        in_specs.append(pl.BlockSpec((bm, bn), lambda i, j, k: (i, j)))
        args.append(res)
    return pl.pallas_call(
        functools.partial(_mm_kernel, nk=nk, has_bias=bias is not None, has_res=res is not None, alpha=alpha),
        grid=(M // bm, N // bn, nk),
        in_specs=in_specs,
        out_specs=pl.BlockSpec((bm, bn), lambda i, j, k: (i, j)),
        out_shape=jax.ShapeDtypeStruct((M, N), out_dtype),
        scratch_shapes=[pltpu.VMEM((bm, bn), _F32)] if nk > 1 else [],
        compiler_params=_params("parallel", "parallel", "arbitrary"),
        name=name,
    )(*args)


_REGROUP_EXTRA = 64


def _regroup_kernel(a_ref, b_ref, o_ref, *, regions):
    j = pl.program_id(2)

    def emit(lo, hi, shift):
        @pl.when(jnp.logical_and(j >= lo, j < hi))
        def _():
            if shift == 0:
                t = a_ref[0]
            else:
                t = jnp.concatenate([a_ref[0, shift:, :], b_ref[0, :shift, :]], axis=0)
            o_ref[0] = _bf(t.T)

    for lo, hi, shift in regions:
        emit(lo, hi, shift)


def _regroup_weights(wt, spans):
    depth, N, K = wt.shape
    starts, shifts, pos = [], [], 0
    for lo, hi in spans:
        starts.append(pos)
        shifts.append(lo - pos)
        pos += hi - lo
    assert all(0 <= s <= _REGROUP_EXTRA and s % _SUBLANE == 0 for s in shifts)
    bn = _LANE
    for t in range(_LANE, 512 + 1, _LANE):
        if all(v % t == 0 for v in starts + [pos]):
            bn = t
    bk = _tile(K, 2048)
    last_extra_block = (N - 1) // _REGROUP_EXTRA
    bounds = [s // bn for s in starts] + [pos // bn]
    regions = tuple((bounds[r], bounds[r + 1], shifts[r]) for r in range(len(spans)))
    return pl.pallas_call(
        functools.partial(_regroup_kernel, regions=regions),
        grid=(depth, K // bk, pos // bn),
        in_specs=[pl.BlockSpec((1, bn, bk), lambda l, i, j: (l, j, i)),
                  pl.BlockSpec((1, _REGROUP_EXTRA, bk),
                               lambda l, i, j: (l, jnp.minimum((j + 1) * (bn // _REGROUP_EXTRA), last_extra_block), i))],
        out_specs=pl.BlockSpec((1, bk, bn), lambda l, i, j: (l, i, j)),
        out_shape=jax.ShapeDtypeStruct((depth, K, pos), _BF16),
        compiler_params=_params("parallel", "parallel", "parallel"),
        name="regroup_w_in",
    )(wt, wt)


def _small_proj_kernel(x_ref, *refs, nblk):
    xs = _split(x_ref[...])
    w_refs, b_refs, o_ref = refs[:nblk], refs[nblk:2 * nblk], refs[2 * nblk]
    o_ref[...] = _gate_proj(xs, w_refs, b_refs)


def _small_proj(x, wt, b, layer, row_blocks):
    M, K = x.shape
    depth, N = b.shape
    bm = _tile(M, 256)
    nblk = len(row_blocks)
    wspec = lambda c: pl.BlockSpec((1, _LANE, K), lambda i: (layer, c, 0))
    bspec = lambda c: pl.BlockSpec((1, 1, _LANE), lambda i: (layer, 0, c))
    return pl.pallas_call(
        functools.partial(_small_proj_kernel, nblk=nblk),
        grid=(M // bm,),
        in_specs=[pl.BlockSpec((bm, K), lambda i: (i, 0))] + [wspec(c) for c in row_blocks] + [bspec(c) for c in row_blocks],
        out_specs=pl.BlockSpec((bm, nblk * _LANE), lambda i: (i, 0)),
        out_shape=jax.ShapeDtypeStruct((M, nblk * _LANE), _F32),
        compiler_params=_params("parallel"),
        name="small_proj",
    )(x, *([wt] * nblk), *([b.reshape(depth, 1, N)] * nblk))


def _gate_proj(xs, w_refs, b_refs):
    wh, wl = _split(jnp.concatenate([r[0] for r in w_refs], axis=0))
    acc = _dot_nt(xs[0], wh) + (_dot_nt(xs[0], wl) + _dot_nt(xs[1], wh))
    return acc + jnp.concatenate([r[0] for r in b_refs], axis=1)


def _ln_kernel(z_ref, g_ref, b_ref, *refs, nblk):
    z = z_ref[...]
    mu = jnp.mean(z, axis=-1, keepdims=True)
    d = z - mu
    var = jnp.mean(d * d, axis=-1, keepdims=True)
    y = d * lax.rsqrt(var + _LN_EPS) * g_ref[...] + b_ref[...]
    o_ref, ob_ref = refs[2 * nblk], refs[2 * nblk + 1]
    o_ref[...] = y
    ob_ref[...] = y.astype(_BF16)
    if nblk:
        refs[2 * nblk + 2][...] = _gate_proj(_split(y), refs[:nblk], refs[nblk:2 * nblk])


def _layernorm(z, g, b, gate=None):
    M, D = z.shape
    bm = _tile(M, 256)
    in_specs = [pl.BlockSpec((bm, D), lambda i: (i, 0)),
                pl.BlockSpec((1, D), lambda i: (0, 0)),
                pl.BlockSpec((1, D), lambda i: (0, 0))]
    out_specs = [pl.BlockSpec((bm, D), lambda i: (i, 0)), pl.BlockSpec((bm, D), lambda i: (i, 0))]
    out_shape = [jax.ShapeDtypeStruct((M, D), _F32), jax.ShapeDtypeStruct((M, D), _BF16)]
    args, nblk = [z, g.reshape(1, D), b.reshape(1, D)], 0
    if gate is not None:
        wt, bias, layer, row_blocks = gate
        depth, N = bias.shape
        nblk = len(row_blocks)
        in_specs += [pl.BlockSpec((1, _LANE, D), lambda i, c=c: (layer, c, 0)) for c in row_blocks]
        in_specs += [pl.BlockSpec((1, 1, _LANE), lambda i, c=c: (layer, 0, c)) for c in row_blocks]
        args += [wt] * nblk + [bias.reshape(depth, 1, N)] * nblk
        out_specs.append(pl.BlockSpec((bm, nblk * _LANE), lambda i: (i, 0)))
        out_shape.append(jax.ShapeDtypeStruct((M, nblk * _LANE), _F32))
    return pl.pallas_call(
        functools.partial(_ln_kernel, nblk=nblk),
        grid=(M // bm,),
        in_specs=in_specs,
        out_specs=out_specs,
        out_shape=out_shape,
        compiler_params=_params("parallel"),
        name="layernorm",
    )(*args)


def _merge_kernel(a_ref, b_ref, c_ref, w_ref, g_ref, o_ref, acc_ref):
    br = pl.program_id(2)

    def contrib(x_ref):
        return jax.nn.sigmoid(g_ref[...]) * jnp.dot(x_ref[...], w_ref[0, 0], preferred_element_type=_F32)

    @pl.when(br == 0)
    def _():
        acc_ref[...] = contrib(a_ref)

    @pl.when(br == 1)
    def _():
        acc_ref[...] += contrib(b_ref)

    @pl.when(br == 2)
    def _():
        o_ref[...] = (acc_ref[...] + contrib(c_ref)).astype(o_ref.dtype)


def _merge(oa, ob, oc, w_branch, layer, h, gate_col0):
    M, BW = oa.shape
    D = w_branch.shape[3]
    bm, bn = _tile(M, 1024), _tile(D, 1024)
    nj = D // bn
    assert gate_col0 % bn == 0
    g0 = gate_col0 // bn
    xspec = pl.BlockSpec((bm, BW), lambda i, j, n: (i, 0))
    return pl.pallas_call(
        _merge_kernel,
        grid=(M // bm, nj, 3),
        in_specs=[xspec, xspec, xspec,
                  pl.BlockSpec((1, 1, BW, bn), lambda i, j, n: (layer, n, 0, j)),
                  pl.BlockSpec((bm, bn), lambda i, j, n: (i, g0 + n * nj + j))],
        out_specs=pl.BlockSpec((bm, bn), lambda i, j, n: (i, j)),
        out_shape=jax.ShapeDtypeStruct((M, D), _BF16),
        scratch_shapes=[pltpu.VMEM((bm, bn), _F32)],
        compiler_params=_params("parallel", "parallel", "arbitrary"),
        name="branch_merge",
    )(oa, ob, oc, w_branch, h)


def _ffn_up_kernel(x_ref, wg_ref, wv_ref, o_ref):
    x = x_ref[...]
    g = jnp.dot(x, _bf(wg_ref[0]), preferred_element_type=_F32)
    v = jnp.dot(x, _bf(wv_ref[0]), preferred_element_type=_F32)
    o_ref[...] = (g * jax.nn.sigmoid(g) * v).astype(o_ref.dtype)


def _ffn_up(x, w_up, layer):
    M, K = x.shape
    F = w_up.shape[2] // 2
    bm, bn = _tile(M, 1024), _tile(F, 256)
    nj = F // bn
    return pl.pallas_call(
        _ffn_up_kernel,
        grid=(M // bm, nj),
        in_specs=[pl.BlockSpec((bm, K), lambda i, j: (i, 0)),
                  pl.BlockSpec((1, K, bn), lambda i, j: (layer, 0, j)),
                  pl.BlockSpec((1, K, bn), lambda i, j: (layer, 0, nj + j))],
        out_specs=pl.BlockSpec((bm, bn), lambda i, j: (i, j)),
        out_shape=jax.ShapeDtypeStruct((M, F), _BF16),
        compiler_params=_params("parallel", "parallel"),
        name="ffn_up",
    )(x, w_up, w_up)


def _transpose_hi(x):
    n = x.shape[1]
    eye = (lax.broadcasted_iota(jnp.int32, (n, n), 0) == lax.broadcasted_iota(jnp.int32, (n, n), 1)).astype(_F32)
    return lax.dot_general(eye, x, (((1,), (1,)), ((), ())), precision=_HI, preferred_element_type=_F32)


def _silu(x):
    return x * jax.nn.sigmoid(x)


def _softplus(x):
    return jnp.maximum(x, 0.0) + jnp.log1p(jnp.exp(-jnp.abs(x)))


def _chunk_masks():
    L = _CHUNK
    row = lax.broadcasted_iota(jnp.int32, (L, L), 0)
    col = lax.broadcasted_iota(jnp.int32, (L, L), 1)
    return row, col


def _ln_head(o, g):
    mu = jnp.mean(o, axis=-1, keepdims=True)
    d = o - mu
    var = jnp.mean(d * d, axis=-1, keepdims=True)
    return d * lax.rsqrt(var + _LN_EPS) * g


def _unit_lower_inverses(As, row, col, left):
    def bdiag(ys):
        return tuple(jnp.concatenate([jnp.where(left, y, jnp.zeros_like(y)),
                                      jnp.where(left, jnp.zeros_like(y), y)], axis=0) for y in ys)

    same16 = (row // 16) == (col // 16)
    same32 = (row // 32) == (col // 32)
    eye = (row == col).astype(_F32)
    D = [jnp.where(same16, A, 0.0) for A in As]
    Ds = [_split(d) for d in D]
    D2s = [_split(_mm3(d, bdiag(d))) for d in Ds]
    D2b = [bdiag(d2) for d2 in D2s]
    P = [eye - d for d in D]
    P = [p + _mm3(_split(p), d2) for p, d2 in zip(P, D2b)]
    D4s = [_split(_mm3(d2, d2b)) for d2, d2b in zip(D2s, D2b)]
    D4b = [bdiag(d4) for d4 in D4s]
    P = [p + _mm3(_split(p), d4) for p, d4 in zip(P, D4b)]
    D8b = [bdiag(_split(_mm3(d4, d4b))) for d4, d4b in zip(D4s, D4b)]
    P = [p + _mm3(_split(p), d8) for p, d8 in zip(P, D8b)]
    for off_mask in (jnp.logical_and(same32, jnp.logical_not(same16)), jnp.logical_not(same32)):
        Es = [_split(jnp.where(off_mask, A, 0.0)) for A in As]
        Ps = [_split(p) for p in P]
        Qb = [bdiag(_split(_mm3(e, bdiag(ps)))) for e, ps in zip(Es, Ps)]
        P = [p - _mm3(ps, q) for p, ps, q in zip(P, Ps, Qb)]
    return P


class _Steps:
    def __init__(self, bp, tp, bs, ts):
        assert tp % _CHUNK == 0 and ts % _CHUNK == 0
        self.bp, self.bs = bp, bs
        self.ncp, self.ncs = tp // _CHUNK, ts // _CHUNK
        self.np_steps = bp * self.ncp
        self.n = self.np_steps + bs * self.ncs

    def in_prompt(self, i):
        return i < self.np_steps

    def _sample_step(self, i):
        return jnp.maximum(i - self.np_steps, 0)

    def seq(self, i):
        return jnp.where(self.in_prompt(i), i // self.ncp, self.bp + self._sample_step(i) // self.ncs)

    def sample_seq(self, i):
        return self._sample_step(i) // self.ncs

    def prompt_seq(self, i):
        return jnp.minimum(i // self.ncp, self.bp - 1)

    def chunk(self, i):
        return jnp.where(self.in_prompt(i), i % self.ncp, self._sample_step(i) % self.ncs)

    def pos_block(self, i):
        return jnp.where(self.in_prompt(i), self.chunk(i), self.ncp + self.chunk(i))

    def first_last(self, i):
        c = self.chunk(i)
        return c == 0, jnp.where(self.in_prompt(i), c == self.ncp - 1, c == self.ncs - 1)


def _gdn_kernel(q_ref, k_ref, v_ref, z_ref, gs_ref, cwq_ref, cwk_ref, cwv_ref, gp_ref, ng_ref,
                s0_ref, c0q_ref, c0k_ref, c0v_ref, *rest, hpg, dk, dv, lane0, steps):
    o_ref, sp_ref, ss_ref, s_scr, bq, bk, bv = rest[-7:]
    L, P = _CHUNK, _SUBLANE
    i = pl.program_id(1)
    first, last = steps.first_last(i)
    in_p = steps.in_prompt(i)
    nc = c0q_ref.shape[2]

    @pl.when(jnp.logical_and(first, in_p))
    def _():
        s_scr[...] = jnp.zeros(s_scr.shape, _F32)
        for buf in (bq, bk, bv):
            buf[0:P, :] = jnp.zeros((P, buf.shape[1]), _F32)

    @pl.when(jnp.logical_and(first, jnp.logical_not(in_p)))
    def _():
        s_scr[...] = s0_ref[0, 0]
        bq[P - nc:P, :] = c0q_ref[0, 0]
        bk[P - nc:P, :] = c0k_ref[0, 0]
        bv[P - nc:P, :] = c0v_ref[0, 0]

    bq[P:P + L, :] = q_ref[...]
    bk[P:P + L, :] = k_ref[...]
    bv[P:P + L, :] = v_ref[...]

    row, col = _chunk_masks()
    tril = (row >= col).astype(_F32)
    gs = gs_ref[...]
    gp = gp_ref[0]
    beta_blk = jax.nn.sigmoid(gs)
    g_blk = -jnp.exp(gp[0:1, :]) * _softplus(gs + gp[1:2, :])
    G_blk = _dot_hi(tril, g_blk)
    GT2 = _transpose_hi(jnp.concatenate([G_blk, G_blk], axis=0))
    ng = ng_ref[...]
    heads = range(hpg)

    def conv(buf, cw_ref, cs):
        out = buf[P - nc:P - nc + L, cs] * cw_ref[0:1, cs]
        for j in range(1, nc + 1):
            out = out + buf[P - nc + j:P - nc + j + L, cs] * cw_ref[j:j + 1, cs]
        return out

    def l2norm(x):
        return x * lax.rsqrt(jnp.sum(x * x, axis=-1, keepdims=True) + 1e-6)

    ks = [slice(h * dk, (h + 1) * dk) for h in heads]
    vs = [slice(h * dv, (h + 1) * dv) for h in heads]
    q = [l2norm(_silu(conv(bq, cwq_ref, ks[h]))) * dk ** -0.5 for h in heads]
    k = [l2norm(_silu(conv(bk, cwk_ref, ks[h]))) for h in heads]
    v = [_silu(conv(bv, cwv_ref, vs[h])) for h in heads]
    la = lane0 + hpg
    beta = [beta_blk[:, lane0 + h:lane0 + h + 1] for h in heads]
    Gc = [G_blk[:, la + h:la + h + 1] for h in heads]
    eG = [jnp.exp(Gc[h]) for h in heads]
    kb = [_bf(k[h]) for h in heads]
    qb = [_bf(q[h]) for h in heads]

    pairs = range(hpg // 2)
    lane = lax.broadcasted_iota(jnp.int32, (L, 2 * L), 1)
    left = lane < L
    rowp = lax.broadcasted_iota(jnp.int32, (L, 2 * L), 0)
    colp = jnp.where(left, lane, lane - L)
    inclp, strictp = rowp >= colp, rowp > colp

    def side(a0, a1):
        return jnp.where(left, a0, a1)

    def bdiag(y0, y1):
        return jnp.concatenate([jnp.concatenate([y0, jnp.zeros_like(y1)], axis=1),
                                jnp.concatenate([jnp.zeros_like(y0), y1], axis=1)], axis=0)

    Gcp = [side(Gc[2 * p], Gc[2 * p + 1]) for p in pairs]
    Grp = [side(GT2[la + 2 * p:la + 2 * p + 1, :], GT2[la + 2 * p + 1:la + 2 * p + 2, :]) for p in pairs]
    betap = [side(beta[2 * p], beta[2 * p + 1]) for p in pairs]
    gam = [jnp.where(inclp, jnp.exp(jnp.where(inclp, Gcp[p] - Grp[p], 0.0)), 0.0) for p in pairs]
    kq = [_dot_nt(jnp.concatenate([jnp.concatenate([kb[2 * p], kb[2 * p + 1]], axis=1),
                                   jnp.concatenate([qb[2 * p], qb[2 * p + 1]], axis=1)], axis=0),
                  bdiag(kb[2 * p], kb[2 * p + 1])) for p in pairs]
    A = [jnp.where(strictp, betap[p] * kq[p][0:L] * gam[p], 0.0) for p in pairs]
    qk = [_bf(kq[p][L:2 * L] * gam[p]) for p in pairs]
    T = _unit_lower_inverses(A, rowp, colp, left)
    rhs = [_split(jnp.concatenate([beta[h] * v[h], (beta[h] * eG[h]) * k[h]], axis=1)) for h in heads]
    UWp = [_mm3(_split(T[p]), tuple(bdiag(rhs[2 * p][t], rhs[2 * p + 1][t]) for t in range(2))) for p in pairs]
    UW = [UWp[h // 2][:, (h % 2) * (dv + dk):(h % 2 + 1) * (dv + dk)] for h in heads]
    G_last = [Gc[h][L - 1:L, :] for h in heads]
    S = [s_scr[h] for h in heads]
    Sb = [_bf(S[h]) for h in heads]
    WQ = [_dot(jnp.concatenate([_bf(UW[h][:, dv:]), _bf(q[h] * eG[h])], axis=0), Sb[h]) for h in heads]
    db = [_bf(UW[h][:, 0:dv] - WQ[h][0:L]) for h in heads]
    qd = [_dot(qk[p], bdiag(db[2 * p], db[2 * p + 1])) for p in pairs]
    o = [WQ[h][L:2 * L] + qd[h // 2][:, (h % 2) * dv:(h % 2 + 1) * dv] for h in heads]
    for h in heads:
        k_dec = k[h] * jnp.exp(G_last[h] - Gc[h])
        s_scr[h] = jnp.exp(G_last[h]) * S[h] + _dot_tn(_bf(k_dec), db[h])
    for h in heads:
        on = o[h] * lax.rsqrt(jnp.mean(o[h] * o[h], axis=-1, keepdims=True) + 1e-6) * ng
        o_ref[:, vs[h]] = (on * _silu(z_ref[:, vs[h]])).astype(o_ref.dtype)

    bq[0:P, :] = bq[L:L + P, :]
    bk[0:P, :] = bk[L:L + P, :]
    bv[0:P, :] = bv[L:L + P, :]

    @pl.when(jnp.logical_and(last, in_p))
    def _():
        sp_ref[0, 0] = s_scr[...]

    @pl.when(jnp.logical_and(last, jnp.logical_not(in_p)))
    def _():
        ss_ref[0, 0] = s_scr[...]


def _state_outputs(prev, shapes_p, shapes_s, n_in, n_lead_out):
    shapes = [jax.ShapeDtypeStruct(s, _F32) for pair in zip(shapes_p, shapes_s) for s in pair]
    if prev is None:
        return shapes, [], {}, ()
    specs = [pl.BlockSpec(memory_space=pl.ANY)] * len(prev)
    aliases = {n_in + j: n_lead_out + j for j in range(len(prev))}
    return shapes, specs, aliases, tuple(prev)


def _gdn(h, hs, gate_block, lane0, conv_w, gp, norm_g, s_in, conv_in, prev, layer, steps):
    ntok = h.shape[0]
    depth, _, H, dk, dv = s_in.shape
    nc = conv_in.shape[2]
    assert dk == dv and nc < _SUBLANE and conv_w.shape[0] == nc + 1 and lane0 + 2 * H <= _LANE and H % 2 == 0
    hpg, ng_ = H, 1
    L, P = _CHUNK, _SUBLANE
    wq = hpg * dk
    sseq, pseq = steps.sample_seq, steps.prompt_seq
    tok = lambda off: pl.BlockSpec((L, wq), lambda g, i: (i, off + g))
    cw = lambda off: pl.BlockSpec((nc + 1, wq), lambda g, i: (0, off + g))
    c0 = lambda off: pl.BlockSpec((1, 1, nc, wq), lambda g, i: (layer, sseq(i), 0, off + g))
    in_specs = [tok(0), tok(ng_), tok(2 * ng_), tok(3 * ng_),
                pl.BlockSpec((L, _LANE), lambda g, i: (i, gate_block)),
                cw(0), cw(ng_), cw(2 * ng_),
                pl.BlockSpec((1, 2, _LANE), lambda g, i: (layer, 0, 0)),
                pl.BlockSpec((1, dv), lambda g, i: (0, 0)),
                pl.BlockSpec((1, 1, hpg, dk, dv), lambda g, i: (layer, sseq(i), g, 0, 0)),
                c0(0), c0(ng_), c0(2 * ng_)]
    shapes, pspecs, aliases, pargs = _state_outputs(
        prev, [(depth, steps.bp, H, dk, dv)], [(depth, steps.bs, H, dk, dv)], len(in_specs), 1)
    out = pl.pallas_call(
        functools.partial(_gdn_kernel, hpg=hpg, dk=dk, dv=dv, lane0=lane0, steps=steps),
        grid=(ng_, steps.n),
        in_specs=in_specs + pspecs,
        out_specs=[pl.BlockSpec((L, hpg * dv), lambda g, i: (i, g)),
                   pl.BlockSpec((1, 1, hpg, dk, dv), lambda g, i: (layer, pseq(i), g, 0, 0)),
                   pl.BlockSpec((1, 1, hpg, dk, dv), lambda g, i: (layer, sseq(i), g, 0, 0))],
        out_shape=[jax.ShapeDtypeStruct((ntok, H * dv), _BF16)] + shapes,
        input_output_aliases=aliases,
        scratch_shapes=[pltpu.VMEM((hpg, dk, dv), _F32)] + [pltpu.VMEM((P + L, wq), _F32)] * 3,
        compiler_params=_params("parallel", "arbitrary"),
        name="gdn_mixer",
    )(h, h, h, h, hs, conv_w, conv_w, conv_w, gp, norm_g.reshape(1, dv), s_in, conv_in, conv_in, conv_in, *pargs)
    return out[0], out[1:]


def _mlstm_kernel(q_ref, k_ref, v_ref, og_ref, gs_ref, fb_ref, ng_ref, c0_ref, n0_ref, m0_ref,
                  *rest, H, dk, dv, lane0, steps):
    o_ref, cp_ref, cs_ref, np_ref, ns_ref, mp_ref, ms_ref, c_scr, n_scr, m_scr = rest[-10:]
    L, P = _CHUNK, _SUBLANE
    i = pl.program_id(0)
    first, last = steps.first_last(i)
    in_p = steps.in_prompt(i)

    @pl.when(jnp.logical_and(first, in_p))
    def _():
        c_scr[...] = jnp.zeros(c_scr.shape, _F32)
        n_scr[...] = jnp.zeros(n_scr.shape, _F32)
        m_scr[...] = jnp.zeros(m_scr.shape, _F32)

    @pl.when(jnp.logical_and(first, jnp.logical_not(in_p)))
    def _():
        c_scr[...] = c0_ref[0, 0]
        n_scr[...] = n0_ref[0, 0]
        m0 = m0_ref[0, 0]
        for h in range(H):
            m_scr[h] = jnp.broadcast_to(m0[:, h:h + 1], (P, _LANE))

    row, col = _chunk_masks()
    incl = row >= col
    tril = incl.astype(_F32)
    gs = gs_ref[...]
    li, lf = lane0, lane0 + H
    lf_blk = -_softplus(-(gs + fb_ref[0]))
    F_blk = _dot_hi(tril, lf_blk)
    FT = _transpose_hi(F_blk)
    gsT = _transpose_hi(gs)

    heads = range(H)
    ks = [slice(h * dk, (h + 1) * dk) for h in heads]
    vs = [slice(h * dv, (h + 1) * dv) for h in heads]
    q = [q_ref[:, ks[h]] * dk ** -0.5 for h in heads]
    k = [k_ref[:, ks[h]] for h in heads]
    qb = [_bf(q[h]) for h in heads]
    kb = [_bf(k[h]) for h in heads]
    vb = [_bf(v_ref[:, vs[h]]) for h in heads]
    qk = [_dot_nt(qb[h], kb[h]) for h in heads]
    C = [c_scr[h] for h in heads]
    qC = [_dot(qb[h], _bf(C[h])) for h in heads]
    igc = [gs[:, li + h:li + h + 1] for h in heads]
    Fc = [F_blk[:, lf + h:lf + h + 1] for h in heads]
    logD = [jnp.where(incl, Fc[h] - FT[lf + h:lf + h + 1, :] + gsT[li + h:li + h + 1, :], -jnp.inf) for h in heads]
    m_prev = [m_scr[h][0:1, 0:1] for h in heads]
    m_t = [jnp.maximum(Fc[h] + m_prev[h], jnp.max(logD[h], axis=-1, keepdims=True)) for h in heads]
    Sm = [jnp.exp(logD[h] - m_t[h]) * qk[h] for h in heads]
    Sv = [_dot(_bf(Sm[h]), vb[h]) for h in heads]
    inter = [jnp.exp(Fc[h] + m_prev[h] - m_t[h]) for h in heads]
    n = [n_scr[h:h + 1, :] for h in heads]
    m_last = [m_t[h][L - 1:L, :] for h in heads]
    wk = [jnp.exp(Fc[h][L - 1:L, :] - Fc[h] + igc[h] - m_last[h]) * k[h] for h in heads]
    kv = [_dot_tn(_bf(wk[h]), vb[h]) for h in heads]
    for h in heads:
        dec = inter[h][L - 1:L, :]
        c_scr[h] = dec * C[h] + kv[h]
        n_scr[h:h + 1, :] = dec * n[h] + jnp.sum(wk[h], axis=0, keepdims=True)
        m_scr[h] = jnp.broadcast_to(m_last[h], (P, _LANE))
    for h in heads:
        num = inter[h] * qC[h] + Sv[h]
        den = inter[h] * jnp.sum(q[h] * n[h], axis=-1, keepdims=True) + jnp.sum(Sm[h], axis=-1, keepdims=True)
        hh = num / jnp.maximum(jnp.abs(den), jnp.exp(-m_t[h]))
        o_ref[:, vs[h]] = (_ln_head(hh, ng_ref[:, vs[h]]) * jax.nn.sigmoid(og_ref[:, vs[h]])).astype(o_ref.dtype)

    def write_state(c_ref, n_ref, m_ref):
        c_ref[0, 0] = c_scr[...]
        n_ref[0, 0] = n_scr[...]
        lane = lax.broadcasted_iota(jnp.int32, (1, _LANE), 1)
        m_row = jnp.zeros((1, _LANE), _F32)
        for h in range(H):
            m_row = jnp.where(lane == h, m_scr[h][0:1, :], m_row)
        m_ref[0, 0] = m_row

    @pl.when(jnp.logical_and(last, in_p))
    def _():
        write_state(cp_ref, np_ref, mp_ref)

    @pl.when(jnp.logical_and(last, jnp.logical_not(in_p)))
    def _():
        write_state(cs_ref, ns_ref, ms_ref)


def _mlstm(h, col0, hs, gate_block, lane0, f_bias, norm_g, c_in, n_in, m_in, prev, layer, steps):
    ntok = h.shape[0]
    depth, _, H, dk, dv = c_in.shape
    assert dv % dk == 0 and 2 % (dv // dk) == 0 and lane0 + 2 * H <= _LANE and col0 % (H * dv) == 0
    L = _CHUNK
    bp, bs = steps.bp, steps.bs
    jq, jv = col0 // (H * dk), col0 // (H * dv) + 2 * dk // dv
    tq = lambda j: pl.BlockSpec((L, H * dk), lambda i: (i, jq + j))
    tv = lambda j: pl.BlockSpec((L, H * dv), lambda i: (i, jv + j))
    sseq, pseq = steps.sample_seq, steps.prompt_seq
    in_specs = [tq(0), tq(1), tv(0), tv(1),
                pl.BlockSpec((L, _LANE), lambda i: (i, gate_block)),
                pl.BlockSpec((1, 1, _LANE), lambda i: (layer, 0, 0)),
                pl.BlockSpec((1, H * dv), lambda i: (0, 0)),
                pl.BlockSpec((1, 1, H, dk, dv), lambda i: (layer, sseq(i), 0, 0, 0)),
                pl.BlockSpec((1, 1, H, dk), lambda i: (layer, sseq(i), 0, 0)),
                pl.BlockSpec((1, 1, 1, H), lambda i: (layer, sseq(i), 0, 0))]
    tails = [(H, dk, dv), (H, dk), (1, _LANE)]
    shapes, pspecs, aliases, pargs = _state_outputs(
        prev, [(depth, bp) + t for t in tails], [(depth, bs) + t for t in tails], len(in_specs), 1)
    state_specs = []
    for t in tails:
        zeros = (0,) * len(t)
        state_specs.append(pl.BlockSpec((1, 1) + t, lambda i, z=zeros: (layer, pseq(i)) + z))
        state_specs.append(pl.BlockSpec((1, 1) + t, lambda i, z=zeros: (layer, sseq(i)) + z))
    out = pl.pallas_call(
        functools.partial(_mlstm_kernel, H=H, dk=dk, dv=dv, lane0=lane0, steps=steps),
        grid=(steps.n,),
        in_specs=in_specs + pspecs,
        out_specs=[pl.BlockSpec((L, H * dv), lambda i: (i, 0))] + state_specs,
        out_shape=[jax.ShapeDtypeStruct((ntok, H * dv), _BF16)] + shapes,
        input_output_aliases=aliases,
        scratch_shapes=[pltpu.VMEM((H, dk, dv), _F32), pltpu.VMEM((H, dk), _F32),
                        pltpu.VMEM((H, _SUBLANE, _LANE), _F32)],
        compiler_params=_params("arbitrary"),
        name="mlstm_mixer",
    )(h, h, h, h, hs, f_bias, norm_g.reshape(1, H * dv), c_in, n_in,
      m_in.reshape(m_in.shape[0], m_in.shape[1], 1, H), *pargs)
    return out[0], out[1:]


def _ret_kernel(lg_ref, q_ref, k_ref, v_ref, og_ref, cos_ref, sin_ref, ng_ref, s0_ref,
                *rest, H, dk, dv, steps):
    o_ref, sp_ref, ss_ref, s_scr = rest[-4:]
    L = _CHUNK
    i = pl.program_id(0)
    first, last = steps.first_last(i)
    in_p = steps.in_prompt(i)

    @pl.when(jnp.logical_and(first, in_p))
    def _():
        s_scr[...] = jnp.zeros(s_scr.shape, _F32)

    @pl.when(jnp.logical_and(first, jnp.logical_not(in_p)))
    def _():
        s_scr[...] = s0_ref[0, 0]

    row, col = _chunk_masks()
    incl = row >= col
    tdiff = (row - col).astype(_F32)
    jc = lax.broadcasted_iota(jnp.int32, (L, 1), 0).astype(_F32)
    cos, sin = cos_ref[...], sin_ref[...]

    def rotary(x):
        return x * cos + pltpu.roll(x, dk // 2, 1) * sin

    heads = range(H)
    ks = [slice(h * dk, (h + 1) * dk) for h in heads]
    vs = [slice(h * dv, (h + 1) * dv) for h in heads]
    lg = [lg_ref[h] for h in heads]
    q = [rotary(q_ref[:, ks[h]]) for h in heads]
    k = [rotary(k_ref[:, ks[h]]) * dk ** -0.5 for h in heads]
    vb = [_bf(v_ref[:, vs[h]]) for h in heads]
    qk = [_dot_nt(_bf(q[h]), _bf(k[h])) for h in heads]
    S = [s_scr[h] for h in heads]
    qS = [_dot(_bf(q[h] * jnp.exp(lg[h] * (jc + 1.0))), _bf(S[h])) for h in heads]
    kv = [_dot_tn(_bf(k[h] * jnp.exp(lg[h] * (L - 1.0 - jc))), vb[h]) for h in heads]
    D = [jnp.where(incl, jnp.exp(jnp.where(incl, lg[h] * tdiff, 0.0)), 0.0) for h in heads]
    o = [_dot(_bf(qk[h] * D[h]), vb[h]) + qS[h] for h in heads]
    for h in heads:
        s_scr[h] = jnp.exp(lg[h] * jnp.full((1, 1), float(L), _F32)) * S[h] + kv[h]
    for h in heads:
        o_ref[:, vs[h]] = (_ln_head(o[h], ng_ref[:, vs[h]]) * _silu(og_ref[:, vs[h]])).astype(o_ref.dtype)

    @pl.when(jnp.logical_and(last, in_p))
    def _():
        sp_ref[0, 0] = s_scr[...]

    @pl.when(jnp.logical_and(last, jnp.logical_not(in_p)))
    def _():
        ss_ref[0, 0] = s_scr[...]


def _retention(h, col0, lg, cos_tab, sin_tab, norm_g, s_in, prev, layer, steps):
    ntok = h.shape[0]
    depth, _, H, dk, dv = s_in.shape
    assert dv % dk == 0 and 2 % (dv // dk) == 0 and col0 % (H * dv) == 0
    L = _CHUNK
    jq, jv = col0 // (H * dk), col0 // (H * dv) + 2 * dk // dv
    tq = lambda j: pl.BlockSpec((L, H * dk), lambda i: (i, jq + j))
    tv = lambda j: pl.BlockSpec((L, H * dv), lambda i: (i, jv + j))
    tab = pl.BlockSpec((L, dk), lambda i: (steps.pos_block(i), 0))
    sseq, pseq = steps.sample_seq, steps.prompt_seq
    in_specs = [pl.BlockSpec(memory_space=pltpu.SMEM),
                tq(0), tq(1), tv(0), tv(1), tab, tab,
                pl.BlockSpec((1, H * dv), lambda i: (0, 0)),
                pl.BlockSpec((1, 1, H, dk, dv), lambda i: (layer, sseq(i), 0, 0, 0))]
    shapes, pspecs, aliases, pargs = _state_outputs(
        prev, [(depth, steps.bp, H, dk, dv)], [(depth, steps.bs, H, dk, dv)], len(in_specs), 1)
    out = pl.pallas_call(
        functools.partial(_ret_kernel, H=H, dk=dk, dv=dv, steps=steps),
        grid=(steps.n,),
        in_specs=in_specs + pspecs,
        out_specs=[pl.BlockSpec((L, H * dv), lambda i: (i, 0)),
                   pl.BlockSpec((1, 1, H, dk, dv), lambda i: (layer, pseq(i), 0, 0, 0)),
                   pl.BlockSpec((1, 1, H, dk, dv), lambda i: (layer, sseq(i), 0, 0, 0))],
        out_shape=[jax.ShapeDtypeStruct((ntok, H * dv), _BF16)] + shapes,
        input_output_aliases=aliases,
        scratch_shapes=[pltpu.VMEM((H, dk, dv), _F32)],
        compiler_params=_params("arbitrary"),
        name="retention_mixer",
    )(lg, h, h, h, h, cos_tab, sin_tab, norm_g.reshape(1, H * dv), s_in, *pargs)
    return out[0], out[1:]


def _rotary_tables(tp, ts, dk):
    half = dk // 2
    freq = _ROPE_BASE ** (-jnp.arange(half, dtype=_F32) / half)
    pos = jnp.concatenate([0.0 + jnp.arange(tp, dtype=_F32), float(_PAST_LEN) + jnp.arange(ts, dtype=_F32)])
    ang = pos[:, None] * freq[None, :]
    cos, sin = jnp.cos(ang), jnp.sin(ang)
    return jnp.concatenate([cos, cos], axis=1), jnp.concatenate([-sin, sin], axis=1)


def kernel(x_prompt, x_sample, state_gdn_S, state_gdn_conv, state_mlstm_C, state_mlstm_n, state_mlstm_m, state_ret_S, w_in, b_in, gdn_conv_w, gdn_A_log, gdn_dt_bias, gdn_norm_g, mlstm_f_bias, mlstm_norm_g, ret_norm_g, w_branch, w_out, ln1_g, ln1_b, ln2_g, ln2_b, w_ffn_up, w_ffn_down):
    bp, tp, D = x_prompt.shape
    bs, ts, _ = x_sample.shape
    depth = w_in.shape[0]
    alpha = (2 * depth) ** 0.25
    Hg, gdk, gdv = state_gdn_S.shape[2:]
    Hm, mdk, mdv = state_mlstm_C.shape[2:]
    Hr, rdk, rdv = state_ret_S.shape[2:]
    nconv = state_gdn_conv.shape[2]
    steps = _Steps(bp, tp, bs, ts)
    np_tok = bp * tp

    widths = (2 * Hg * gdk + Hg * gdv, Hg * gdv, Hg, Hg,
              Hm * mdk, Hm * mdk, Hm * mdv, Hm * mdv, Hm, Hm,
              Hr * rdk, Hr * rdk, Hr * rdv, Hr * rdv, 3 * D)
    offs = [0]
    for w_ in widths:
        offs.append(offs[-1] + w_)
    o_qkv, _, o_b, o_a, o_mq, _, _, _, o_mi, o_mf, o_rq, _, _, _, _, o_end = offs
    assert o_end == w_in.shape[2]
    qkv_w = widths[0]
    wide = ((o_qkv, o_b), (o_mq, o_mi), (o_rq, o_end))
    ml_col0 = o_b - o_qkv
    ret_col0 = ml_col0 + (o_mi - o_mq)
    gate_col0 = ret_col0 + sum(widths[10:14])

    gdn_blk, gdn_lane = o_b // _LANE, o_b % _LANE
    ml_blk, ml_lane = o_mi // _LANE, o_mi % _LANE
    assert o_a == o_b + Hg and o_mf == o_mi + Hm

    def wide_cols(a):
        return jnp.concatenate([a[..., lo:hi] for lo, hi in wide], axis=-1)

    cos_tab, sin_tab = _rotary_tables(tp, ts, rdk)
    lg = jnp.log(1.0 - 2.0 ** (-5.0 - jnp.arange(Hr, dtype=_F32)))

    w_in_t = jnp.swapaxes(w_in, 1, 2)
    w_wide, b_wide = _regroup_weights(w_in_t, wide), wide_cols(b_in)
    w_br, w_o, w_dn = _bf(w_branch), _bf(w_out), _bf(w_ffn_down)
    la = gdn_lane + Hg
    gp = jnp.zeros((depth, 2, _LANE), _F32)
    gp = gp.at[:, 0, la:la + Hg].set(gdn_A_log).at[:, 1, la:la + Hg].set(gdn_dt_bias)
    lf = ml_lane + Hm
    fb = jnp.zeros((depth, 1, _LANE), _F32).at[:, 0, lf:lf + Hm].set(mlstm_f_bias)

    x = jnp.concatenate([x_prompt.reshape(np_tok, D), x_sample.reshape(bs * ts, D)], axis=0)
    xb = x.astype(_BF16)
    g_st = m_st = r_st = None
    conv_p, conv_s = [], []
    for l in range(depth):
        h = _matmul(xb, w_wide, l, bias=b_wide[l], name="in_proj")
        if l == 0:
            hs = _small_proj(x, w_in_t, b_in, l, (gdn_blk, ml_blk))
        oa, g_st = _gdn(h, hs, 0, gdn_lane, gdn_conv_w[l], gp, gdn_norm_g[l], state_gdn_S, state_gdn_conv, g_st, l, steps)
        tails = h.reshape(steps.n, _CHUNK, -1)[:, _CHUNK - nconv:, :qkv_w]
        conv_p.append(tails[steps.ncp - 1:steps.np_steps:steps.ncp])
        conv_s.append(tails[steps.np_steps + steps.ncs - 1::steps.ncs])
        ob, m_st = _mlstm(h, ml_col0, hs, 1, ml_lane, fb, mlstm_norm_g[l],
                          state_mlstm_C, state_mlstm_n, state_mlstm_m, m_st, l, steps)
        oc, r_st = _retention(h, ret_col0, lg, cos_tab, sin_tab, ret_norm_g[l], state_ret_S, r_st, l, steps)

        merged = _merge(oa, ob, oc, w_br, l, h, gate_col0)
        z1 = _matmul(merged, w_o, l, res=x, alpha=alpha, name="out_proj")
        x, xb = _layernorm(z1, ln1_g[l], ln1_b[l])

        act = _ffn_up(xb, w_ffn_up, l)
        z2 = _matmul(act, w_dn, l, res=x, alpha=alpha, bm=512, bn=512, name="ffn_down")
        if l + 1 < depth:
            x, xb, hs = _layernorm(z2, ln2_g[l], ln2_b[l], gate=(w_in_t, b_in, l + 1, (gdn_blk, ml_blk)))
        else:
            x, xb = _layernorm(z2, ln2_g[l], ln2_b[l])

    gS_p, gS_s = g_st
    mC_p, mC_s, mn_p, mn_s, mm_p, mm_s = m_st
    rS_p, rS_s = r_st
    yp = x[:np_tok].reshape(bp, tp, D)
    ys = x[np_tok:].reshape(bs, ts, D)
    return (yp, ys,
            gS_p, jnp.stack(conv_p), mC_p, mn_p, mm_p[:, :, 0, :Hm], rS_p,
            gS_s, jnp.stack(conv_s), mC_s, mn_s, mm_s[:, :, 0, :Hm], rS_s)
```

```python
import functools

import jax
import jax.numpy as jnp
from jax import lax
from jax.experimental import pallas as pl
from jax.experimental.pallas import tpu as pltpu

_F32 = jnp.float32
_BF16 = jnp.bfloat16

_CHUNK = 64
_PAST_LEN = 1024
_ROPE_BASE = 10000.0
_LN_EPS = 1e-5
_LANE = 128
_SUBLANE = 8
_VMEM_LIMIT = 56 * 1024 * 1024


def _tile(dim, pref, align=_LANE):
    best = None
    t = align
    while t <= min(dim, pref):
        if dim % t == 0:
            best = t
        t += align
    return dim if best is None else best


def _params(*sem):
    return pltpu.CompilerParams(dimension_semantics=sem, vmem_limit_bytes=_VMEM_LIMIT)


def _bf(x):
    return x.astype(_BF16)


def _dot(a, b):
    return jnp.dot(a, b, preferred_element_type=_F32)


def _dot_nt(a, b):
    return lax.dot_general(a, b, (((1,), (1,)), ((), ())), preferred_element_type=_F32)


def _dot_tn(a, b):
    return lax.dot_general(a, b, (((0,), (0,)), ((), ())), preferred_element_type=_F32)


def _split3(x):
    p0 = x.astype(_BF16)
    r1 = x - p0.astype(_F32)
    p1 = r1.astype(_BF16)
    return p0, p1, (r1 - p1.astype(_F32)).astype(_BF16)


def _dot_hi(a, b):
    a = a.astype(_BF16)
    p0, p1, p2 = _split3(b)
    return _dot(a, p0) + (_dot(a, p1) + _dot(a, p2))


def _split(x):
    hi = x.astype(_BF16)
    return hi, (x - hi.astype(_F32)).astype(_BF16)


def _mm3(a, b):
    return _dot(a[0], b[0]) + (_dot(a[0], b[1]) + _dot(a[1], b[0]))


def _mm_kernel(*refs, nk, has_bias, has_res, alpha):
    it = iter(refs)
    x_ref, w_ref = next(it), next(it)
    b_ref = next(it) if has_bias else None
    r_ref = next(it) if has_res else None
    o_ref = next(it)
    acc_ref = next(it) if nk > 1 else None

    def finish(v):
        if has_bias:
            v = v + b_ref[...]
        if has_res:
            v = alpha * r_ref[...] + v
        o_ref[...] = v.astype(o_ref.dtype)

    part = jnp.dot(x_ref[...], w_ref[0], preferred_element_type=_F32)
    if nk == 1:
        finish(part)
    else:
        k = pl.program_id(2)

        @pl.when(k == 0)
        def _():
            acc_ref[...] = part

        @pl.when(k > 0)
        def _():
            acc_ref[...] += part

        @pl.when(k == nk - 1)
        def _():
            finish(acc_ref[...])


def _matmul(x, w, layer, *, bias=None, res=None, alpha=1.0, out_dtype=_F32, bm=1024, bn=1024, bk=None, name):
    M, K = x.shape
    N = w.shape[2]
    bm, bn = _tile(M, bm), _tile(N, bn)
    bk = K if bk is None else _tile(K, bk)
    nk = K // bk
    in_specs = [pl.BlockSpec((bm, bk), lambda i, j, k: (i, k)),
                pl.BlockSpec((1, bk, bn), lambda i, j, k: (layer, k, j))]
    args = [x, w]
    if bias is not None:
        in_specs.append(pl.BlockSpec((1, bn), lambda i, j, k: (0, j)))
        args.append(bias.reshape(1, N).astype(_F32))
    if res is not None:
        in_specs.append(pl.BlockSpec((bm, bn), lambda i, j, k: (i, j)))
        args.append(res)
    return pl.pallas_call(
        functools.partial(_mm_kernel, nk=nk, has_bias=bias is not None, has_res=res is not None, alpha=alpha),
        grid=(M // bm, N // bn, nk),
        in_specs=in_specs,
        out_specs=pl.BlockSpec((bm, bn), lambda i, j, k: (i, j)),
        out_shape=jax.ShapeDtypeStruct((M, N), out_dtype),
        scratch_shapes=[pltpu.VMEM((bm, bn), _F32)] if nk > 1 else [],
        compiler_params=_params("parallel", "parallel", "arbitrary"),
        name=name,
    )(*args)


_REGROUP_EXTRA = 64


def _regroup_kernel(a_ref, b_ref, o_ref, *, regions):
    j = pl.program_id(2)

    def emit(lo, hi, shift):
        @pl.when(jnp.logical_and(j >= lo, j < hi))
        def _():
            if shift == 0:
                t = a_ref[0]
            else:
                t = jnp.concatenate([a_ref[0, shift:, :], b_ref[0, :shift, :]], axis=0)
            o_ref[0] = _bf(t.T)

    for lo, hi, shift in regions:
        emit(lo, hi, shift)


def _regroup_weights(wt, spans):
    depth, N, K = wt.shape
    starts, shifts, pos = [], [], 0
    for lo, hi in spans:
        starts.append(pos)
        shifts.append(lo - pos)
        pos += hi - lo
    assert all(0 <= s <= _REGROUP_EXTRA and s % _SUBLANE == 0 for s in shifts)
    bn = _LANE
    for t in range(_LANE, 512 + 1, _LANE):
        if all(v % t == 0 for v in starts + [pos]):
            bn = t
    bk = _tile(K, 2048)
    last_extra_block = (N - 1) // _REGROUP_EXTRA
    bounds = [s // bn for s in starts] + [pos // bn]
    regions = tuple((bounds[r], bounds[r + 1], shifts[r]) for r in range(len(spans)))
    return pl.pallas_call(
        functools.partial(_regroup_kernel, regions=regions),
        grid=(depth, K // bk, pos // bn),
        in_specs=[pl.BlockSpec((1, bn, bk), lambda l, i, j: (l, j, i)),
                  pl.BlockSpec((1, _REGROUP_EXTRA, bk),
                               lambda l, i, j: (l, jnp.minimum((j + 1) * (bn // _REGROUP_EXTRA), last_extra_block), i))],
        out_specs=pl.BlockSpec((1, bk, bn), lambda l, i, j: (l, i, j)),
        out_shape=jax.ShapeDtypeStruct((depth, K, pos), _BF16),
        compiler_params=_params("parallel", "parallel", "parallel"),
        name="regroup_w_in",
    )(wt, wt)


def _small_proj_kernel(x_ref, *refs, nblk):
    xs = _split(x_ref[...])
    w_refs, b_refs, o_ref = refs[:nblk], refs[nblk:2 * nblk], refs[2 * nblk]
    o_ref[...] = _gate_proj(xs, w_refs, b_refs)


def _small_proj(x, wt, b, layer, row_blocks):
    M, K = x.shape
    depth, N = b.shape
    bm = _tile(M, 256)
    nblk = len(row_blocks)
    wspec = lambda c: pl.BlockSpec((1, _LANE, K), lambda i: (layer, c, 0))
    bspec = lambda c: pl.BlockSpec((1, 1, _LANE), lambda i: (layer, 0, c))
    return pl.pallas_call(
        functools.partial(_small_proj_kernel, nblk=nblk),
        grid=(M // bm,),
        in_specs=[pl.BlockSpec((bm, K), lambda i: (i, 0))] + [wspec(c) for c in row_blocks] + [bspec(c) for c in row_blocks],
        out_specs=pl.BlockSpec((bm, nblk * _LANE), lambda i: (i, 0)),
        out_shape=jax.ShapeDtypeStruct((M, nblk * _LANE), _F32),
        compiler_params=_params("parallel"),
        name="small_proj",
    )(x, *([wt] * nblk), *([b.reshape(depth, 1, N)] * nblk))


def _gate_proj(xs, w_refs, b_refs):
    wh, wl = _split(jnp.concatenate([r[0] for r in w_refs], axis=0))
    acc = _dot_nt(xs[0], wh) + (_dot_nt(xs[0], wl) + _dot_nt(xs[1], wh))
    return acc + jnp.concatenate([r[0] for r in b_refs], axis=1)


def _ln_kernel(z_ref, g_ref, b_ref, *refs, nblk):
    z = z_ref[...]
    mu = jnp.mean(z, axis=-1, keepdims=True)
    d = z - mu
    var = jnp.mean(d * d, axis=-1, keepdims=True)
    y = d * lax.rsqrt(var + _LN_EPS) * g_ref[...] + b_ref[...]
    o_ref, ob_ref = refs[2 * nblk], refs[2 * nblk + 1]
    o_ref[...] = y
    ob_ref[...] = y.astype(_BF16)
    if nblk:
        refs[2 * nblk + 2][...] = _gate_proj(_split(y), refs[:nblk], refs[nblk:2 * nblk])


def _layernorm(z, g, b, gate=None):
    M, D = z.shape
    bm = _tile(M, 256)
    in_specs = [pl.BlockSpec((bm, D), lambda i: (i, 0)),
                pl.BlockSpec((1, D), lambda i: (0, 0)),
                pl.BlockSpec((1, D), lambda i: (0, 0))]
    out_specs = [pl.BlockSpec((bm, D), lambda i: (i, 0)), pl.BlockSpec((bm, D), lambda i: (i, 0))]
    out_shape = [jax.ShapeDtypeStruct((M, D), _F32), jax.ShapeDtypeStruct((M, D), _BF16)]
    args, nblk = [z, g.reshape(1, D), b.reshape(1, D)], 0
    if gate is not None:
        wt, bias, layer, row_blocks = gate
        depth, N = bias.shape
        nblk = len(row_blocks)
        in_specs += [pl.BlockSpec((1, _LANE, D), lambda i, c=c: (layer, c, 0)) for c in row_blocks]
        in_specs += [pl.BlockSpec((1, 1, _LANE), lambda i, c=c: (layer, 0, c)) for c in row_blocks]
        args += [wt] * nblk + [bias.reshape(depth, 1, N)] * nblk
        out_specs.append(pl.BlockSpec((bm, nblk * _LANE), lambda i: (i, 0)))
        out_shape.append(jax.ShapeDtypeStruct((M, nblk * _LANE), _F32))
    return pl.pallas_call(
        functools.partial(_ln_kernel, nblk=nblk),
        grid=(M // bm,),
        in_specs=in_specs,
        out_specs=out_specs,
        out_shape=out_shape,
        compiler_params=_params("parallel"),
        name="layernorm",
    )(*args)


def _merge_kernel(a_ref, b_ref, c_ref, w_ref, g_ref, o_ref, acc_ref):
    br = pl.program_id(2)

    def contrib(x_ref):
        return jax.nn.sigmoid(g_ref[...]) * jnp.dot(x_ref[...], w_ref[0, 0], preferred_element_type=_F32)

    @pl.when(br == 0)
    def _():
        acc_ref[...] = contrib(a_ref)

    @pl.when(br == 1)
    def _():
        acc_ref[...] += contrib(b_ref)

    @pl.when(br == 2)
    def _():
        o_ref[...] = (acc_ref[...] + contrib(c_ref)).astype(o_ref.dtype)


def _merge(oa, ob, oc, w_branch, layer, h, gate_col0):
    M, BW = oa.shape
    D = w_branch.shape[3]
    bm, bn = _tile(M, 1024), _tile(D, 1024)
    nj = D // bn
    assert gate_col0 % bn == 0
    g0 = gate_col0 // bn
    xspec = pl.BlockSpec((bm, BW), lambda i, j, n: (i, 0))
    return pl.pallas_call(
        _merge_kernel,
        grid=(M // bm, nj, 3),
        in_specs=[xspec, xspec, xspec,
                  pl.BlockSpec((1, 1, BW, bn), lambda i, j, n: (layer, n, 0, j)),
                  pl.BlockSpec((bm, bn), lambda i, j, n: (i, g0 + n * nj + j))],
        out_specs=pl.BlockSpec((bm, bn), lambda i, j, n: (i, j)),
        out_shape=jax.ShapeDtypeStruct((M, D), _BF16),
        scratch_shapes=[pltpu.VMEM((bm, bn), _F32)],
        compiler_params=_params("parallel", "parallel", "arbitrary"),
        name="branch_merge",
    )(oa, ob, oc, w_branch, h)


def _ffn_up_kernel(x_ref, wg_ref, wv_ref, o_ref):
    x = x_ref[...]
    g = jnp.dot(x, _bf(wg_ref[0]), preferred_element_type=_F32)
    v = jnp.dot(x, _bf(wv_ref[0]), preferred_element_type=_F32)
    o_ref[...] = (g * jax.nn.sigmoid(g) * v).astype(o_ref.dtype)


def _ffn_up(x, w_up, layer):
    M, K = x.shape
    F = w_up.shape[2] // 2
    bm, bn = _tile(M, 1024), _tile(F, 256)
    nj = F // bn
    return pl.pallas_call(
        _ffn_up_kernel,
        grid=(M // bm, nj),
        in_specs=[pl.BlockSpec((bm, K), lambda i, j: (i, 0)),
                  pl.BlockSpec((1, K, bn), lambda i, j: (layer, 0, j)),
                  pl.BlockSpec((1, K, bn), lambda i, j: (layer, 0, nj + j))],
        out_specs=pl.BlockSpec((bm, bn), lambda i, j: (i, j)),
        out_shape=jax.ShapeDtypeStruct((M, F), _BF16),
        compiler_params=_params("parallel", "parallel"),
        name="ffn_up",
    )(x, w_up, w_up)


def _transpose_hi(x):
    n = x.shape[1]
    eye = (lax.broadcasted_iota(jnp.int32, (n, n), 0) == lax.broadcasted_iota(jnp.int32, (n, n), 1)).astype(_BF16)
    p0, p1, p2 = _split3(x)
    return _dot_nt(eye, p0) + (_dot_nt(eye, p1) + _dot_nt(eye, p2))


def _silu(x):
    hx = 0.5 * x
    return hx * jnp.tanh(hx) + hx


def _softplus(x):
    return jnp.maximum(x, 0.0) + jnp.log1p(jnp.exp(-jnp.abs(x)))


def _chunk_masks():
    L = _CHUNK
    row = lax.broadcasted_iota(jnp.int32, (L, L), 0)
    col = lax.broadcasted_iota(jnp.int32, (L, L), 1)
    return row, col


def _ln_head(o, g):
    mu = jnp.mean(o, axis=-1, keepdims=True)
    d = o - mu
    var = jnp.mean(d * d, axis=-1, keepdims=True)
    return d * lax.rsqrt(var + _LN_EPS) * g


def _unit_lower_inverses(As, row, col, left):
    def bdiag(ys):
        return tuple(jnp.concatenate([jnp.where(left, y, jnp.zeros_like(y)),
                                      jnp.where(left, jnp.zeros_like(y), y)], axis=0) for y in ys)

    same16 = (row // 16) == (col // 16)
    same32 = (row // 32) == (col // 32)
    eye = (row == col).astype(_F32)
    D = [jnp.where(same16, A, 0.0) for A in As]
    Ds = [_split(d) for d in D]
    D2s = [_split(_mm3(d, bdiag(d))) for d in Ds]
    D2b = [bdiag(d2) for d2 in D2s]
    P = [eye - d for d in D]
    P = [p + _mm3(_split(p), d2) for p, d2 in zip(P, D2b)]
    D4s = [_split(_mm3(d2, d2b)) for d2, d2b in zip(D2s, D2b)]
    D4b = [bdiag(d4) for d4 in D4s]
    P = [p + _mm3(_split(p), d4) for p, d4 in zip(P, D4b)]
    D8b = [bdiag(_split(_mm3(d4, d4b))) for d4, d4b in zip(D4s, D4b)]
    P = [p + _mm3(_split(p), d8) for p, d8 in zip(P, D8b)]
    for off_mask in (jnp.logical_and(same32, jnp.logical_not(same16)), jnp.logical_not(same32)):
        Es = [_split(jnp.where(off_mask, A, 0.0)) for A in As]
        Ps = [_split(p) for p in P]
        Qb = [bdiag(_split(_mm3(e, bdiag(ps)))) for e, ps in zip(Es, Ps)]
        P = [p - _mm3(ps, q) for p, ps, q in zip(P, Ps, Qb)]
    return P


class _Steps:
    def __init__(self, bp, tp, bs, ts):
        assert tp % _CHUNK == 0 and ts % _CHUNK == 0
        self.bp, self.bs = bp, bs
        self.ncp, self.ncs = tp // _CHUNK, ts // _CHUNK
        self.np_steps = bp * self.ncp
        self.n = self.np_steps + bs * self.ncs

    def in_prompt(self, i):
        return i < self.np_steps

    def _sample_step(self, i):
        return jnp.maximum(i - self.np_steps, 0)

    def seq(self, i):
        return jnp.where(self.in_prompt(i), i // self.ncp, self.bp + self._sample_step(i) // self.ncs)

    def sample_seq(self, i):
        return self._sample_step(i) // self.ncs

    def prompt_seq(self, i):
        return jnp.minimum(i // self.ncp, self.bp - 1)

    def chunk(self, i):
        return jnp.where(self.in_prompt(i), i % self.ncp, self._sample_step(i) % self.ncs)

    def pos_block(self, i):
        return jnp.where(self.in_prompt(i), self.chunk(i), self.ncp + self.chunk(i))

    def first_last(self, i):
        c = self.chunk(i)
        return c == 0, jnp.where(self.in_prompt(i), c == self.ncp - 1, c == self.ncs - 1)


def _gdn_kernel(q_ref, k_ref, v_ref, z_ref, gs_ref, cwq_ref, cwk_ref, cwv_ref, gp_ref, ng_ref,
                s0_ref, c0q_ref, c0k_ref, c0v_ref, *rest, hpg, dk, dv, lane0, steps):
    o_ref, sp_ref, ss_ref, s_scr, bq, bk, bv = rest[-7:]
    L, P = _CHUNK, _SUBLANE
    i = pl.program_id(1)
    first, last = steps.first_last(i)
    in_p = steps.in_prompt(i)
    nc = c0q_ref.shape[2]

    @pl.when(jnp.logical_and(first, in_p))
    def _():
        s_scr[...] = jnp.zeros(s_scr.shape, _F32)
        for buf in (bq, bk, bv):
            buf[0:P, :] = jnp.zeros((P, buf.shape[1]), _F32)

    @pl.when(jnp.logical_and(first, jnp.logical_not(in_p)))
    def _():
        s_scr[...] = s0_ref[0, 0]
        bq[P - nc:P, :] = c0q_ref[0, 0]
        bk[P - nc:P, :] = c0k_ref[0, 0]
        bv[P - nc:P, :] = c0v_ref[0, 0]

    bq[P:P + L, :] = q_ref[...]
    bk[P:P + L, :] = k_ref[...]
    bv[P:P + L, :] = v_ref[...]

    row, col = _chunk_masks()
    tril = (row >= col).astype(_F32)
    gs = gs_ref[...]
    gp = gp_ref[0]
    beta_blk = jax.nn.sigmoid(gs)
    g_blk = -jnp.exp(gp[0:1, :]) * _softplus(gs + gp[1:2, :])
    G_blk = _dot_hi(tril, g_blk)
    GT2 = _transpose_hi(jnp.concatenate([G_blk, G_blk], axis=0))
    ng = ng_ref[...]
    heads = range(hpg)

    def conv(buf, cw_ref, cs):
        out = buf[P - nc:P - nc + L, cs] * cw_ref[0:1, cs]
        for j in range(1, nc + 1):
            out = out + buf[P - nc + j:P - nc + j + L, cs] * cw_ref[j:j + 1, cs]
        return out

    def l2norm(x):
        return x * lax.rsqrt(jnp.sum(x * x, axis=-1, keepdims=True) + 1e-6)

    ks = [slice(h * dk, (h + 1) * dk) for h in heads]
    vs = [slice(h * dv, (h + 1) * dv) for h in heads]
    q = [l2norm(_silu(conv(bq, cwq_ref, ks[h]))) * dk ** -0.5 for h in heads]
    k = [l2norm(_silu(conv(bk, cwk_ref, ks[h]))) for h in heads]
    v = [_silu(conv(bv, cwv_ref, vs[h])) for h in heads]
    la = lane0 + hpg
    beta = [beta_blk[:, lane0 + h:lane0 + h + 1] for h in heads]
    Gc = [G_blk[:, la + h:la + h + 1] for h in heads]
    eG = [jnp.exp(Gc[h]) for h in heads]
    kb = [_bf(k[h]) for h in heads]
    qb = [_bf(q[h]) for h in heads]

    pairs = range(hpg // 2)
    lane = lax.broadcasted_iota(jnp.int32, (L, 2 * L), 1)
    left = lane < L
    rowp = lax.broadcasted_iota(jnp.int32, (L, 2 * L), 0)
    colp = jnp.where(left, lane, lane - L)
    inclp, strictp = rowp >= colp, rowp > colp

    def side(a0, a1):
        return jnp.where(left, a0, a1)

    def bdiag(y0, y1):
        return jnp.concatenate([jnp.concatenate([y0, jnp.zeros_like(y1)], axis=1),
                                jnp.concatenate([jnp.zeros_like(y0), y1], axis=1)], axis=0)

    Gcp = [side(Gc[2 * p], Gc[2 * p + 1]) for p in pairs]
    Grp = [side(GT2[la + 2 * p:la + 2 * p + 1, :], GT2[la + 2 * p + 1:la + 2 * p + 2, :]) for p in pairs]
    betap = [side(beta[2 * p], beta[2 * p + 1]) for p in pairs]
    gam = [jnp.where(inclp, jnp.exp(jnp.where(inclp, Gcp[p] - Grp[p], 0.0)), 0.0) for p in pairs]
    kq = [_dot_nt(jnp.concatenate([jnp.concatenate([kb[2 * p], kb[2 * p + 1]], axis=1),
                                   jnp.concatenate([qb[2 * p], qb[2 * p + 1]], axis=1)], axis=0),
                  bdiag(kb[2 * p], kb[2 * p + 1])) for p in pairs]
    A = [jnp.where(strictp, betap[p] * kq[p][0:L] * gam[p], 0.0) for p in pairs]
    qk = [_bf(kq[p][L:2 * L] * gam[p]) for p in pairs]
    T = _unit_lower_inverses(A, rowp, colp, left)
    rhs = [_split(jnp.concatenate([beta[h] * v[h], (beta[h] * eG[h]) * k[h]], axis=1)) for h in heads]
    UWp = [_mm3(_split(T[p]), tuple(bdiag(rhs[2 * p][t], rhs[2 * p + 1][t]) for t in range(2))) for p in pairs]
    UW = [UWp[h // 2][:, (h % 2) * (dv + dk):(h % 2 + 1) * (dv + dk)] for h in heads]
    G_last = [Gc[h][L - 1:L, :] for h in heads]
    S = [s_scr[h] for h in heads]
    Sb = [_bf(S[h]) for h in heads]
    WQ = [_dot(jnp.concatenate([_bf(UW[h][:, dv:]), _bf(q[h] * eG[h])], axis=0), Sb[h]) for h in heads]
    db = [_bf(UW[h][:, 0:dv] - WQ[h][0:L]) for h in heads]
    qd = [_dot(qk[p], bdiag(db[2 * p], db[2 * p + 1])) for p in pairs]
    o = [WQ[h][L:2 * L] + qd[h // 2][:, (h % 2) * dv:(h % 2 + 1) * dv] for h in heads]
    for h in heads:
        k_dec = k[h] * jnp.exp(G_last[h] - Gc[h])
        s_scr[h] = jnp.exp(G_last[h]) * S[h] + _dot_tn(_bf(k_dec), db[h])
    for h in heads:
        on = o[h] * lax.rsqrt(jnp.mean(o[h] * o[h], axis=-1, keepdims=True) + 1e-6) * ng
        o_ref[:, vs[h]] = (on * _silu(z_ref[:, vs[h]])).astype(o_ref.dtype)

    bq[0:P, :] = bq[L:L + P, :]
    bk[0:P, :] = bk[L:L + P, :]
    bv[0:P, :] = bv[L:L + P, :]

    @pl.when(jnp.logical_and(last, in_p))
    def _():
        sp_ref[0, 0] = s_scr[...]

    @pl.when(jnp.logical_and(last, jnp.logical_not(in_p)))
    def _():
        ss_ref[0, 0] = s_scr[...]


def _state_outputs(prev, shapes_p, shapes_s, n_in, n_lead_out):
    shapes = [jax.ShapeDtypeStruct(s, _F32) for pair in zip(shapes_p, shapes_s) for s in pair]
    if prev is None:
        return shapes, [], {}, ()
    specs = [pl.BlockSpec(memory_space=pl.ANY)] * len(prev)
    aliases = {n_in + j: n_lead_out + j for j in range(len(prev))}
    return shapes, specs, aliases, tuple(prev)


def _gdn(h, hs, gate_block, lane0, conv_w, gp, norm_g, s_in, conv_in, prev, layer, steps):
    ntok = h.shape[0]
    depth, _, H, dk, dv = s_in.shape
    nc = conv_in.shape[2]
    assert dk == dv and nc < _SUBLANE and conv_w.shape[0] == nc + 1 and lane0 + 2 * H <= _LANE and H % 2 == 0
    hpg, ng_ = H, 1
    L, P = _CHUNK, _SUBLANE
    wq = hpg * dk
    sseq, pseq = steps.sample_seq, steps.prompt_seq
    tok = lambda off: pl.BlockSpec((L, wq), lambda g, i: (i, off + g))
    cw = lambda off: pl.BlockSpec((nc + 1, wq), lambda g, i: (0, off + g))
    c0 = lambda off: pl.BlockSpec((1, 1, nc, wq), lambda g, i: (layer, sseq(i), 0, off + g))
    in_specs = [tok(0), tok(ng_), tok(2 * ng_), tok(3 * ng_),
                pl.BlockSpec((L, _LANE), lambda g, i: (i, gate_block)),
                cw(0), cw(ng_), cw(2 * ng_),
                pl.BlockSpec((1, 2, _LANE), lambda g, i: (layer, 0, 0)),
                pl.BlockSpec((1, dv), lambda g, i: (0, 0)),
                pl.BlockSpec((1, 1, hpg, dk, dv), lambda g, i: (layer, sseq(i), g, 0, 0)),
                c0(0), c0(ng_), c0(2 * ng_)]
    shapes, pspecs, aliases, pargs = _state_outputs(
        prev, [(depth, steps.bp, H, dk, dv)], [(depth, steps.bs, H, dk, dv)], len(in_specs), 1)
    out = pl.pallas_call(
        functools.partial(_gdn_kernel, hpg=hpg, dk=dk, dv=dv, lane0=lane0, steps=steps),
        grid=(ng_, steps.n),
        in_specs=in_specs + pspecs,
        out_specs=[pl.BlockSpec((L, hpg * dv), lambda g, i: (i, g)),
                   pl.BlockSpec((1, 1, hpg, dk, dv), lambda g, i: (layer, pseq(i), g, 0, 0)),
                   pl.BlockSpec((1, 1, hpg, dk, dv), lambda g, i: (layer, sseq(i), g, 0, 0))],
        out_shape=[jax.ShapeDtypeStruct((ntok, H * dv), _BF16)] + shapes,
        input_output_aliases=aliases,
        scratch_shapes=[pltpu.VMEM((hpg, dk, dv), _F32)] + [pltpu.VMEM((P + L, wq), _F32)] * 3,
        compiler_params=_params("parallel", "arbitrary"),
        name="gdn_mixer",
    )(h, h, h, h, hs, conv_w, conv_w, conv_w, gp, norm_g.reshape(1, dv), s_in, conv_in, conv_in, conv_in, *pargs)
    return out[0], out[1:]


def _mlstm_kernel(q_ref, k_ref, v_ref, og_ref, gs_ref, fb_ref, ng_ref, c0_ref, n0_ref, m0_ref,
                  *rest, H, dk, dv, lane0, steps):
    o_ref, cp_ref, cs_ref, np_ref, ns_ref, mp_ref, ms_ref, c_scr, n_scr, m_scr = rest[-10:]
    L, P = _CHUNK, _SUBLANE
    i = pl.program_id(0)
    first, last = steps.first_last(i)
    in_p = steps.in_prompt(i)

    @pl.when(jnp.logical_and(first, in_p))
    def _():
        c_scr[...] = jnp.zeros(c_scr.shape, _F32)
        n_scr[...] = jnp.zeros(n_scr.shape, _F32)
        m_scr[...] = jnp.zeros(m_scr.shape, _F32)

    @pl.when(jnp.logical_and(first, jnp.logical_not(in_p)))
    def _():
        c_scr[...] = c0_ref[0, 0]
        n_scr[...] = n0_ref[0, 0]
        m0 = m0_ref[0, 0]
        for h in range(H):
            m_scr[h] = jnp.broadcast_to(m0[:, h:h + 1], (P, _LANE))

    row, col = _chunk_masks()
    incl = row >= col
    tril = incl.astype(_F32)
    gs = gs_ref[...]
    li, lf = lane0, lane0 + H
    lf_blk = -_softplus(-(gs + fb_ref[0]))
    F_blk = _dot_hi(tril, lf_blk)
    FT = _transpose_hi(F_blk)
    gsT = _transpose_hi(gs)

    heads = range(H)
    ks = [slice(h * dk, (h + 1) * dk) for h in heads]
    vs = [slice(h * dv, (h + 1) * dv) for h in heads]
    q = [q_ref[:, ks[h]] * dk ** -0.5 for h in heads]
    k = [k_ref[:, ks[h]] for h in heads]
    qb = [_bf(q[h]) for h in heads]
    kb = [_bf(k[h]) for h in heads]
    vb = [_bf(v_ref[:, vs[h]]) for h in heads]
    qk = [_dot_nt(qb[h], kb[h]) for h in heads]
    C = [c_scr[h] for h in heads]
    qC = [_dot(qb[h], _bf(C[h])) for h in heads]
    igc = [gs[:, li + h:li + h + 1] for h in heads]
    Fc = [F_blk[:, lf + h:lf + h + 1] for h in heads]
    logD = [jnp.where(incl, Fc[h] - FT[lf + h:lf + h + 1, :] + gsT[li + h:li + h + 1, :], -jnp.inf) for h in heads]
    m_prev = [m_scr[h][0:1, 0:1] for h in heads]
    m_t = [jnp.maximum(Fc[h] + m_prev[h], jnp.max(logD[h], axis=-1, keepdims=True)) for h in heads]
    Sm = [jnp.exp(logD[h] - m_t[h]) * qk[h] for h in heads]
    Sv = [_dot(_bf(Sm[h]), vb[h]) for h in heads]
    inter = [jnp.exp(Fc[h] + m_prev[h] - m_t[h]) for h in heads]
    n = [n_scr[h:h + 1, :] for h in heads]
    m_last = [m_t[h][L - 1:L, :] for h in heads]
    wk = [jnp.exp(Fc[h][L - 1:L, :] - Fc[h] + igc[h] - m_last[h]) * k[h] for h in heads]
    kv = [_dot_tn(_bf(wk[h]), vb[h]) for h in heads]
    for h in heads:
        dec = inter[h][L - 1:L, :]
        c_scr[h] = dec * C[h] + kv[h]
        n_scr[h:h + 1, :] = dec * n[h] + jnp.sum(wk[h], axis=0, keepdims=True)
        m_scr[h] = jnp.broadcast_to(m_last[h], (P, _LANE))
    for h in heads:
        num = inter[h] * qC[h] + Sv[h]
        den = inter[h] * jnp.sum(q[h] * n[h], axis=-1, keepdims=True) + jnp.sum(Sm[h], axis=-1, keepdims=True)
        hh = num / jnp.maximum(jnp.abs(den), jnp.exp(-m_t[h]))
        o_ref[:, vs[h]] = (_ln_head(hh, ng_ref[:, vs[h]]) * jax.nn.sigmoid(og_ref[:, vs[h]])).astype(o_ref.dtype)

    def write_state(c_ref, n_ref, m_ref):
        c_ref[0, 0] = c_scr[...]
        n_ref[0, 0] = n_scr[...]
        lane = lax.broadcasted_iota(jnp.int32, (1, _LANE), 1)
        m_row = jnp.zeros((1, _LANE), _F32)
        for h in range(H):
            m_row = jnp.where(lane == h, m_scr[h][0:1, :], m_row)
        m_ref[0, 0] = m_row

    @pl.when(jnp.logical_and(last, in_p))
    def _():
        write_state(cp_ref, np_ref, mp_ref)

    @pl.when(jnp.logical_and(last, jnp.logical_not(in_p)))
    def _():
        write_state(cs_ref, ns_ref, ms_ref)


def _mlstm(h, col0, hs, gate_block, lane0, f_bias, norm_g, c_in, n_in, m_in, prev, layer, steps):
    ntok = h.shape[0]
    depth, _, H, dk, dv = c_in.shape
    assert dv % dk == 0 and 2 % (dv // dk) == 0 and lane0 + 2 * H <= _LANE and col0 % (H * dv) == 0
    L = _CHUNK
    bp, bs = steps.bp, steps.bs
    jq, jv = col0 // (H * dk), col0 // (H * dv) + 2 * dk // dv
    tq = lambda j: pl.BlockSpec((L, H * dk), lambda i: (i, jq + j))
    tv = lambda j: pl.BlockSpec((L, H * dv), lambda i: (i, jv + j))
    sseq, pseq = steps.sample_seq, steps.prompt_seq
    in_specs = [tq(0), tq(1), tv(0), tv(1),
                pl.BlockSpec((L, _LANE), lambda i: (i, gate_block)),
                pl.BlockSpec((1, 1, _LANE), lambda i: (layer, 0, 0)),
                pl.BlockSpec((1, H * dv), lambda i: (0, 0)),
                pl.BlockSpec((1, 1, H, dk, dv), lambda i: (layer, sseq(i), 0, 0, 0)),
                pl.BlockSpec((1, 1, H, dk), lambda i: (layer, sseq(i), 0, 0)),
                pl.BlockSpec((1, 1, 1, H), lambda i: (layer, sseq(i), 0, 0))]
    tails = [(H, dk, dv), (H, dk), (1, _LANE)]
    shapes, pspecs, aliases, pargs = _state_outputs(
        prev, [(depth, bp) + t for t in tails], [(depth, bs) + t for t in tails], len(in_specs), 1)
    state_specs = []
    for t in tails:
        zeros = (0,) * len(t)
        state_specs.append(pl.BlockSpec((1, 1) + t, lambda i, z=zeros: (layer, pseq(i)) + z))
        state_specs.append(pl.BlockSpec((1, 1) + t, lambda i, z=zeros: (layer, sseq(i)) + z))
    out = pl.pallas_call(
        functools.partial(_mlstm_kernel, H=H, dk=dk, dv=dv, lane0=lane0, steps=steps),
        grid=(steps.n,),
        in_specs=in_specs + pspecs,
        out_specs=[pl.BlockSpec((L, H * dv), lambda i: (i, 0))] + state_specs,
        out_shape=[jax.ShapeDtypeStruct((ntok, H * dv), _BF16)] + shapes,
        input_output_aliases=aliases,
        scratch_shapes=[pltpu.VMEM((H, dk, dv), _F32), pltpu.VMEM((H, dk), _F32),
                        pltpu.VMEM((H, _SUBLANE, _LANE), _F32)],
        compiler_params=_params("arbitrary"),
        name="mlstm_mixer",
    )(h, h, h, h, hs, f_bias, norm_g.reshape(1, H * dv), c_in, n_in,
      m_in.reshape(m_in.shape[0], m_in.shape[1], 1, H), *pargs)
    return out[0], out[1:]


def _ret_kernel(lg_ref, q_ref, k_ref, v_ref, og_ref, cos_ref, sin_ref, ng_ref, s0_ref,
                *rest, H, dk, dv, steps):
    o_ref, sp_ref, ss_ref, s_scr = rest[-4:]
    L = _CHUNK
    i = pl.program_id(0)
    first, last = steps.first_last(i)
    in_p = steps.in_prompt(i)

    @pl.when(jnp.logical_and(first, in_p))
    def _():
        s_scr[...] = jnp.zeros(s_scr.shape, _F32)

    @pl.when(jnp.logical_and(first, jnp.logical_not(in_p)))
    def _():
        s_scr[...] = s0_ref[0, 0]

    row, col = _chunk_masks()
    incl = row >= col
    tdiff = (row - col).astype(_F32)
    jc = lax.broadcasted_iota(jnp.int32, (L, 1), 0).astype(_F32)
    cos, sin = cos_ref[...], sin_ref[...]

    def rotary(x):
        return x * cos + pltpu.roll(x, dk // 2, 1) * sin

    heads = range(H)
    ks = [slice(h * dk, (h + 1) * dk) for h in heads]
    vs = [slice(h * dv, (h + 1) * dv) for h in heads]
    lg = [lg_ref[h] for h in heads]
    q = [rotary(q_ref[:, ks[h]]) for h in heads]
    k = [rotary(k_ref[:, ks[h]]) * dk ** -0.5 for h in heads]
    vb = [_bf(v_ref[:, vs[h]]) for h in heads]
    qk = [_dot_nt(_bf(q[h]), _bf(k[h])) for h in heads]
    S = [s_scr[h] for h in heads]
    qS = [_dot(_bf(q[h] * jnp.exp(lg[h] * (jc + 1.0))), _bf(S[h])) for h in heads]
    kv = [_dot_tn(_bf(k[h] * jnp.exp(lg[h] * (L - 1.0 - jc))), vb[h]) for h in heads]
    D = [jnp.where(incl, jnp.exp(jnp.where(incl, lg[h] * tdiff, 0.0)), 0.0) for h in heads]
    o = [_dot(_bf(qk[h] * D[h]), vb[h]) + qS[h] for h in heads]
    for h in heads:
        s_scr[h] = jnp.exp(lg[h] * jnp.full((1, 1), float(L), _F32)) * S[h] + kv[h]
    for h in heads:
        o_ref[:, vs[h]] = (_ln_head(o[h], ng_ref[:, vs[h]]) * _silu(og_ref[:, vs[h]])).astype(o_ref.dtype)

    @pl.when(jnp.logical_and(last, in_p))
    def _():
        sp_ref[0, 0] = s_scr[...]

    @pl.when(jnp.logical_and(last, jnp.logical_not(in_p)))
    def _():
        ss_ref[0, 0] = s_scr[...]


def _retention(h, col0, lg, cos_tab, sin_tab, norm_g, s_in, prev, layer, steps):
    ntok = h.shape[0]
    depth, _, H, dk, dv = s_in.shape
    assert dv % dk == 0 and 2 % (dv // dk) == 0 and col0 % (H * dv) == 0
    L = _CHUNK
    jq, jv = col0 // (H * dk), col0 // (H * dv) + 2 * dk // dv
    tq = lambda j: pl.BlockSpec((L, H * dk), lambda i: (i, jq + j))
    tv = lambda j: pl.BlockSpec((L, H * dv), lambda i: (i, jv + j))
    tab = pl.BlockSpec((L, dk), lambda i: (steps.pos_block(i), 0))
    sseq, pseq = steps.sample_seq, steps.prompt_seq
    in_specs = [pl.BlockSpec(memory_space=pltpu.SMEM),
                tq(0), tq(1), tv(0), tv(1), tab, tab,
                pl.BlockSpec((1, H * dv), lambda i: (0, 0)),
                pl.BlockSpec((1, 1, H, dk, dv), lambda i: (layer, sseq(i), 0, 0, 0))]
    shapes, pspecs, aliases, pargs = _state_outputs(
        prev, [(depth, steps.bp, H, dk, dv)], [(depth, steps.bs, H, dk, dv)], len(in_specs), 1)
    out = pl.pallas_call(
        functools.partial(_ret_kernel, H=H, dk=dk, dv=dv, steps=steps),
        grid=(steps.n,),
        in_specs=in_specs + pspecs,
        out_specs=[pl.BlockSpec((L, H * dv), lambda i: (i, 0)),
                   pl.BlockSpec((1, 1, H, dk, dv), lambda i: (layer, pseq(i), 0, 0, 0)),
                   pl.BlockSpec((1, 1, H, dk, dv), lambda i: (layer, sseq(i), 0, 0, 0))],
        out_shape=[jax.ShapeDtypeStruct((ntok, H * dv), _BF16)] + shapes,
        input_output_aliases=aliases,
        scratch_shapes=[pltpu.VMEM((H, dk, dv), _F32)],
        compiler_params=_params("arbitrary"),
        name="retention_mixer",
    )(lg, h, h, h, h, cos_tab, sin_tab, norm_g.reshape(1, H * dv), s_in, *pargs)
    return out[0], out[1:]


def _rotary_tables(tp, ts, dk):
    half = dk // 2
    freq = _ROPE_BASE ** (-jnp.arange(half, dtype=_F32) / half)
    pos = jnp.concatenate([0.0 + jnp.arange(tp, dtype=_F32), float(_PAST_LEN) + jnp.arange(ts, dtype=_F32)])
    ang = pos[:, None] * freq[None, :]
    cos, sin = jnp.cos(ang), jnp.sin(ang)
    return jnp.concatenate([cos, cos], axis=1), jnp.concatenate([-sin, sin], axis=1)


def kernel(x_prompt, x_sample, state_gdn_S, state_gdn_conv, state_mlstm_C, state_mlstm_n, state_mlstm_m, state_ret_S, w_in, b_in, gdn_conv_w, gdn_A_log, gdn_dt_bias, gdn_norm_g, mlstm_f_bias, mlstm_norm_g, ret_norm_g, w_branch, w_out, ln1_g, ln1_b, ln2_g, ln2_b, w_ffn_up, w_ffn_down):
    bp, tp, D = x_prompt.shape
    bs, ts, _ = x_sample.shape
    depth = w_in.shape[0]
    alpha = (2 * depth) ** 0.25
    Hg, gdk, gdv = state_gdn_S.shape[2:]
    Hm, mdk, mdv = state_mlstm_C.shape[2:]
    Hr, rdk, rdv = state_ret_S.shape[2:]
    nconv = state_gdn_conv.shape[2]
    steps = _Steps(bp, tp, bs, ts)
    np_tok = bp * tp

    widths = (2 * Hg * gdk + Hg * gdv, Hg * gdv, Hg, Hg,
              Hm * mdk, Hm * mdk, Hm * mdv, Hm * mdv, Hm, Hm,
              Hr * rdk, Hr * rdk, Hr * rdv, Hr * rdv, 3 * D)
    offs = [0]
    for w_ in widths:
        offs.append(offs[-1] + w_)
    o_qkv, _, o_b, o_a, o_mq, _, _, _, o_mi, o_mf, o_rq, _, _, _, _, o_end = offs
    assert o_end == w_in.shape[2]
    qkv_w = widths[0]
    wide = ((o_qkv, o_b), (o_mq, o_mi), (o_rq, o_end))
    ml_col0 = o_b - o_qkv
    ret_col0 = ml_col0 + (o_mi - o_mq)
    gate_col0 = ret_col0 + sum(widths[10:14])

    gdn_blk, gdn_lane = o_b // _LANE, o_b % _LANE
    ml_blk, ml_lane = o_mi // _LANE, o_mi % _LANE
    assert o_a == o_b + Hg and o_mf == o_mi + Hm

    def wide_cols(a):
        return jnp.concatenate([a[..., lo:hi] for lo, hi in wide], axis=-1)

    cos_tab, sin_tab = _rotary_tables(tp, ts, rdk)
    lg = jnp.log(1.0 - 2.0 ** (-5.0 - jnp.arange(Hr, dtype=_F32)))

    w_in_t = jnp.swapaxes(w_in, 1, 2)
    w_wide, b_wide = _regroup_weights(w_in_t, wide), wide_cols(b_in)
    w_br, w_o, w_dn = _bf(w_branch), _bf(w_out), _bf(w_ffn_down)
    la = gdn_lane + Hg
    gp = jnp.zeros((depth, 2, _LANE), _F32)
    gp = gp.at[:, 0, la:la + Hg].set(gdn_A_log).at[:, 1, la:la + Hg].set(gdn_dt_bias)
    lf = ml_lane + Hm
    fb = jnp.zeros((depth, 1, _LANE), _F32).at[:, 0, lf:lf + Hm].set(mlstm_f_bias)

    x = jnp.concatenate([x_prompt.reshape(np_tok, D), x_sample.reshape(bs * ts, D)], axis=0)
    xb = x.astype(_BF16)
    g_st = m_st = r_st = None
    conv_p, conv_s = [], []
    for l in range(depth):
        h = _matmul(xb, w_wide, l, bias=b_wide[l], name="in_proj")
        if l == 0:
            hs = _small_proj(x, w_in_t, b_in, l, (gdn_blk, ml_blk))
        oa, g_st = _gdn(h, hs, 0, gdn_lane, gdn_conv_w[l], gp, gdn_norm_g[l], state_gdn_S, state_gdn_conv, g_st, l, steps)
        tails = h.reshape(steps.n, _CHUNK, -1)[:, _CHUNK - nconv:, :qkv_w]
        conv_p.append(tails[steps.ncp - 1:steps.np_steps:steps.ncp])
        conv_s.append(tails[steps.np_steps + steps.ncs - 1::steps.ncs])
        ob, m_st = _mlstm(h, ml_col0, hs, 1, ml_lane, fb, mlstm_norm_g[l],
                          state_mlstm_C, state_mlstm_n, state_mlstm_m, m_st, l, steps)
        oc, r_st = _retention(h, ret_col0, lg, cos_tab, sin_tab, ret_norm_g[l], state_ret_S, r_st, l, steps)

        merged = _merge(oa, ob, oc, w_br, l, h, gate_col0)
        z1 = _matmul(merged, w_o, l, res=x, alpha=alpha, name="out_proj")
        x, xb = _layernorm(z1, ln1_g[l], ln1_b[l])

        act = _ffn_up(xb, w_ffn_up, l)
        z2 = _matmul(act, w_dn, l, res=x, alpha=alpha, bm=512, bn=512, name="ffn_down")
        if l + 1 < depth:
            x, xb, hs = _layernorm(z2, ln2_g[l], ln2_b[l], gate=(w_in_t, b_in, l + 1, (gdn_blk, ml_blk)))
        else:
            x, xb = _layernorm(z2, ln2_g[l], ln2_b[l])

    gS_p, gS_s = g_st
    mC_p, mC_s, mn_p, mn_s, mm_p, mm_s = m_st
    rS_p, rS_s = r_st
    yp = x[:np_tok].reshape(bp, tp, D)
    ys = x[np_tok:].reshape(bs, ts, D)
    return (yp, ys,
            gS_p, jnp.stack(conv_p), mC_p, mn_p, mm_p[:, :, 0, :Hm], rS_p,
            gS_s, jnp.stack(conv_s), mC_s, mn_s, mm_s[:, :, 0, :Hm], rS_s)
```
